```python
import math, functools
import jax, jax.numpy as jnp
from jax import lax
import numpy as np

D_MODEL = 4096
BATCH = 4
SEQ = 2048
DEPTH = 2
DEC_BATCH = 128
DEC_SEQ = 4
PAST_LEN = 16384
PAGE_SIZE = 128

N_EVEN = (DEPTH + 1) // 2
N_ODD = DEPTH // 2

D_MIX = D_MODEL
D_A = D_MIX // 2
A_GROUP_DIM = 128
A_GROUPS = D_A // A_GROUP_DIM
A_CHUNK = 128
B_VDIM = 128
B_HEADS = (D_MIX // 2) // B_VDIM
B_NOPE = 128
B_ROPE = 64
Q_LORA = 768
KV_LORA = 512
ROPE_THETA = 10000.0
B_QBLOCK = 128
B_SCALE = (B_NOPE + B_ROPE) ** -0.5
D_C = D_MIX // 2
C_WIDTH = 3
DN_DK = 128
DN_DV = 128
DN_HEADS = (D_MIX // 2) // DN_DV
DN_CONV = 4
DN_CHUNK = 64
DN_QKV = 2 * DN_HEADS * DN_DK + DN_HEADS * DN_DV
D_FF = 11008
EPS = 1e-6
F32 = jnp.float32

N_IN_EVEN = 2 * D_A + Q_LORA + KV_LORA + B_ROPE
N_IN_ODD = 3 * D_C + DN_QKV + DN_HEADS * DN_DV + 2 * DN_HEADS
D_OUT_EVEN = D_A + B_HEADS * B_VDIM
D_OUT_ODD = D_C + DN_HEADS * DN_DV

kernel_name = 'hybrid_gmlp_mla_shortconv_gdn_macaron_step'


def rmsnorm(x, g):
    xf = x.astype(F32)
    y = xf * lax.rsqrt(jnp.mean(xf * xf, axis=-1, keepdims=True) + EPS)
    return (y * g.astype(F32)).astype(x.dtype)


def l2norm(x):
    return x * lax.rsqrt(jnp.sum(x * x, axis=-1, keepdims=True) + EPS)


def ffn_half(x, g, wg, wu, wd):
    h = rmsnorm(x, g)
    return x + 0.5 * ((jax.nn.silu(h @ wg) * (h @ wu)) @ wd)


def rope(x, pos):
    half = B_ROPE // 2
    inv = ROPE_THETA ** (-jnp.arange(half, dtype=F32) / half)
    ang = pos[:, None] * inv[None, :]
    cos = jnp.cos(ang)[None, :, None, :]
    sin = jnp.sin(ang)[None, :, None, :]
    xf = x.astype(F32)
    x1, x2 = xf[..., :half], xf[..., half:]
    return jnp.concatenate([x1 * cos - x2 * sin, x2 * cos + x1 * sin], axis=-1).astype(x.dtype)


def causal_dwconv(x, buf, w):
    W = w.shape[0]
    L = x.shape[1]
    xp = jnp.concatenate([buf.astype(x.dtype), x], axis=1)
    y = w[0] * xp[:, 0:L]
    for j in range(1, W):
        y = y + w[j] * xp[:, j:j + L]
    return y, xp[:, L:]


def gmlp_mixer(z_a, v_norm, ws, bs):
    a = jax.nn.gelu(z_a)
    u, v = jnp.split(a, 2, axis=-1)
    v = rmsnorm(v, v_norm)
    Bn, L, _ = v.shape
    n = min(L, A_CHUNK)
    mask = jnp.tril(jnp.ones((n, n), dtype=bool))
    w = jnp.where(mask, ws[:, :n, :n], 0)
    vc = v.reshape(Bn, L // n, n, A_GROUPS, A_GROUP_DIM)
    s = jnp.einsum('gts,bcsgd->bctgd', w, vc) + bs[:, :n].T[None, None, :, :, None]
    return u * s.reshape(Bn, L, D_A), v


def mla_prompt_attend(q_abs, q_pe, c, k_pe):
    Bn, L, H, R = q_abs.shape
    nb = L // B_QBLOCK
    qa = q_abs.reshape(Bn, nb, B_QBLOCK, H, R).swapaxes(0, 1)
    qp = q_pe.reshape(Bn, nb, B_QBLOCK, H, B_ROPE).swapaxes(0, 1)
    kpos = jnp.arange(L)

    def block(args):
        i, qa_i, qp_i = args
        s = (jnp.einsum('bqhr,bkr->bhqk', qa_i, c)
             + jnp.einsum('bqhd,bkd->bhqk', qp_i, k_pe)).astype(F32) * B_SCALE
        qpos = i * B_QBLOCK + jnp.arange(B_QBLOCK)
        s = jnp.where(kpos[None, :] <= qpos[:, None], s, -jnp.inf)
        p = jax.nn.softmax(s, axis=-1)
        return jnp.einsum('bhqk,bkr->bqhr', p, c.astype(F32)).astype(c.dtype)

    o = lax.map(block, (jnp.arange(nb), qa, qp))
    return o.swapaxes(0, 1).reshape(Bn, L, H, R)


def mla_sample_attend(cache_c, cache_pe, page_table, layer, q_abs, q_pe, c, k_pe):
    T = q_abs.shape[1]
    s = (jnp.einsum('bqhr,bkr->bhqk', q_abs, c)
         + jnp.einsum('bqhd,bkd->bhqk', q_pe, k_pe)).astype(F32) * B_SCALE
    s = jnp.where(jnp.tril(jnp.ones((T, T), dtype=bool)), s, -jnp.inf)
    m = jnp.max(s, axis=-1)
    p = jnp.exp(s - m[..., None])
    l = jnp.sum(p, axis=-1)
    acc = jnp.einsum('bhqk,bkr->bhqr', p, c.astype(F32))

    def step(carry, ids):
        m, l, acc = carry
        kc = cache_c[layer, ids]
        kp = cache_pe[layer, ids]
        s = (jnp.einsum('bqhr,bkr->bhqk', q_abs, kc)
             + jnp.einsum('bqhd,bkd->bhqk', q_pe, kp)).astype(F32) * B_SCALE
        m_new = jnp.maximum(m, jnp.max(s, axis=-1))
        corr = jnp.exp(m - m_new)
        p = jnp.exp(s - m_new[..., None])
        l = l * corr + jnp.sum(p, axis=-1)
        acc = acc * corr[..., None] + jnp.einsum('bhqk,bkr->bhqr', p, kc.astype(F32))
        return (m_new, l, acc), None

    (m, l, acc), _ = lax.scan(step, (m, l, acc), page_table.T)
    o = acc / l[..., None]
    return o.transpose(0, 2, 1, 3).astype(q_abs.dtype)


def even_mixer(h, pos, attend, w_in, v_norm, ws, bs, q_norm, w_uq, kv_norm, w_uk, w_uv, w_out):
    Bn, L, _ = h.shape
    z = h @ w_in
    z_a, z_q, z_kv, z_r = jnp.split(z, [2 * D_A, 2 * D_A + Q_LORA, 2 * D_A + Q_LORA + KV_LORA], axis=-1)
    a_out, v_rows = gmlp_mixer(z_a, v_norm, ws, bs)
    q = jnp.einsum('blr,rhd->blhd', rmsnorm(z_q, q_norm), w_uq)
    q_abs = jnp.einsum('blhd,rhd->blhr', q[..., :B_NOPE], w_uk)
    q_pe = rope(q[..., B_NOPE:], pos)
    c = rmsnorm(z_kv, kv_norm)
    k_pe = rope(z_r[:, :, None, :], pos)[:, :, 0]
    o_lat = attend(q_abs, q_pe, c, k_pe)
    o = jnp.einsum('blhr,rhv->blhv', o_lat, w_uv).reshape(Bn, L, B_HEADS * B_VDIM)
    y = jnp.concatenate([a_out, o], axis=-1) @ w_out
    return y, v_rows, c, k_pe


def gated_delta_chunked(q, k, v, g, beta, s0, chunk):
    Bn, L, H, DK = q.shape
    DV = v.shape[-1]
    n = L // chunk

    def blk(t):
        t = t.reshape(Bn, n, chunk, H, *t.shape[3:])
        return jnp.moveaxis(t, (1, 3), (0, 2))

    qc, kc, vc, gc, bc = blk(q), blk(k), blk(v), blk(g), blk(beta)
    gcum = jnp.cumsum(gc, axis=-1)
    idx = jnp.arange(chunk)
    incl = idx[:, None] >= idx[None, :]
    strict = idx[:, None] > idx[None, :]
    decay = jnp.exp(jnp.where(incl, gcum[..., :, None] - gcum[..., None, :], -jnp.inf))
    kb = kc * bc[..., None]
    A = jnp.where(strict, jnp.einsum('...id,...jd->...ij', kb, kc) * decay, 0.0)
    eye = jnp.eye(chunk, dtype=F32)
    Tm = lax.linalg.triangular_solve(A + eye, jnp.broadcast_to(eye, A.shape),
                                     left_side=True, lower=True, unit_diagonal=True)
    u = Tm @ (vc * bc[..., None])
    w = Tm @ (kb * jnp.exp(gcum)[..., None])
    qk = jnp.where(incl, jnp.einsum('...id,...jd->...ij', qc, kc) * decay, 0.0)
    q_dec = qc * jnp.exp(gcum)[..., None]
    k_dec = kc * jnp.exp(gcum[..., -1:] - gcum)[..., None]
    g_last = jnp.exp(gcum[..., -1])

    def step(S, xs):
        u_i, w_i, qk_i, qd_i, kd_i, gl_i = xs
        v_new = u_i - w_i @ S
        o = qd_i @ S + qk_i @ v_new
        S = S * gl_i[..., None, None] + jnp.swapaxes(kd_i, -1, -2) @ v_new
        return S, o

    S, o = lax.scan(step, s0, (u, w, qk, q_dec, k_dec, g_last))
    o = jnp.moveaxis(o, (0, 2), (1, 3)).reshape(Bn, L, H, DV)
    return o, S


def odd_mixer(h, buf_c, buf_d, s0, w_in, conv_c_w, conv_d_w, a_log, dt_bias, o_norm, w_out):
    Bn, L, _ = h.shape
    z = h @ w_in
    cuts = [int(t) for t in np.cumsum([D_C, D_C, D_C, DN_QKV, DN_HEADS * DN_DV, DN_HEADS])]
    b_gate, c_gate, xc, z_qkv, z_gate, z_beta, z_a = jnp.split(z, cuts, axis=-1)
    conv_c, new_buf_c = causal_dwconv(c_gate * xc, buf_c, conv_c_w)
    y_c = b_gate * conv_c
    qkv, new_buf_d = causal_dwconv(z_qkv, buf_d, conv_d_w)
    qkv = jax.nn.silu(qkv).astype(F32)
    q, k, v = jnp.split(qkv, [DN_HEADS * DN_DK, 2 * DN_HEADS * DN_DK], axis=-1)
    q = l2norm(q.reshape(Bn, L, DN_HEADS, DN_DK)) * DN_DK ** -0.5
    k = l2norm(k.reshape(Bn, L, DN_HEADS, DN_DK))
    v = v.reshape(Bn, L, DN_HEADS, DN_DV)
    beta = jax.nn.sigmoid(z_beta.astype(F32))
    g = -jnp.exp(a_log.astype(F32)) * jax.nn.softplus(z_a.astype(F32) + dt_bias.astype(F32))
    o, s_new = gated_delta_chunked(q, k, v, g, beta, s0.astype(F32), math.gcd(L, DN_CHUNK))
    o = rmsnorm(o, o_norm) * jax.nn.silu(z_gate.reshape(Bn, L, DN_HEADS, DN_DV).astype(F32))
    o = o.astype(h.dtype).reshape(Bn, L, DN_HEADS * DN_DV)
    y = jnp.concatenate([y_c, o], axis=-1) @ w_out
    return y, new_buf_c, new_buf_d, s_new


def setup_inputs(seed: int = 0) -> dict:
    key = jax.random.key(seed)
    ks = iter(jax.random.split(key, 48))

    def nrm(shape, scale=1.0):
        return jax.random.normal(next(ks), shape, F32) * scale

    def gain(shape):
        return 1.0 + 0.05 * nrm(shape)

    n_pages = PAST_LEN // PAGE_SIZE
    n_pool = (DEC_BATCH * n_pages * 5) // 4
    x_prompt = nrm((BATCH, SEQ, D_MODEL))
    x_sample = nrm((DEC_BATCH, DEC_SEQ, D_MODEL))
    cache_mla_latent = nrm((N_EVEN, n_pool, PAGE_SIZE, KV_LORA))
    cache_mla_krope = nrm((N_EVEN, n_pool, PAGE_SIZE, B_ROPE))
    state_conv_c = nrm((N_ODD, DEC_BATCH, C_WIDTH - 1, D_C))
    state_conv_d = nrm((N_ODD, DEC_BATCH, DN_CONV - 1, DN_QKV))
    state_delta = nrm((N_ODD, DEC_BATCH, DN_HEADS, DN_DK, DN_DV), 0.1)
    perm = jax.random.permutation(next(ks), n_pool)
    page_table = perm[: DEC_BATCH * n_pages].reshape(DEC_BATCH, n_pages).astype(jnp.int32)

    dt = jnp.exp(jax.random.uniform(next(ks), (N_ODD, DN_HEADS), F32) * (math.log(0.1) - math.log(0.001)) + math.log(0.001))
    delta_dt_bias = dt + jnp.log(-jnp.expm1(-dt))
    delta_a_log = jnp.log(jax.random.uniform(next(ks), (N_ODD, DN_HEADS), F32, 1.0, 16.0))

    return {
        'x_prompt': x_prompt,
        'x_sample': x_sample,
        'cache_mla_latent': cache_mla_latent,
        'cache_mla_krope': cache_mla_krope,
        'state_conv_c': state_conv_c,
        'state_conv_d': state_conv_d,
        'state_delta': state_delta,
        'page_table': page_table,
        'ffn1_norm': gain((DEPTH, D_MODEL)),
        'ffn1_w_gate': nrm((DEPTH, D_MODEL, D_FF), D_MODEL ** -0.5),
        'ffn1_w_up': nrm((DEPTH, D_MODEL, D_FF), D_MODEL ** -0.5),
        'ffn1_w_down': nrm((DEPTH, D_FF, D_MODEL), D_FF ** -0.5),
        'mix_norm': gain((DEPTH, D_MODEL)),
        'ffn2_norm': gain((DEPTH, D_MODEL)),
        'ffn2_w_gate': nrm((DEPTH, D_MODEL, D_FF), D_MODEL ** -0.5),
        'ffn2_w_up': nrm((DEPTH, D_MODEL, D_FF), D_MODEL ** -0.5),
        'ffn2_w_down': nrm((DEPTH, D_FF, D_MODEL), D_FF ** -0.5),
        'even_w_in': nrm((N_EVEN, D_MODEL, N_IN_EVEN), D_MODEL ** -0.5),
        'gmlp_v_norm': gain((N_EVEN, D_A)),
        'gmlp_ws': nrm((N_EVEN, A_GROUPS, A_CHUNK, A_CHUNK), A_CHUNK ** -0.5),
        'gmlp_bs': gain((N_EVEN, A_GROUPS, A_CHUNK)),
        'mla_q_norm': gain((N_EVEN, Q_LORA)),
        'mla_w_uq': nrm((N_EVEN, Q_LORA, B_HEADS, B_NOPE + B_ROPE), Q_LORA ** -0.5),
        'mla_kv_norm': gain((N_EVEN, KV_LORA)),
        'mla_w_uk': nrm((N_EVEN, KV_LORA, B_HEADS, B_NOPE), KV_LORA ** -0.5),
        'mla_w_uv': nrm((N_EVEN, KV_LORA, B_HEADS, B_VDIM), KV_LORA ** -0.5),
        'even_w_out': nrm((N_EVEN, D_OUT_EVEN, D_MODEL), D_OUT_EVEN ** -0.5),
        'odd_w_in': nrm((N_ODD, D_MODEL, N_IN_ODD), D_MODEL ** -0.5),
        'conv_c_w': nrm((N_ODD, C_WIDTH, D_C), C_WIDTH ** -0.5),
        'conv_d_w': nrm((N_ODD, DN_CONV, DN_QKV), DN_CONV ** -0.5),
        'delta_a_log': delta_a_log,
        'delta_dt_bias': delta_dt_bias,
        'delta_o_norm': gain((N_ODD, DN_DV)),
        'odd_w_out': nrm((N_ODD, D_OUT_ODD, D_MODEL), D_OUT_ODD ** -0.5),
        'final_norm': gain((D_MODEL,)),
    }


def reference(x_prompt, x_sample, cache_mla_latent, cache_mla_krope, state_conv_c, state_conv_d,
              state_delta, page_table, ffn1_norm, ffn1_w_gate, ffn1_w_up, ffn1_w_down, mix_norm,
              ffn2_norm, ffn2_w_gate, ffn2_w_up, ffn2_w_down, even_w_in, gmlp_v_norm, gmlp_ws, gmlp_bs,
              mla_q_norm, mla_w_uq, mla_kv_norm, mla_w_uk, mla_w_uv, even_w_out, odd_w_in, conv_c_w,
              conv_d_w, delta_a_log, delta_dt_bias, delta_o_norm, odd_w_out, final_norm):
    nb_p = x_prompt.shape[0]
    past_len = page_table.shape[1] * PAGE_SIZE
    pos_p = jnp.arange(x_prompt.shape[1], dtype=F32)
    pos_s = jnp.arange(x_sample.shape[1], dtype=F32) + past_len
    x_p, x_s = x_prompt, x_sample
    lat_p, kr_p, lat_s, kr_s, v_s = [], [], [], [], []
    cc_p, cc_s, cd_p, cd_s, sd_p, sd_s = [], [], [], [], [], []

    for layer in range(DEPTH):
        f1 = (ffn1_norm[layer], ffn1_w_gate[layer], ffn1_w_up[layer], ffn1_w_down[layer])
        f2 = (ffn2_norm[layer], ffn2_w_gate[layer], ffn2_w_up[layer], ffn2_w_down[layer])
        x_p = ffn_half(x_p, *f1)
        x_s = ffn_half(x_s, *f1)
        h_p = rmsnorm(x_p, mix_norm[layer])
        h_s = rmsnorm(x_s, mix_norm[layer])
        if layer % 2 == 0:
            e = layer // 2
            ew = (even_w_in[e], gmlp_v_norm[e], gmlp_ws[e], gmlp_bs[e], mla_q_norm[e], mla_w_uq[e],
                  mla_kv_norm[e], mla_w_uk[e], mla_w_uv[e], even_w_out[e])
            y_p, _, c_p, k_p = even_mixer(h_p, pos_p, mla_prompt_attend, *ew)
            attend_s = functools.partial(mla_sample_attend, cache_mla_latent, cache_mla_krope, page_table, e)
            y_s, vr_s, c_s, k_s = even_mixer(h_s, pos_s, attend_s, *ew)
            lat_p.append(c_p); kr_p.append(k_p); lat_s.append(c_s); kr_s.append(k_s); v_s.append(vr_s)
        else:
            o = layer // 2
            ow = (odd_w_in[o], conv_c_w[o], conv_d_w[o], delta_a_log[o], delta_dt_bias[o],
                  delta_o_norm[o], odd_w_out[o])
            zc = jnp.zeros((nb_p, C_WIDTH - 1, D_C), x_p.dtype)
            zd = jnp.zeros((nb_p, DN_CONV - 1, DN_QKV), x_p.dtype)
            zs = jnp.zeros((nb_p, DN_HEADS, DN_DK, DN_DV), F32)
            y_p, bc_p, bd_p, st_p = odd_mixer(h_p, zc, zd, zs, *ow)
            y_s, bc_s, bd_s, st_s = odd_mixer(h_s, state_conv_c[o], state_conv_d[o], state_delta[o], *ow)
            cc_p.append(bc_p); cc_s.append(bc_s); cd_p.append(bd_p); cd_s.append(bd_s)
            sd_p.append(st_p); sd_s.append(st_s)
        x_p = x_p + y_p
        x_s = x_s + y_s
        x_p = ffn_half(x_p, *f2)
        x_s = ffn_half(x_s, *f2)

    y_prompt = rmsnorm(x_p, final_norm)
    y_sample = rmsnorm(x_s, final_norm)
    return (y_prompt, y_sample, jnp.stack(lat_p), jnp.stack(kr_p), jnp.stack(lat_s), jnp.stack(kr_s),
            jnp.stack(v_s), jnp.stack(cc_p), jnp.stack(cc_s), jnp.stack(cd_p), jnp.stack(cd_s),
            jnp.stack(sd_p), jnp.stack(sd_s))
```

```python
import functools
import math

import jax
import jax.numpy as jnp
from jax import lax
from jax.experimental import pallas as pl
from jax.experimental.pallas import tpu as pltpu

F32 = jnp.float32
BF16 = jnp.bfloat16
EPS = 1e-6

PAGE = 128
GROUP = 128
N_HEADS = 16
NOPE = 128
ROPE = 64
HEAD_PAD = 256
KV_LORA = 512
V_DIM = 128
DK = 128
DV = 128
ROPE_THETA = 10000.0
ATT_SCALE = (NOPE + ROPE) ** -0.5
DN_CHUNK = 64
SAMPLE_CHUNK = 8
PAGES_PER_STEP = 16

VMEM_CAP_BYTES = 60 * 1024 * 1024


def _pick(n, target, mult=8):
    for d in range(min(n, target), 0, -1):
        if n % d == 0 and d % mult == 0:
            return d
    return n


def _params(sem, vmem_bytes):
    return pltpu.CompilerParams(
        dimension_semantics=sem,
        vmem_limit_bytes=int(min(VMEM_CAP_BYTES, max(vmem_bytes, 32 * 1024 * 1024))))


def _sigmoid(x):
    return 1.0 / (1.0 + jnp.exp(-x))


def _silu(x):
    return x * _sigmoid(x)


def _dot(a, b):
    return jnp.dot(a, b, preferred_element_type=F32)


def _dot_nt(a, b):
    return lax.dot_general(a, b, (((1,), (1,)), ((), ())), preferred_element_type=F32)


def _dot_tn(a, b):
    return lax.dot_general(a, b, (((0,), (0,)), ((), ())), preferred_element_type=F32)


def _split3(x):
    hi = x.astype(BF16)
    r = x - hi.astype(F32)
    mid = r.astype(BF16)
    lo = (r - mid.astype(F32)).astype(BF16)
    return hi, mid, lo


def _dot_hp(a, b):
    a_hi, a_mid, _ = _split3(a)
    b_hi, b_mid, _ = _split3(b)
    return _dot(a_hi, b_hi) + (_dot(a_hi, b_mid) + _dot(a_mid, b_hi))


def _rms_body(x_ref, g_ref, o_ref):
    x = x_ref[...]
    y = x * lax.rsqrt(jnp.mean(x * x, axis=-1, keepdims=True) + EPS)
    o_ref[...] = (y * g_ref[...]).astype(o_ref.dtype)


def _rms(x, g, out_dtype):
    m, d = x.shape
    tm = _pick(m, 256, 16)
    return pl.pallas_call(
        _rms_body,
        grid=(m // tm,),
        in_specs=[pl.BlockSpec((tm, d), lambda i: (i, 0)),
                  pl.BlockSpec((1, d), lambda i: (0, 0))],
        out_specs=pl.BlockSpec((tm, d), lambda i: (i, 0)),
        out_shape=jax.ShapeDtypeStruct((m, d), out_dtype),
        compiler_params=_params(("parallel",), 6 * tm * d * 4),
        name="rmsnorm",
    )(x, g.reshape(1, d))


def _mm_body(*refs, n_w, n_x, epi):
    lhs = refs[0][...]
    accs = [_dot(lhs, refs[1 + k][...]) for k in range(n_w)]
    xs = [refs[1 + n_w + k][...] for k in range(n_x)]
    o_ref = refs[1 + n_w + n_x]
    o_ref[...] = epi(accs, xs).astype(o_ref.dtype)


def _epi_plain(accs, xs):
    return accs[0]


def _epi_swiglu(accs, xs):
    return _silu(accs[0]) * accs[1]


def _epi_resid(accs, xs, scale):
    return xs[0] + scale * accs[0]


def _epi_rope(accs, xs):
    return accs[0] * xs[0] + accs[1] * xs[1]


def _mm(lhs, ws, extras, epi, out_dtype, tm_target, tn, name):
    m, k = lhs.shape
    n = ws[0].shape[1]
    tm = _pick(m, tm_target, 16)
    assert n % tn == 0, (n, tn)
    in_specs = [pl.BlockSpec((tm, k), lambda i, j: (i, 0))]
    in_specs += [pl.BlockSpec((k, tn), lambda i, j: (0, j)) for _ in ws]
    for _, kind in extras:
        if kind == "ij":
            in_specs.append(pl.BlockSpec((tm, tn), lambda i, j: (i, j)))
        else:
            in_specs.append(pl.BlockSpec((tm, tn), lambda i, j: (i, 0)))
    vmem = 2 * (tm * k * 2 + len(ws) * k * tn * 2 + (len(extras) + 1) * tm * tn * 4)
    vmem += (len(ws) + 1) * tm * tn * 4 + (4 << 20)
    return pl.pallas_call(
        functools.partial(_mm_body, n_w=len(ws), n_x=len(extras), epi=epi),
        grid=(m // tm, n // tn),
        in_specs=in_specs,
        out_specs=pl.BlockSpec((tm, tn), lambda i, j: (i, j)),
        out_shape=jax.ShapeDtypeStruct((m, n), out_dtype),
        compiler_params=_params(("parallel", "arbitrary"), vmem),
        name=name,
    )(lhs, *ws, *[a for a, _ in extras])


def _headmm_body(l_ref, w_ref, o_ref):
    o_ref[...] = _dot(l_ref[...], w_ref[...]).astype(o_ref.dtype)


def _headmm(lhs, w, out_dtype, name):
    m = lhs.shape[0]
    nh, kd, nd = w.shape
    return pl.pallas_call(
        _headmm_body,
        grid=(nh,),
        in_specs=[pl.BlockSpec((m, kd), lambda h: (0, h)),
                  pl.BlockSpec((None, kd, nd), lambda h: (h, 0, 0))],
        out_specs=pl.BlockSpec((m, nd), lambda h: (0, h)),
        out_shape=jax.ShapeDtypeStruct((m, nh * nd), out_dtype),
        compiler_params=_params(("parallel",), 0),
        name=name,
    )(lhs, w)


def _ffn_half(x, g, wg, wu, wd):
    d, f = wg.shape
    fp = -(-f // 512) * 512
    wg_b = jnp.pad(wg.astype(BF16), ((0, 0), (0, fp - f)))
    wu_b = jnp.pad(wu.astype(BF16), ((0, 0), (0, fp - f)))
    wd_b = jnp.pad(wd.astype(BF16), ((0, fp - f), (0, 0)))
    h = _rms(x, g, BF16)
    a = _mm(h, [wg_b, wu_b], [], _epi_swiglu, BF16, 1088, 512, "ffn_gate_up")
    return _mm(a, [wd_b], [(x, "ij")], functools.partial(_epi_resid, scale=0.5),
               F32, 544, 256, "ffn_down")


def _gmlp_body(z_ref, w_ref, b_ref, vn_ref, a_ref, v_ref, *, n_prompt_tiles, t_sample, d_a):
    i = pl.program_id(0)
    z = z_ref[...]
    c0 = math.sqrt(2.0 / math.pi)
    a = 0.5 * z * (1.0 + jnp.tanh(c0 * (z + 0.044715 * (z * z * z))))
    u = a[:, :d_a]
    v = a[:, d_a:]
    v = v * lax.rsqrt(jnp.mean(v * v, axis=-1, keepdims=True) + EPS) * vn_ref[...]
    v_ref[...] = v
    row = lax.broadcasted_iota(jnp.int32, (GROUP, GROUP), 0)
    col = lax.broadcasted_iota(jnp.int32, (GROUP, GROUP), 1)
    same_seq = (row // t_sample) == (col // t_sample)
    mask = (col <= row) & (same_seq | (i < n_prompt_tiles))
    for g in range(d_a // GROUP):
        sl = slice(g * GROUP, (g + 1) * GROUP)
        w = jnp.where(mask, w_ref[g], 0.0).astype(BF16)
        s = _dot(w, v[:, sl].astype(BF16)) + b_ref[:, sl]
        a_ref[:, sl] = (u[:, sl] * s).astype(a_ref.dtype)


def _gmlp(z_a, ws, bs, v_norm, n_prompt, t_sample):
    m = z_a.shape[0]
    d_a = z_a.shape[1] // 2
    ng = d_a // GROUP
    assert n_prompt % GROUP == 0 and (m - n_prompt) % GROUP == 0 and GROUP % t_sample == 0
    npt = n_prompt // GROUP
    rep = GROUP // t_sample
    w_all = jnp.stack([ws, jnp.tile(ws[:, :t_sample, :t_sample], (1, rep, rep))])
    b_p = jnp.repeat(bs.T, GROUP, axis=1)
    b_s = jnp.repeat(jnp.tile(bs[:, :t_sample].T, (rep, 1)), GROUP, axis=1)
    b_all = jnp.stack([b_p, b_s])

    def sel(i):
        return jnp.where(i < npt, 0, 1)

    return pl.pallas_call(
        functools.partial(_gmlp_body, n_prompt_tiles=npt, t_sample=t_sample, d_a=d_a),
        grid=(m // GROUP,),
        in_specs=[pl.BlockSpec((GROUP, 2 * d_a), lambda i: (i, 0)),
                  pl.BlockSpec((None, ng, GROUP, GROUP), lambda i: (sel(i), 0, 0, 0)),
                  pl.BlockSpec((None, GROUP, d_a), lambda i: (sel(i), 0, 0)),
                  pl.BlockSpec((1, d_a), lambda i: (0, 0))],
        out_specs=[pl.BlockSpec((GROUP, d_a), lambda i: (i, 0)),
                   pl.BlockSpec((GROUP, d_a), lambda i: (i, 0))],
        out_shape=[jax.ShapeDtypeStruct((m, d_a), BF16),
                   jax.ShapeDtypeStruct((m, d_a), F32)],
        compiler_params=_params(("parallel",), 0),
        name="gmlp",
    )(z_a, w_all, b_all, v_norm.reshape(1, d_a))


def _mla_prep_body(z_ref, qg_ref, kg_ref, cos_ref, sin_ref, qn_ref, c_ref, kpe_ref, ck_ref, *, q_lora):
    z = z_ref[...]
    zq = z[:, :q_lora]
    qn_ref[...] = (zq * lax.rsqrt(jnp.mean(zq * zq, axis=-1, keepdims=True) + EPS)
                   * qg_ref[...]).astype(qn_ref.dtype)
    zkv = z[:, q_lora:q_lora + KV_LORA]
    c = zkv * lax.rsqrt(jnp.mean(zkv * zkv, axis=-1, keepdims=True) + EPS) * kg_ref[...]
    c_ref[...] = c
    r0 = q_lora + KV_LORA
    kpe = z[:, r0:r0 + 128] * cos_ref[...] + z[:, r0 + 128:r0 + 256] * sin_ref[...]
    kpe_ref[...] = kpe[:, :ROPE]
    ck_ref[:, :KV_LORA] = c.astype(ck_ref.dtype)
    ck_ref[:, KV_LORA:] = kpe.astype(ck_ref.dtype)


def _mla_prep(z_m, q_norm, kv_norm, cos128, sin128, q_lora):
    m, nz = z_m.shape
    tm = _pick(m, 544, 16)
    row = lambda i: (i, 0)
    fix = lambda i: (0, 0)
    return pl.pallas_call(
        functools.partial(_mla_prep_body, q_lora=q_lora),
        grid=(m // tm,),
        in_specs=[pl.BlockSpec((tm, nz), row),
                  pl.BlockSpec((1, q_lora), fix),
                  pl.BlockSpec((1, KV_LORA), fix),
                  pl.BlockSpec((tm, 128), row),
                  pl.BlockSpec((tm, 128), row)],
        out_specs=[pl.BlockSpec((tm, q_lora), row),
                   pl.BlockSpec((tm, KV_LORA), row),
                   pl.BlockSpec((tm, ROPE), row),
                   pl.BlockSpec((tm, KV_LORA + 128), row)],
        out_shape=[jax.ShapeDtypeStruct((m, q_lora), BF16),
                   jax.ShapeDtypeStruct((m, KV_LORA), F32),
                   jax.ShapeDtypeStruct((m, ROPE), F32),
                   jax.ShapeDtypeStruct((m, KV_LORA + 128), BF16)],
        compiler_params=_params(("parallel",), 0),
        name="mla_prep",
    )(z_m, q_norm.reshape(1, q_lora), kv_norm.reshape(1, KV_LORA), cos128, sin128)


def _attn_prompt_body(q_ref, k_ref, v_ref, o_ref, m_ref, l_ref, acc_ref, *, tq, tk):
    qi = pl.program_id(1)
    ki = pl.program_id(2)
    nk = pl.num_programs(2)

    @pl.when(ki == 0)
    def _():
        m_ref[...] = jnp.full(m_ref.shape, -jnp.inf, F32)
        l_ref[...] = jnp.zeros(l_ref.shape, F32)
        acc_ref[...] = jnp.zeros(acc_ref.shape, F32)

    @pl.when(ki * tk <= qi * tq + (tq - 1))
    def _():
        rows = qi * tq + lax.broadcasted_iota(jnp.int32, (tq, tk), 0)
        cols = ki * tk + lax.broadcasted_iota(jnp.int32, (tq, tk), 1)
        mask = cols <= rows
        for h in range(N_HEADS):
            qs = slice(h * HEAD_PAD, (h + 1) * HEAD_PAD)
            vs = slice(h * V_DIM, (h + 1) * V_DIM)
            s = _dot_nt(q_ref[:, qs], k_ref[:, qs]) * ATT_SCALE
            s = jnp.where(mask, s, -jnp.inf)
            m_old = m_ref[h][:, :1]
            m_new = jnp.maximum(m_old, jnp.max(s, axis=-1, keepdims=True))
            corr = jnp.exp(m_old - m_new)
            p = jnp.exp(s - m_new)
            l_ref[h] = jnp.broadcast_to(l_ref[h][:, :1] * corr + jnp.sum(p, axis=-1, keepdims=True),
                                        (tq, 128))
            acc_ref[:, vs] = acc_ref[:, vs] * corr + _dot(p.astype(BF16), v_ref[:, vs])
            m_ref[h] = jnp.broadcast_to(m_new, (tq, 128))

    @pl.when(ki == nk - 1)
    def _():
        for h in range(N_HEADS):
            vs = slice(h * V_DIM, (h + 1) * V_DIM)
            o_ref[:, vs] = (acc_ref[:, vs] / l_ref[h][:, :1]).astype(o_ref.dtype)


def _attn_prompt(q_cat, kv, n_seq, seq_len):
    tq = _pick(seq_len, 256, 16)
    tk = tq
    nq = seq_len // tq
    qw = N_HEADS * HEAD_PAD
    vw = N_HEADS * V_DIM
    assert qw % vw == 0
    v_blk = qw // vw

    def q_map(b, qi, ki):
        return (b * nq + qi, 0)

    def k_map(b, qi, ki):
        return (b * nq + jnp.minimum(ki, qi), 0)

    def v_map(b, qi, ki):
        return (b * nq + jnp.minimum(ki, qi), v_blk)

    return pl.pallas_call(
        functools.partial(_attn_prompt_body, tq=tq, tk=tk),
        grid=(n_seq, nq, nq),
        in_specs=[pl.BlockSpec((tq, qw), q_map),
                  pl.BlockSpec((tk, qw), k_map),
                  pl.BlockSpec((tk, vw), v_map)],
        out_specs=pl.BlockSpec((tq, vw), q_map),
        out_shape=jax.ShapeDtypeStruct((n_seq * seq_len, vw), BF16),
        scratch_shapes=[pltpu.VMEM((N_HEADS, tq, 128), F32),
                        pltpu.VMEM((N_HEADS, tq, 128), F32),
                        pltpu.VMEM((tq, vw), F32)],
        compiler_params=_params(("parallel", "parallel", "arbitrary"), 0),
        name="attn_prompt",
    )(q_cat, kv, kv)


def _decode_body(pt_ref, qa_ref, qp_ref, sc_ref, sk_ref, *refs, pps, t_new):
    lat_refs = refs[:pps]
    kr_refs = refs[pps:2 * pps]
    o_ref, m_ref, l_ref, acc_ref, kc_ref, kp_ref = refs[2 * pps:]
    c = pl.program_id(1)
    nc = pl.num_programs(1)
    qa = qa_ref[...]
    qp = qp_ref[...]
    nrow = qa.shape[0]

    def attend(kc, kp, mask):
        s = (_dot_nt(qa, kc) + _dot_nt(qp, kp)) * ATT_SCALE
        if mask is not None:
            s = jnp.where(mask, s, -jnp.inf)
        m_old = m_ref[:, :1]
        m_new = jnp.maximum(m_old, jnp.max(s, axis=-1, keepdims=True))
        corr = jnp.exp(m_old - m_new)
        p = jnp.exp(s - m_new)
        l_ref[...] = jnp.broadcast_to(l_ref[:, :1] * corr + jnp.sum(p, axis=-1, keepdims=True),
                                      l_ref.shape)
        acc_ref[...] = acc_ref[...] * corr + _dot(p.astype(BF16), kc)
        m_ref[...] = jnp.broadcast_to(m_new, m_ref.shape)

    @pl.when(c == 0)
    def _():
        m_ref[...] = jnp.full(m_ref.shape, -jnp.inf, F32)
        l_ref[...] = jnp.zeros(l_ref.shape, F32)
        acc_ref[...] = jnp.zeros(acc_ref.shape, F32)
        key = lax.broadcasted_iota(jnp.int32, (nrow, PAGE), 1)
        tok = lax.broadcasted_iota(jnp.int32, (nrow, PAGE), 0) // N_HEADS
        attend(sc_ref[...].astype(BF16), sk_ref[...].astype(BF16), key <= tok)

    for k in range(pps):
        kc_ref[k * PAGE:(k + 1) * PAGE, :] = lat_refs[k][...].astype(BF16)
        kp_ref[k * PAGE:(k + 1) * PAGE, :] = kr_refs[k][...].astype(BF16)
    attend(kc_ref[...], kp_ref[...], None)

    @pl.when(c == nc - 1)
    def _():
        o_ref[...] = (acc_ref[...] / l_ref[:, :1]).astype(o_ref.dtype)


def _decode(q_abs, q_pe, self_c, self_k, cache_lat, cache_kr, page_table, layer, t_new):
    bs, nrow, _ = q_abs.shape
    n_pages = page_table.shape[1]
    pps = _pick(n_pages, PAGES_PER_STEP, 1)
    nc = n_pages // pps

    def fix(b, c, pt):
        return (b, 0, 0)

    def page_map(k):
        return lambda b, c, pt: (layer, pt[b, c * pps + k], 0, 0)

    in_specs = [pl.BlockSpec((None, nrow, KV_LORA), fix),
                pl.BlockSpec((None, nrow, ROPE), fix),
                pl.BlockSpec((None, PAGE, KV_LORA), fix),
                pl.BlockSpec((None, PAGE, ROPE), fix)]
    in_specs += [pl.BlockSpec((None, None, PAGE, KV_LORA), page_map(k)) for k in range(pps)]
    in_specs += [pl.BlockSpec((None, None, PAGE, ROPE), page_map(k)) for k in range(pps)]
    grid_spec = pltpu.PrefetchScalarGridSpec(
        num_scalar_prefetch=1,
        grid=(bs, nc),
        in_specs=in_specs,
        out_specs=pl.BlockSpec((None, nrow, KV_LORA), fix),
        scratch_shapes=[pltpu.VMEM((nrow, 128), F32),
                        pltpu.VMEM((nrow, 128), F32),
                        pltpu.VMEM((nrow, KV_LORA), F32),
                        pltpu.VMEM((pps * PAGE, KV_LORA), BF16),
                        pltpu.VMEM((pps * PAGE, ROPE), BF16)])
    return pl.pallas_call(
        functools.partial(_decode_body, pps=pps, t_new=t_new),
        grid_spec=grid_spec,
        out_shape=jax.ShapeDtypeStruct((bs, nrow, KV_LORA), BF16),
        compiler_params=_params(("parallel", "arbitrary"), 0),
        name="attn_decode",
    )(page_table, q_abs, q_pe, self_c, self_k, *([cache_lat] * pps), *([cache_kr] * pps))


def _even_mixer(x, dims, tabs, cache_lat, cache_kr, page_table, e, w_in, v_norm, ws, bs,
                q_norm, w_uq, kv_norm, w_uk, w_uv, w_out, mix_norm):
    n_p, b_p, l_p, b_s, t_s = dims
    cos_k, sin_k, cos_q, sin_q = tabs
    d = x.shape[1]
    d_a = v_norm.shape[0]
    q_lora = q_norm.shape[0]
    n_s = x.shape[0] - n_p
    half = ROPE // 2
    perm = jnp.concatenate([jnp.arange(half, ROPE), jnp.arange(0, half)])

    h = _rms(x, mix_norm, BF16)
    o_q = 2 * d_a
    o_kv = o_q + q_lora
    o_r = o_kv + KV_LORA
    w_r = w_in[:, o_r:o_r + ROPE]
    z64 = jnp.zeros((d, 128 - ROPE), F32)
    w_m = jnp.concatenate([w_in[:, o_q:o_r], w_r, z64, w_r[:, perm], z64], axis=1).astype(BF16)
    z_a = _mm(h, [w_in[:, :o_q].astype(BF16)], [], _epi_plain, F32, 1088, 512, "even_in_a")
    z_m = _mm(h, [w_m], [], _epi_plain, F32, 1088, 512, "even_in_m")

    a_out, v_rows = _gmlp(z_a, ws, bs, v_norm, n_p, t_s)
    qn, c, kpe, ck = _mla_prep(z_m, q_norm, kv_norm, cos_k, sin_k, q_lora)

    zpad = jnp.zeros((q_lora, N_HEADS, HEAD_PAD - NOPE - ROPE), F32)
    w1 = jnp.concatenate([w_uq, zpad], axis=2).reshape(q_lora, N_HEADS * HEAD_PAD).astype(BF16)
    w2 = jnp.concatenate([jnp.zeros((q_lora, N_HEADS, NOPE), F32), w_uq[:, :, NOPE:][:, :, perm], zpad],
                         axis=2).reshape(q_lora, N_HEADS * HEAD_PAD).astype(BF16)
    q_cat = _mm(qn, [w1, w2], [(cos_q, "i0"), (sin_q, "i0")], _epi_rope, BF16, 1088, HEAD_PAD, "mla_q")

    ckw = ck.shape[1]
    eye = jnp.eye(ROPE, dtype=F32)
    wk = jnp.zeros((ckw, N_HEADS, HEAD_PAD), F32)
    wk = wk.at[:KV_LORA, :, :NOPE].set(w_uk)
    wk = wk.at[KV_LORA:KV_LORA + ROPE, :, NOPE:NOPE + ROPE].set(jnp.broadcast_to(eye[:, None, :], (ROPE, N_HEADS, ROPE)))
    wv = jnp.zeros((ckw, N_HEADS * V_DIM), F32).at[:KV_LORA].set(w_uv.reshape(KV_LORA, N_HEADS * V_DIM))
    w_kv = jnp.concatenate([wk.reshape(ckw, N_HEADS * HEAD_PAD), wv], axis=1).astype(BF16)
    kv_p = _mm(ck[:n_p], [w_kv], [], _epi_plain, BF16, 1024, 512, "mla_kv_up")
    o_p = _attn_prompt(q_cat, kv_p, b_p, l_p)

    qs = q_cat[n_p:].reshape(n_s, N_HEADS, HEAD_PAD)
    q_nope_s = qs[:, :, :NOPE].reshape(n_s, N_HEADS * NOPE)
    q_pe_s = qs[:, :, NOPE:NOPE + ROPE].reshape(b_s, t_s * N_HEADS, ROPE)
    w_ukt = jnp.transpose(w_uk, (1, 2, 0)).astype(BF16)
    q_abs = _headmm(q_nope_s, w_ukt, BF16, "mla_q_absorb").reshape(b_s, t_s * N_HEADS, KV_LORA)
    self_c = jnp.zeros((b_s, PAGE, KV_LORA), F32).at[:, :t_s].set(c[n_p:].reshape(b_s, t_s, KV_LORA))
    self_k = jnp.zeros((b_s, PAGE, ROPE), F32).at[:, :t_s].set(kpe[n_p:].reshape(b_s, t_s, ROPE))
    o_lat = _decode(q_abs, q_pe_s, self_c, self_k, cache_lat, cache_kr, page_table, e, t_s)
    w_uvh = jnp.transpose(w_uv, (1, 0, 2)).astype(BF16)
    o_s = _headmm(o_lat.reshape(n_s, N_HEADS * KV_LORA), w_uvh, BF16, "mla_o_up")

    cat = jnp.concatenate([a_out, jnp.concatenate([o_p, o_s], axis=0)], axis=1)
    x = _mm(cat, [w_out.astype(BF16)], [(x, "ij")], functools.partial(_epi_resid, scale=1.0),
            F32, 1088, 512, "even_out")
    return x, v_rows, c, kpe


def _shift_rows(x, j):
    if j == 0:
        return x
    row = lax.broadcasted_iota(jnp.int32, x.shape, 0)
    return jnp.where(row >= j, pltpu.roll(x, j, 0), 0.0)


def _qkv_factor(y, j, tiles_per_part):
    nrm = lax.rsqrt(jnp.sum(y * y, axis=-1, keepdims=True) + EPS)
    part = j // tiles_per_part
    return jnp.where(part == 0, nrm * DK ** -0.5, jnp.where(part == 1, nrm, 1.0))


def _conv_c_prompt_body(b_ref, c_ref, x_ref, w_ref, y_ref, tail_ref):
    xg = c_ref[...] * x_ref[...]
    w = w_ref[...]
    nw = w.shape[0]
    conv = w[nw - 1:nw] * xg
    for j in range(1, nw):
        conv = conv + w[nw - 1 - j:nw - j] * _shift_rows(xg, j)
    y_ref[...] = (b_ref[...] * conv).astype(y_ref.dtype)
    n = xg.shape[0]
    tail_ref[...] = xg[n - 8:, :]


def _conv_c_prompt(z, w, n_seq, seq_len, d_c):
    tc = 256
    nb = d_c // tc
    return pl.pallas_call(
        _conv_c_prompt_body,
        grid=(n_seq, nb),
        in_specs=[pl.BlockSpec((seq_len, tc), lambda b, j: (b, j)),
                  pl.BlockSpec((seq_len, tc), lambda b, j: (b, nb + j)),
                  pl.BlockSpec((seq_len, tc), lambda b, j: (b, 2 * nb + j)),
                  pl.BlockSpec((w.shape[0], tc), lambda b, j: (0, j))],
        out_specs=[pl.BlockSpec((seq_len, tc), lambda b, j: (b, j)),
                   pl.BlockSpec((None, 8, tc), lambda b, j: (b, 0, j))],
        out_shape=[jax.ShapeDtypeStruct((n_seq * seq_len, d_c), BF16),
                   jax.ShapeDtypeStruct((n_seq, 8, d_c), F32)],
        compiler_params=_params(("parallel", "parallel"), 0),
        name="conv_c_prompt",
    )(z, z, z, w)


def _conv_d_prompt_body(x_ref, w_ref, y_ref, tail_ref, *, tiles_per_part):
    j = pl.program_id(1)
    x = x_ref[...]
    w = w_ref[...]
    nw = w.shape[0]
    conv = w[nw - 1:nw] * x
    for s in range(1, nw):
        conv = conv + w[nw - 1 - s:nw - s] * _shift_rows(x, s)
    y = _silu(conv)
    y_ref[...] = y * _qkv_factor(y, j, tiles_per_part)
    n = x.shape[0]
    tail_ref[...] = x[n - 8:, :]


def _conv_d_prompt(z, w, n_seq, seq_len, col0, d_qkv):
    tc = DK
    nb = d_qkv // tc
    off = col0 // tc
    return pl.pallas_call(
        functools.partial(_conv_d_prompt_body, tiles_per_part=nb // 3),
        grid=(n_seq, nb),
        in_specs=[pl.BlockSpec((seq_len, tc), lambda b, j: (b, off + j)),
                  pl.BlockSpec((w.shape[0], tc), lambda b, j: (0, j))],
        out_specs=[pl.BlockSpec((seq_len, tc), lambda b, j: (b, j)),
                   pl.BlockSpec((None, 8, tc), lambda b, j: (b, 0, j))],
        out_shape=[jax.ShapeDtypeStruct((n_seq * seq_len, d_qkv), F32),
                   jax.ShapeDtypeStruct((n_seq, 8, d_qkv), F32)],
        compiler_params=_params(("parallel", "parallel"), 0),
        name="conv_d_prompt",
    )(z, w)


def _conv_c_sample_body(b_ref, c_ref, x_ref, buf_ref, w_ref, y_ref, nbuf_ref):
    t_new = x_ref.shape[0]
    w = w_ref[...]
    nw = w.shape[0]
    xp = [buf_ref[s] for s in range(nw - 1)] + [c_ref[t] * x_ref[t] for t in range(t_new)]
    for t in range(t_new):
        conv = w[0:1] * xp[t]
        for s in range(1, nw):
            conv = conv + w[s:s + 1] * xp[t + s]
        y_ref[t] = (b_ref[t] * conv).astype(y_ref.dtype)
    for s in range(nw - 1):
        nbuf_ref[s] = xp[t_new + s]


def _conv_c_sample(zt, buf_t, w, d_c):
    t_new, n_seq, _ = zt.shape
    tc = 512
    nb = d_c // tc
    nw = w.shape[0]
    return pl.pallas_call(
        _conv_c_sample_body,
        grid=(nb,),
        in_specs=[pl.BlockSpec((t_new, n_seq, tc), lambda j: (0, 0, j)),
                  pl.BlockSpec((t_new, n_seq, tc), lambda j: (0, 0, nb + j)),
                  pl.BlockSpec((t_new, n_seq, tc), lambda j: (0, 0, 2 * nb + j)),
                  pl.BlockSpec((nw - 1, n_seq, tc), lambda j: (0, 0, j)),
                  pl.BlockSpec((nw, tc), lambda j: (0, j))],
        out_specs=[pl.BlockSpec((t_new, n_seq, tc), lambda j: (0, 0, j)),
                   pl.BlockSpec((nw - 1, n_seq, tc), lambda j: (0, 0, j))],
        out_shape=[jax.ShapeDtypeStruct((t_new, n_seq, d_c), BF16),
                   jax.ShapeDtypeStruct((nw - 1, n_seq, d_c), F32)],
        compiler_params=_params(("parallel",), 0),
        name="conv_c_sample",
    )(zt, zt, zt, buf_t, w)


def _conv_d_sample_body(x_ref, buf_ref, w_ref, y_ref, nbuf_ref, *, tiles_per_part):
    j = pl.program_id(0)
    t_new = x_ref.shape[0]
    w = w_ref[...]
    nw = w.shape[0]
    xp = [buf_ref[s] for s in range(nw - 1)] + [x_ref[t] for t in range(t_new)]
    for t in range(t_new):
        conv = w[0:1] * xp[t]
        for s in range(1, nw):
            conv = conv + w[s:s + 1] * xp[t + s]
        y = _silu(conv)
        y_ref[t] = y * _qkv_factor(y, j, tiles_per_part)
    for s in range(nw - 1):
        nbuf_ref[s] = xp[t_new + s]


def _conv_d_sample(zt, buf_t, w, col0, d_qkv):
    t_new, n_seq, _ = zt.shape
    tc = DK
    nb = d_qkv // tc
    off = col0 // tc
    nw = w.shape[0]
    return pl.pallas_call(
        functools.partial(_conv_d_sample_body, tiles_per_part=nb // 3),
        grid=(nb,),
        in_specs=[pl.BlockSpec((t_new, n_seq, tc), lambda j: (0, 0, off + j)),
                  pl.BlockSpec((nw - 1, n_seq, tc), lambda j: (0, 0, j)),
                  pl.BlockSpec((nw, tc), lambda j: (0, j))],
        out_specs=[pl.BlockSpec((t_new, n_seq, tc), lambda j: (0, 0, j)),
                   pl.BlockSpec((nw - 1, n_seq, tc), lambda j: (0, 0, j))],
        out_shape=[jax.ShapeDtypeStruct((t_new, n_seq, d_qkv), F32),
                   jax.ShapeDtypeStruct((nw - 1, n_seq, d_qkv), F32)],
        compiler_params=_params(("parallel",), 0),
        name="conv_d_sample",
    )(zt, buf_t, w)


def _cumsum_rows(x):
    n = x.shape[0]
    row = lax.broadcasted_iota(jnp.int32, x.shape, 0)
    s = 1
    while s < n:
        x = x + jnp.where(row >= s, pltpu.roll(x, s, 0), 0.0)
        s *= 2
    return x


def _delta_body(q_ref, k_ref, v_ref, zg_ref, gate_ref, alog_ref, dt_ref, on_ref, s0_ref,
                o_ref, sout_ref, s_ref, *, chunk, n_valid):
    c = pl.program_id(1)
    nc = pl.num_programs(1)

    @pl.when(c == 0)
    def _():
        s_ref[...] = s0_ref[...]

    gate = gate_ref[...]
    beta_all = _sigmoid(gate)
    x = gate + dt_ref[...]
    softplus = jnp.maximum(x, 0.0) + jnp.log(1.0 + jnp.exp(-jnp.abs(x)))
    g_all = -jnp.exp(alog_ref[...]) * softplus
    row128 = lax.broadcasted_iota(jnp.int32, (chunk, 128), 0)
    g_all = jnp.where(row128 < n_valid, g_all, 0.0)
    gcum = _cumsum_rows(g_all)
    eye = (lax.broadcasted_iota(jnp.int32, (128, 128), 0)
           == lax.broadcasted_iota(jnp.int32, (128, 128), 1)).astype(BF16)
    g_hi, g_mid, g_lo = _split3(gcum)
    gcum_t = _dot_nt(eye, g_hi) + (_dot_nt(eye, g_mid) + _dot_nt(eye, g_lo))

    ri = lax.broadcasted_iota(jnp.int32, (chunk, chunk), 0)
    ci = lax.broadcasted_iota(jnp.int32, (chunk, chunk), 1)
    incl = ri >= ci
    strict = ri > ci
    ident = (ri == ci).astype(F32)
    on = on_ref[...]

    for h in range(N_HEADS):
        hs = slice(h * DK, (h + 1) * DK)
        q = q_ref[:, hs]
        k = k_ref[:, hs]
        v = v_ref[:, hs]
        beta = beta_all[:, h:h + 1]
        gc = gcum[:, N_HEADS + h:N_HEADS + h + 1]
        gr = gcum_t[N_HEADS + h:N_HEADS + h + 1, :]
        g_last = gcum[chunk - 1:chunk, N_HEADS + h:N_HEADS + h + 1]
        decay = jnp.exp(jnp.where(incl, gc - gr, -jnp.inf))
        e_col = jnp.exp(gc)
        kb = k * beta
        k_b = k.astype(BF16)
        a = jnp.where(strict, _dot_nt(kb.astype(BF16), k_b) * decay, 0.0)
        pw = -a
        tm = ident + pw
        n = 2
        while n < chunk:
            pw = _dot_hp(pw, pw)
            tm = tm + _dot_hp(tm, pw)
            n *= 2
        tm_b = tm.astype(BF16)
        u = _dot(tm_b, (v * beta).astype(BF16))
        w = _dot(tm_b, (kb * e_col).astype(BF16))
        qk = jnp.where(incl, _dot_nt(q.astype(BF16), k_b) * decay, 0.0)
        q_dec = q * e_col
        k_dec = k * jnp.exp(g_last - gc)
        s_old = s_ref[h]
        s_b = s_old.astype(BF16)
        v_new = u - _dot(w.astype(BF16), s_b)
        o = _dot(q_dec.astype(BF16), s_b) + _dot(qk.astype(BF16), v_new.astype(BF16))
        s_ref[h] = s_old * jnp.exp(g_last) + _dot_tn(k_dec.astype(BF16), v_new.astype(BF16))
        o = o * lax.rsqrt(jnp.mean(o * o, axis=-1, keepdims=True) + EPS) * on
        o_ref[:, hs] = (o * _silu(zg_ref[:, hs])).astype(o_ref.dtype)

    @pl.when(c == nc - 1)
    def _():
        sout_ref[...] = s_ref[...]


def _delta(qkv, zg, zg_blk, gate, a_log, dt_bias, o_norm, s0, n_seq, chunk, n_valid):
    rows = qkv.shape[0]
    nc = rows // (n_seq * chunk)
    hw = N_HEADS * DK

    def rmap(blk):
        return lambda s, c: (s * nc + c, blk)

    alog = jnp.zeros((1, 128), F32).at[0, N_HEADS:2 * N_HEADS].set(a_log)
    dtb = jnp.zeros((1, 128), F32).at[0, N_HEADS:2 * N_HEADS].set(dt_bias)
    fix = lambda s, c: (0, 0)
    return pl.pallas_call(
        functools.partial(_delta_body, chunk=chunk, n_valid=n_valid),
        grid=(n_seq, nc),
        in_specs=[pl.BlockSpec((chunk, hw), rmap(0)),
                  pl.BlockSpec((chunk, hw), rmap(1)),
                  pl.BlockSpec((chunk, hw), rmap(2)),
                  pl.BlockSpec((chunk, hw), rmap(zg_blk)),
                  pl.BlockSpec((chunk, 128), rmap(0)),
                  pl.BlockSpec((1, 128), fix),
                  pl.BlockSpec((1, 128), fix),
                  pl.BlockSpec((1, DV), fix),
                  pl.BlockSpec((None, N_HEADS, DK, DV), lambda s, c: (s, 0, 0, 0))],
        out_specs=[pl.BlockSpec((chunk, hw), rmap(0)),
                   pl.BlockSpec((None, N_HEADS, DK, DV), lambda s, c: (s, 0, 0, 0))],
        out_shape=[jax.ShapeDtypeStruct((rows, hw), BF16),
                   jax.ShapeDtypeStruct((n_seq, N_HEADS, DK, DV), F32)],
        scratch_shapes=[pltpu.VMEM((N_HEADS, DK, DV), F32)],
        compiler_params=_params(("parallel", "arbitrary"), 0),
        name="gated_delta",
    )(qkv, qkv, qkv, zg, gate, alog, dtb, o_norm.reshape(1, DV), s0)


def _odd_mixer(x, dims, buf_c, buf_d, s0_s, w_in, conv_c_w, conv_d_w, a_log, dt_bias, o_norm,
               w_out, mix_norm):
    n_p, b_p, l_p, b_s, t_s = dims
    d = x.shape[1]
    d_c = conv_c_w.shape[1]
    d_qkv = conv_d_w.shape[1]
    n_in = w_in.shape[1]
    n_pad = -(-n_in // 512) * 512
    o_qkv = 3 * d_c
    o_gate = o_qkv + d_qkv
    o_tail = o_gate + N_HEADS * DV
    assert o_tail % 128 == 0 and n_pad - o_tail >= 128 and o_gate % (N_HEADS * DV) == 0

    h = _rms(x, mix_norm, BF16)
    w_b = jnp.pad(w_in.astype(BF16), ((0, 0), (0, n_pad - n_in)))
    z = _mm(h, [w_b], [], _epi_plain, F32, 1088, 512, "odd_in")

    y_c_p, tail_c_p = _conv_c_prompt(z, conv_c_w, b_p, l_p, d_c)
    qkv_p, tail_d_p = _conv_d_prompt(z, conv_d_w, b_p, l_p, o_qkv, d_qkv)
    chunk_p = math.gcd(l_p, DN_CHUNK)
    gate_p = z[:n_p, o_tail:o_tail + 128]
    s0_p = jnp.zeros((b_p, N_HEADS, DK, DV), F32)
    o_p, st_p = _delta(qkv_p, z, o_gate // (N_HEADS * DV), gate_p, a_log, dt_bias, o_norm, s0_p,
                       b_p, chunk_p, chunk_p)

    zt = jnp.transpose(z[n_p:].reshape(b_s, t_s, n_pad), (1, 0, 2))
    y_c_t, nbuf_c_t = _conv_c_sample(zt, jnp.transpose(buf_c, (1, 0, 2)), conv_c_w, d_c)
    qkv_t, nbuf_d_t = _conv_d_sample(zt, jnp.transpose(buf_d, (1, 0, 2)), conv_d_w, o_qkv, d_qkv)
    pad_t = SAMPLE_CHUNK - t_s
    assert pad_t >= 0

    def to_seq(a_t):
        a = jnp.transpose(a_t, (1, 0, 2))
        a = jnp.pad(a, ((0, 0), (0, pad_t), (0, 0)))
        return a.reshape(b_s * SAMPLE_CHUNK, a.shape[2])

    qkv_s = to_seq(qkv_t)
    zs = z[n_p:].reshape(b_s, t_s, n_pad)
    zg_s = jnp.pad(zs[:, :, o_gate:o_tail], ((0, 0), (0, pad_t), (0, 0))).reshape(b_s * SAMPLE_CHUNK, -1)
    gate_s = jnp.pad(zs[:, :, o_tail:o_tail + 128], ((0, 0), (0, pad_t), (0, 0))).reshape(b_s * SAMPLE_CHUNK, 128)
    o_s8, st_s = _delta(qkv_s, zg_s, 0, gate_s, a_log, dt_bias, o_norm, s0_s, b_s, SAMPLE_CHUNK, t_s)
    o_s = o_s8.reshape(b_s, SAMPLE_CHUNK, -1)[:, :t_s].reshape(b_s * t_s, -1)
    y_c_s = jnp.transpose(y_c_t, (1, 0, 2)).reshape(b_s * t_s, d_c)

    cat = jnp.concatenate([jnp.concatenate([y_c_p, y_c_s], axis=0),
                           jnp.concatenate([o_p, o_s], axis=0)], axis=1)
    x = _mm(cat, [w_out.astype(BF16)], [(x, "ij")], functools.partial(_epi_resid, scale=1.0),
            F32, 1088, 512, "odd_out")
    nw_c = conv_c_w.shape[0] - 1
    nw_d = conv_d_w.shape[0] - 1
    outs = (tail_c_p[:, 8 - nw_c:], jnp.transpose(nbuf_c_t, (1, 0, 2)),
            tail_d_p[:, 8 - nw_d:], jnp.transpose(nbuf_d_t, (1, 0, 2)), st_p, st_s)
    return x, outs


def _rope_tables(pos):
    half = ROPE // 2
    inv = ROPE_THETA ** (-jnp.arange(half, dtype=F32) / half)
    ang = pos[:, None] * inv[None, :]
    cos = jnp.cos(ang)
    sin = jnp.sin(ang)
    n = pos.shape[0]
    cos_f = jnp.concatenate([cos, cos], axis=1)
    sin_f = jnp.concatenate([-sin, sin], axis=1)
    z = jnp.zeros((n, 128 - ROPE), F32)
    cos_k = jnp.concatenate([cos_f, z], axis=1)
    sin_k = jnp.concatenate([sin_f, z], axis=1)
    cos_q = jnp.concatenate([jnp.ones((n, NOPE), F32), cos_f, z], axis=1)
    sin_q = jnp.concatenate([jnp.zeros((n, NOPE), F32), sin_f, z], axis=1)
    return cos_k, sin_k, cos_q, sin_q


def kernel(x_prompt, x_sample, cache_mla_latent, cache_mla_krope, state_conv_c, state_conv_d, state_delta, page_table, ffn1_norm, ffn1_w_gate, ffn1_w_up, ffn1_w_down, mix_norm, ffn2_norm, ffn2_w_gate, ffn2_w_up, ffn2_w_down, even_w_in, gmlp_v_norm, gmlp_ws, gmlp_bs, mla_q_norm, mla_w_uq, mla_kv_norm, mla_w_uk, mla_w_uv, even_w_out, odd_w_in, conv_c_w, conv_d_w, delta_a_log, delta_dt_bias, delta_o_norm, odd_w_out, final_norm):
    b_p, l_p, d = x_prompt.shape
    b_s, t_s, _ = x_sample.shape
    n_p = b_p * l_p
    n_s = b_s * t_s
    depth = ffn1_norm.shape[0]
    dims = (n_p, b_p, l_p, b_s, t_s)
    past_len = page_table.shape[1] * PAGE
    pos = jnp.concatenate([jnp.tile(jnp.arange(l_p, dtype=F32), b_p),
                           jnp.tile(jnp.arange(t_s, dtype=F32) + past_len, b_s)])
    tabs = _rope_tables(pos)

    x = jnp.concatenate([x_prompt.reshape(n_p, d), x_sample.reshape(n_s, d)], axis=0)
    lat, kr, vrow = [], [], []
    cc_p, cc_s, cd_p, cd_s, sd_p, sd_s = [], [], [], [], [], []
    for layer in range(depth):
        x = _ffn_half(x, ffn1_norm[layer], ffn1_w_gate[layer], ffn1_w_up[layer], ffn1_w_down[layer])
        if layer % 2 == 0:
            e = layer // 2
            x, v_rows, c, kpe = _even_mixer(
                x, dims, tabs, cache_mla_latent, cache_mla_krope, page_table, e, even_w_in[e],
                gmlp_v_norm[e], gmlp_ws[e], gmlp_bs[e], mla_q_norm[e], mla_w_uq[e], mla_kv_norm[e],
                mla_w_uk[e], mla_w_uv[e], even_w_out[e], mix_norm[layer])
            lat.append(c)
            kr.append(kpe)
            vrow.append(v_rows[n_p:])
        else:
            o = layer // 2
            x, outs = _odd_mixer(
                x, dims, state_conv_c[o], state_conv_d[o], state_delta[o], odd_w_in[o], conv_c_w[o],
                conv_d_w[o], delta_a_log[o], delta_dt_bias[o], delta_o_norm[o], odd_w_out[o],
                mix_norm[layer])
            cc_p.append(outs[0]); cc_s.append(outs[1]); cd_p.append(outs[2]); cd_s.append(outs[3])
            sd_p.append(outs[4]); sd_s.append(outs[5])
        x = _ffn_half(x, ffn2_norm[layer], ffn2_w_gate[layer], ffn2_w_up[layer], ffn2_w_down[layer])

    y = _rms(x, final_norm, F32)
    lat = jnp.stack(lat)
    kr = jnp.stack(kr)
    n_e = lat.shape[0]
    return (y[:n_p].reshape(b_p, l_p, d), y[n_p:].reshape(b_s, t_s, d),
            lat[:, :n_p].reshape(n_e, b_p, l_p, -1), kr[:, :n_p].reshape(n_e, b_p, l_p, -1),
            lat[:, n_p:].reshape(n_e, b_s, t_s, -1), kr[:, n_p:].reshape(n_e, b_s, t_s, -1),
            jnp.stack(vrow).reshape(n_e, b_s, t_s, -1),
            jnp.stack(cc_p), jnp.stack(cc_s), jnp.stack(cd_p), jnp.stack(cd_s),
            jnp.stack(sd_p), jnp.stack(sd_s))
```

```python
import functools
import math

import jax
import jax.numpy as jnp
from jax import lax
from jax.experimental import pallas as pl
from jax.experimental.pallas import tpu as pltpu

F32 = jnp.float32
BF16 = jnp.bfloat16
EPS = 1e-6

PAGE = 128
GROUP = 128
N_HEADS = 16
NOPE = 128
ROPE = 64
HEAD_PAD = 256
KV_LORA = 512
V_DIM = 128
DK = 128
DV = 128
ROPE_THETA = 10000.0
ATT_SCALE = (NOPE + ROPE) ** -0.5
DN_CHUNK = 64
SAMPLE_CHUNK = 8
PAGES_PER_STEP = 32
PAGES_PER_GROUP = 4
DELTA_GROUP = 8

VMEM_CAP_BYTES = 60 * 1024 * 1024


def _pick(n, target, mult=8):
    for d in range(min(n, target), 0, -1):
        if n % d == 0 and d % mult == 0:
            return d
    return n


def _params(sem, vmem_bytes):
    return pltpu.CompilerParams(
        dimension_semantics=sem,
        vmem_limit_bytes=int(min(VMEM_CAP_BYTES, max(vmem_bytes, 32 * 1024 * 1024))))


def _sigmoid(x):
    return 1.0 / (1.0 + jnp.exp(-x))


def _silu(x):
    return x * _sigmoid(x)


def _dot(a, b):
    return jnp.dot(a, b, preferred_element_type=F32)


def _dot_nt(a, b):
    return lax.dot_general(a, b, (((1,), (1,)), ((), ())), preferred_element_type=F32)


def _dot_tn(a, b):
    return lax.dot_general(a, b, (((0,), (0,)), ((), ())), preferred_element_type=F32)


def _split3(x):
    hi = x.astype(BF16)
    r = x - hi.astype(F32)
    mid = r.astype(BF16)
    lo = (r - mid.astype(F32)).astype(BF16)
    return hi, mid, lo


def _dot_hp(a, b):
    a_hi, a_mid, _ = _split3(a)
    b_hi, b_mid, _ = _split3(b)
    return _dot(a_hi, b_hi) + (_dot(a_hi, b_mid) + _dot(a_mid, b_hi))


def _rms_body(x_ref, g_ref, o_ref):
    x = x_ref[...]
    y = x * lax.rsqrt(jnp.mean(x * x, axis=-1, keepdims=True) + EPS)
    o_ref[...] = (y * g_ref[...]).astype(o_ref.dtype)


def _rms(x, g, out_dtype):
    m, d = x.shape
    tm = _pick(m, 256, 16)
    return pl.pallas_call(
        _rms_body,
        grid=(m // tm,),
        in_specs=[pl.BlockSpec((tm, d), lambda i: (i, 0)),
                  pl.BlockSpec((1, d), lambda i: (0, 0))],
        out_specs=pl.BlockSpec((tm, d), lambda i: (i, 0)),
        out_shape=jax.ShapeDtypeStruct((m, d), out_dtype),
        compiler_params=_params(("parallel",), 6 * tm * d * 4),
        name="rmsnorm",
    )(x, g.reshape(1, d))


def _mm_body(*refs, n_w, n_x, epi):
    lhs = refs[0][...]
    accs = [_dot(lhs, refs[1 + k][...]) for k in range(n_w)]
    xs = [refs[1 + n_w + k][...] for k in range(n_x)]
    o_ref = refs[1 + n_w + n_x]
    o_ref[...] = epi(accs, xs).astype(o_ref.dtype)


def _epi_plain(accs, xs):
    return accs[0]


def _epi_swiglu(accs, xs):
    return _silu(accs[0]) * accs[1]


def _epi_resid(accs, xs, scale):
    return xs[0] + scale * accs[0]


def _epi_rope(accs, xs):
    return accs[0] * xs[0] + accs[1] * xs[1]


def _mm(lhs, ws, extras, epi, out_dtype, tm_target, tn, name):
    m, k = lhs.shape
    n = ws[0].shape[1]
    tm = _pick(m, tm_target, 16)
    assert n % tn == 0, (n, tn)
    in_specs = [pl.BlockSpec((tm, k), lambda i, j: (i, 0))]
    in_specs += [pl.BlockSpec((k, tn), lambda i, j: (0, j)) for _ in ws]
    for _, kind in extras:
        if kind == "ij":
            in_specs.append(pl.BlockSpec((tm, tn), lambda i, j: (i, j)))
        else:
            in_specs.append(pl.BlockSpec((tm, tn), lambda i, j: (i, 0)))
    vmem = 2 * (tm * k * 2 + len(ws) * k * tn * 2 + (len(extras) + 1) * tm * tn * 4)
    vmem += (len(ws) + 1) * tm * tn * 4 + (4 << 20)
    return pl.pallas_call(
        functools.partial(_mm_body, n_w=len(ws), n_x=len(extras), epi=epi),
        grid=(m // tm, n // tn),
        in_specs=in_specs,
        out_specs=pl.BlockSpec((tm, tn), lambda i, j: (i, j)),
        out_shape=jax.ShapeDtypeStruct((m, n), out_dtype),
        compiler_params=_params(("parallel", "arbitrary"), vmem),
        name=name,
    )(lhs, *ws, *[a for a, _ in extras])


def _headmm_body(l_ref, w_ref, o_ref):
    o_ref[...] = _dot(l_ref[...], w_ref[...]).astype(o_ref.dtype)


def _headmm(lhs, w, out_dtype, name):
    m = lhs.shape[0]
    nh, kd, nd = w.shape
    return pl.pallas_call(
        _headmm_body,
        grid=(nh,),
        in_specs=[pl.BlockSpec((m, kd), lambda h: (0, h)),
                  pl.BlockSpec((None, kd, nd), lambda h: (h, 0, 0))],
        out_specs=pl.BlockSpec((m, nd), lambda h: (0, h)),
        out_shape=jax.ShapeDtypeStruct((m, nh * nd), out_dtype),
        compiler_params=_params(("parallel",), 0),
        name=name,
    )(lhs, w)


def _mmw_body(*refs, n_acc, n_lhs, n_x, epi):
    n_w = n_acc * n_lhs
    lhs_refs = refs[:n_lhs]
    w_refs = refs[n_lhs:n_lhs + n_w]
    x_refs = refs[n_lhs + n_w:n_lhs + n_w + n_x]
    o_ref = refs[n_lhs + n_w + n_x]
    wb_refs = refs[n_lhs + n_w + n_x + 1:]

    @pl.when(pl.program_id(1) == 0)
    def _():
        for w_ref, wb_ref in zip(w_refs, wb_refs):
            wb_ref[...] = w_ref[...].astype(BF16)

    lhs = [r[...] for r in lhs_refs]
    accs = []
    for a in range(n_acc):
        acc = _dot(lhs[0], wb_refs[a * n_lhs][...])
        for l in range(1, n_lhs):
            acc = acc + _dot(lhs[l], wb_refs[a * n_lhs + l][...])
        accs.append(acc)
    o_ref[...] = epi(accs, [r[...] for r in x_refs]).astype(o_ref.dtype)


def _mmw(lhs_list, w_list, extras, epi, out_dtype, tm_target, tn, name):
    m = lhs_list[0][0].shape[0]
    n = w_list[0][0][0].shape[2]
    tm = _pick(m, tm_target, 16)
    assert n % tn == 0, (n, tn)
    n_lhs = len(lhs_list)
    in_specs, args = [], []
    vmem = 0
    for arr, cb, k in lhs_list:
        in_specs.append(pl.BlockSpec((tm, k), lambda j, i, cb=cb: (i, cb)))
        args.append(arr)
        vmem += 2 * tm * k * arr.dtype.itemsize
    scratch = []
    for ws in w_list:
        assert len(ws) == n_lhs
        for (arr, layer, rb), (_, _, k) in zip(ws, lhs_list):
            in_specs.append(pl.BlockSpec((None, k, tn), lambda j, i, layer=layer, rb=rb: (layer, rb, j)))
            args.append(arr)
            scratch.append(pltpu.VMEM((k, tn), BF16))
            vmem += 2 * k * tn * 4 + k * tn * 2
    for arr, kind in extras:
        if kind == "ij":
            in_specs.append(pl.BlockSpec((tm, tn), lambda j, i: (i, j)))
        else:
            in_specs.append(pl.BlockSpec((tm, tn), lambda j, i: (i, 0)))
        args.append(arr)
    vmem += (2 * (len(extras) + 1) + len(w_list) + 1) * tm * tn * 4 + (4 << 20)
    return pl.pallas_call(
        functools.partial(_mmw_body, n_acc=len(w_list), n_lhs=n_lhs, n_x=len(extras), epi=epi),
        grid=(n // tn, m // tm),
        in_specs=in_specs,
        out_specs=pl.BlockSpec((tm, tn), lambda j, i: (i, j)),
        out_shape=jax.ShapeDtypeStruct((m, n), out_dtype),
        scratch_shapes=scratch,
        compiler_params=_params(("parallel", "arbitrary"), vmem),
        name=name,
    )(*args)


def _ffn_half(x, g, wg, wu, wd, layer):
    d, f = wg.shape[1:]
    assert f % 256 == 0
    kh = f // 2
    assert kh % 128 == 0
    h = _rms(x, g, BF16)
    a = _mmw([(h, 0, d)], [[(wg, layer, 0)], [(wu, layer, 0)]], [], _epi_swiglu, BF16, 1088, 256,
             "ffn_gate_up")
    half = functools.partial(_epi_resid, scale=0.5)
    x = _mmw([(a, 0, kh)], [[(wd, layer, 0)]], [(x, "ij")], half, F32, 544, 512, "ffn_down_lo")
    return _mmw([(a, 1, kh)], [[(wd, layer, 1)]], [(x, "ij")], half, F32, 544, 512, "ffn_down_hi")


def _gmlp_body(z_ref, w_ref, b_ref, vn_ref, a_ref, v_ref, *, n_prompt_tiles, t_sample, d_a):
    i = pl.program_id(0)
    z = z_ref[...]
    c0 = math.sqrt(2.0 / math.pi)
    a = 0.5 * z * (1.0 + jnp.tanh(c0 * (z + 0.044715 * (z * z * z))))
    u = a[:, :d_a]
    v = a[:, d_a:]
    v = v * lax.rsqrt(jnp.mean(v * v, axis=-1, keepdims=True) + EPS) * vn_ref[...]
    v_ref[...] = v
    row = lax.broadcasted_iota(jnp.int32, (GROUP, GROUP), 0)
    col = lax.broadcasted_iota(jnp.int32, (GROUP, GROUP), 1)
    same_seq = (row // t_sample) == (col // t_sample)
    mask = (col <= row) & (same_seq | (i < n_prompt_tiles))
    for g in range(d_a // GROUP):
        sl = slice(g * GROUP, (g + 1) * GROUP)
        w = jnp.where(mask, w_ref[g], 0.0).astype(BF16)
        s = _dot(w, v[:, sl].astype(BF16)) + b_ref[:, sl]
        a_ref[:, sl] = (u[:, sl] * s).astype(a_ref.dtype)


def _gmlp(z_a, ws, bs, v_norm, n_prompt, t_sample):
    m = z_a.shape[0]
    d_a = z_a.shape[1] // 2
    ng = d_a // GROUP
    assert n_prompt % GROUP == 0 and (m - n_prompt) % GROUP == 0 and GROUP % t_sample == 0
    npt = n_prompt // GROUP
    rep = GROUP // t_sample
    w_all = jnp.stack([ws, jnp.tile(ws[:, :t_sample, :t_sample], (1, rep, rep))])
    b_p = jnp.repeat(bs.T, GROUP, axis=1)
    b_s = jnp.repeat(jnp.tile(bs[:, :t_sample].T, (rep, 1)), GROUP, axis=1)
    b_all = jnp.stack([b_p, b_s])

    def sel(i):
        return jnp.where(i < npt, 0, 1)

    return pl.pallas_call(
        functools.partial(_gmlp_body, n_prompt_tiles=npt, t_sample=t_sample, d_a=d_a),
        grid=(m // GROUP,),
        in_specs=[pl.BlockSpec((GROUP, 2 * d_a), lambda i: (i, 0)),
                  pl.BlockSpec((None, ng, GROUP, GROUP), lambda i: (sel(i), 0, 0, 0)),
                  pl.BlockSpec((None, GROUP, d_a), lambda i: (sel(i), 0, 0)),
                  pl.BlockSpec((1, d_a), lambda i: (0, 0))],
        out_specs=[pl.BlockSpec((GROUP, d_a), lambda i: (i, 0)),
                   pl.BlockSpec((GROUP, d_a), lambda i: (i, 0))],
        out_shape=[jax.ShapeDtypeStruct((m, d_a), BF16),
                   jax.ShapeDtypeStruct((m, d_a), F32)],
        compiler_params=_params(("parallel",), 0),
        name="gmlp",
    )(z_a, w_all, b_all, v_norm.reshape(1, d_a))


def _mla_prep_body(z_ref, qg_ref, kg_ref, cos_ref, sin_ref, qn_ref, c_ref, kpe_ref, ck_ref, *, q_lora):
    z = z_ref[...]
    zq = z[:, :q_lora]
    qn_ref[...] = (zq * lax.rsqrt(jnp.mean(zq * zq, axis=-1, keepdims=True) + EPS)
                   * qg_ref[...]).astype(qn_ref.dtype)
    zkv = z[:, q_lora:q_lora + KV_LORA]
    c = zkv * lax.rsqrt(jnp.mean(zkv * zkv, axis=-1, keepdims=True) + EPS) * kg_ref[...]
    c_ref[...] = c
    r0 = q_lora + KV_LORA
    kpe = z[:, r0:r0 + 128] * cos_ref[...] + z[:, r0 + 128:r0 + 256] * sin_ref[...]
    kpe_ref[...] = kpe[:, :ROPE]
    ck_ref[:, :KV_LORA] = c.astype(ck_ref.dtype)
    ck_ref[:, KV_LORA:] = kpe.astype(ck_ref.dtype)


def _mla_prep(z_m, q_norm, kv_norm, cos128, sin128, q_lora):
    m, nz = z_m.shape
    tm = _pick(m, 544, 16)
    row = lambda i: (i, 0)
    fix = lambda i: (0, 0)
    return pl.pallas_call(
        functools.partial(_mla_prep_body, q_lora=q_lora),
        grid=(m // tm,),
        in_specs=[pl.BlockSpec((tm, nz), row),
                  pl.BlockSpec((1, q_lora), fix),
                  pl.BlockSpec((1, KV_LORA), fix),
                  pl.BlockSpec((tm, 128), row),
                  pl.BlockSpec((tm, 128), row)],
        out_specs=[pl.BlockSpec((tm, q_lora), row),
                   pl.BlockSpec((tm, KV_LORA), row),
                   pl.BlockSpec((tm, ROPE), row),
                   pl.BlockSpec((tm, KV_LORA + 128), row)],
        out_shape=[jax.ShapeDtypeStruct((m, q_lora), BF16),
                   jax.ShapeDtypeStruct((m, KV_LORA), F32),
                   jax.ShapeDtypeStruct((m, ROPE), F32),
                   jax.ShapeDtypeStruct((m, KV_LORA + 128), BF16)],
        compiler_params=_params(("parallel",), 0),
        name="mla_prep",
    )(z_m, q_norm.reshape(1, q_lora), kv_norm.reshape(1, KV_LORA), cos128, sin128)


def _attn_prompt_body(q_ref, k_ref, v_ref, o_ref, m_ref, l_ref, acc_ref, *, tq, tk):
    qi = pl.program_id(1)
    ki = pl.program_id(2)
    nk = pl.num_programs(2)

    @pl.when(ki == 0)
    def _():
        m_ref[...] = jnp.full(m_ref.shape, -jnp.inf, F32)
        l_ref[...] = jnp.zeros(l_ref.shape, F32)
        acc_ref[...] = jnp.zeros(acc_ref.shape, F32)

    def scores(h):
        qs = slice(h * HEAD_PAD, (h + 1) * HEAD_PAD)
        return _dot_nt(q_ref[:, qs], k_ref[:, qs]) * ATT_SCALE

    def process(diagonal):
        if diagonal:
            mask = (lax.broadcasted_iota(jnp.int32, (tq, tk), 1)
                    <= lax.broadcasted_iota(jnp.int32, (tq, tk), 0))
        s_next = scores(0)
        for h in range(N_HEADS):
            vs = slice(h * V_DIM, (h + 1) * V_DIM)
            s = s_next
            if h + 1 < N_HEADS:
                s_next = scores(h + 1)
            if diagonal:
                s = jnp.where(mask, s, -jnp.inf)
            m_old = m_ref[h][:, :1]
            m_new = jnp.maximum(m_old, jnp.max(s, axis=-1, keepdims=True))
            corr = jnp.exp(m_old - m_new)
            p = jnp.exp(s - m_new)
            l_ref[h] = jnp.broadcast_to(l_ref[h][:, :1] * corr + jnp.sum(p, axis=-1, keepdims=True),
                                        (tq, 128))
            acc_ref[:, vs] = acc_ref[:, vs] * corr + _dot(p.astype(BF16), v_ref[:, vs])
            m_ref[h] = jnp.broadcast_to(m_new, (tq, 128))

    @pl.when(ki < qi)
    def _():
        process(False)

    @pl.when(ki == qi)
    def _():
        process(True)

    @pl.when(ki == nk - 1)
    def _():
        for h in range(N_HEADS):
            vs = slice(h * V_DIM, (h + 1) * V_DIM)
            o_ref[:, vs] = (acc_ref[:, vs] / l_ref[h][:, :1]).astype(o_ref.dtype)


def _attn_prompt(q_cat, kv, n_seq, seq_len):
    tq = _pick(seq_len, 256, 16)
    tk = tq
    nq = seq_len // tq
    qw = N_HEADS * HEAD_PAD
    vw = N_HEADS * V_DIM
    assert qw % vw == 0
    v_blk = qw // vw

    def q_map(b, qi, ki):
        return (b * nq + qi, 0)

    def k_map(b, qi, ki):
        return (b * nq + jnp.minimum(ki, qi), 0)

    def v_map(b, qi, ki):
        return (b * nq + jnp.minimum(ki, qi), v_blk)

    return pl.pallas_call(
        functools.partial(_attn_prompt_body, tq=tq, tk=tk),
        grid=(n_seq, nq, nq),
        in_specs=[pl.BlockSpec((tq, qw), q_map),
                  pl.BlockSpec((tk, qw), k_map),
                  pl.BlockSpec((tk, vw), v_map)],
        out_specs=pl.BlockSpec((tq, vw), q_map),
        out_shape=jax.ShapeDtypeStruct((n_seq * seq_len, vw), BF16),
        scratch_shapes=[pltpu.VMEM((N_HEADS, tq, 128), F32),
                        pltpu.VMEM((N_HEADS, tq, 128), F32),
                        pltpu.VMEM((tq, vw), F32)],
        compiler_params=_params(("parallel", "parallel", "arbitrary"), 0),
        name="attn_prompt",
    )(q_cat, kv, kv)


def _decode_body(pt_ref, qa_ref, qp_ref, sc_ref, sk_ref, *refs, pps, gsz):
    lat_refs = refs[:pps]
    kr_refs = refs[pps:2 * pps]
    o_ref, m_ref, l_ref, acc_ref = refs[2 * pps:]
    c = pl.program_id(1)
    nc = pl.num_programs(1)
    qa = qa_ref[...]
    qp = qp_ref[...]
    nrow = qa.shape[0]

    def scores(kc, kpt):
        return (_dot_nt(qa, kc) + _dot(qp, kpt)) * ATT_SCALE

    def update(state, s, kc):
        m_old, l_old, acc = state
        m_new = jnp.maximum(m_old, jnp.max(s, axis=-1, keepdims=True))
        corr = jnp.exp(m_old - m_new)
        p = jnp.exp(s - m_new)
        l_new = l_old * corr + jnp.sum(p, axis=-1, keepdims=True)
        return m_new, l_new, acc * corr + _dot(p.astype(BF16), kc)

    def store(state):
        m_ref[...] = jnp.broadcast_to(state[0], m_ref.shape)
        l_ref[...] = jnp.broadcast_to(state[1], l_ref.shape)
        acc_ref[...] = state[2]

    @pl.when(c == 0)
    def _():
        key = lax.broadcasted_iota(jnp.int32, (nrow, PAGE), 1)
        tok = lax.broadcasted_iota(jnp.int32, (nrow, PAGE), 0) // N_HEADS
        kc = sc_ref[...].astype(BF16)
        s = jnp.where(key <= tok, scores(kc, sk_ref[...].astype(BF16)), -jnp.inf)
        init = (jnp.full((nrow, 1), -jnp.inf, F32), jnp.zeros((nrow, 1), F32),
                jnp.zeros((nrow, KV_LORA), F32))
        store(update(init, s, kc))

    def load_group(g):
        ks = range(g * gsz, (g + 1) * gsz)
        kc = jnp.concatenate([lat_refs[k][...].astype(BF16) for k in ks], axis=0)
        kpt = jnp.concatenate([kr_refs[k][...].astype(BF16) for k in ks], axis=1)
        return kc, kpt

    state = (m_ref[:, :1], l_ref[:, :1], acc_ref[...])
    kc, kpt = load_group(0)
    s = scores(kc, kpt)
    for g in range(pps // gsz):
        if g + 1 < pps // gsz:
            kc_next, kpt_next = load_group(g + 1)
            s_next = scores(kc_next, kpt_next)
        state = update(state, s, kc)
        if g + 1 < pps // gsz:
            kc, s = kc_next, s_next
    store(state)

    @pl.when(c == nc - 1)
    def _():
        o_ref[...] = (acc_ref[...] / l_ref[:, :1]).astype(o_ref.dtype)


def _decode(q_abs, q_pe, self_c, self_kt, cache_lat, cache_krt, page_table, layer):
    bs, nrow, _ = q_abs.shape
    n_pages = page_table.shape[1]
    pps = _pick(n_pages, PAGES_PER_STEP, 1)
    gsz = _pick(pps, PAGES_PER_GROUP, 1)
    nc = n_pages // pps

    def fix(b, c, pt):
        return (b, 0, 0)

    def page_map(k):
        return lambda b, c, pt: (layer, pt[b, c * pps + k], 0, 0)

    in_specs = [pl.BlockSpec((None, nrow, KV_LORA), fix),
                pl.BlockSpec((None, nrow, ROPE), fix),
                pl.BlockSpec((None, PAGE, KV_LORA), fix),
                pl.BlockSpec((None, ROPE, PAGE), fix)]
    in_specs += [pl.BlockSpec((None, None, PAGE, KV_LORA), page_map(k)) for k in range(pps)]
    in_specs += [pl.BlockSpec((None, None, ROPE, PAGE), page_map(k)) for k in range(pps)]
    grid_spec = pltpu.PrefetchScalarGridSpec(
        num_scalar_prefetch=1,
        grid=(bs, nc),
        in_specs=in_specs,
        out_specs=pl.BlockSpec((None, nrow, KV_LORA), fix),
        scratch_shapes=[pltpu.VMEM((nrow, 128), F32),
                        pltpu.VMEM((nrow, 128), F32),
                        pltpu.VMEM((nrow, KV_LORA), F32)])
    vmem = 2 * pps * (PAGE * KV_LORA + ROPE * PAGE) * 4 + (16 << 20)
    return pl.pallas_call(
        functools.partial(_decode_body, pps=pps, gsz=gsz),
        grid_spec=grid_spec,
        out_shape=jax.ShapeDtypeStruct((bs, nrow, KV_LORA), BF16),
        compiler_params=_params(("parallel", "arbitrary"), vmem),
        name="attn_decode",
    )(page_table, q_abs, q_pe, self_c, self_kt, *([cache_lat] * pps), *([cache_krt] * pps))


def _even_mixer(x, dims, tabs, cache_lat, cache_kr, page_table, e, w_in, v_norm, ws, bs,
                q_norm, w_uq, kv_norm, w_uk, w_uv, w_out, mix_norm):
    n_p, b_p, l_p, b_s, t_s = dims
    cos_k, sin_k, cos_q, sin_q = tabs
    d = x.shape[1]
    d_a = v_norm.shape[0]
    q_lora = q_norm.shape[0]
    n_s = x.shape[0] - n_p
    half = ROPE // 2
    perm = jnp.concatenate([jnp.arange(half, ROPE), jnp.arange(0, half)])

    h = _rms(x, mix_norm, BF16)
    o_q = 2 * d_a
    o_kv = o_q + q_lora
    o_r = o_kv + KV_LORA
    w_r = w_in[:, o_r:o_r + ROPE]
    z64 = jnp.zeros((d, 128 - ROPE), F32)
    w_m = jnp.concatenate([w_in[:, o_q:o_r], w_r, z64, w_r[:, perm], z64], axis=1).astype(BF16)
    z_a = _mm(h, [w_in[:, :o_q].astype(BF16)], [], _epi_plain, F32, 1088, 512, "even_in_a")
    z_m = _mm(h, [w_m], [], _epi_plain, F32, 1088, 512, "even_in_m")

    a_out, v_rows = _gmlp(z_a, ws, bs, v_norm, n_p, t_s)
    qn, c, kpe, ck = _mla_prep(z_m, q_norm, kv_norm, cos_k, sin_k, q_lora)

    zpad = jnp.zeros((q_lora, N_HEADS, HEAD_PAD - NOPE - ROPE), F32)
    w1 = jnp.concatenate([w_uq, zpad], axis=2).reshape(q_lora, N_HEADS * HEAD_PAD).astype(BF16)
    w2 = jnp.concatenate([jnp.zeros((q_lora, N_HEADS, NOPE), F32), w_uq[:, :, NOPE:][:, :, perm], zpad],
                         axis=2).reshape(q_lora, N_HEADS * HEAD_PAD).astype(BF16)
    q_cat = _mm(qn, [w1, w2], [(cos_q, "i0"), (sin_q, "i0")], _epi_rope, BF16, 1088, HEAD_PAD, "mla_q")

    ckw = ck.shape[1]
    eye = jnp.eye(ROPE, dtype=F32)
    wk = jnp.zeros((ckw, N_HEADS, HEAD_PAD), F32)
    wk = wk.at[:KV_LORA, :, :NOPE].set(w_uk)
    wk = wk.at[KV_LORA:KV_LORA + ROPE, :, NOPE:NOPE + ROPE].set(jnp.broadcast_to(eye[:, None, :], (ROPE, N_HEADS, ROPE)))
    wv = jnp.zeros((ckw, N_HEADS * V_DIM), F32).at[:KV_LORA].set(w_uv.reshape(KV_LORA, N_HEADS * V_DIM))
    w_kv = jnp.concatenate([wk.reshape(ckw, N_HEADS * HEAD_PAD), wv], axis=1).astype(BF16)
    kv_p = _mm(ck[:n_p], [w_kv], [], _epi_plain, BF16, 1024, 512, "mla_kv_up")
    o_p = _attn_prompt(q_cat, kv_p, b_p, l_p)

    qs = q_cat[n_p:].reshape(n_s, N_HEADS, HEAD_PAD)
    q_nope_s = qs[:, :, :NOPE].reshape(n_s, N_HEADS * NOPE)
    q_pe_s = qs[:, :, NOPE:NOPE + ROPE].reshape(b_s, t_s * N_HEADS, ROPE)
    w_ukt = jnp.transpose(w_uk, (1, 2, 0)).astype(BF16)
    q_abs = _headmm(q_nope_s, w_ukt, BF16, "mla_q_absorb").reshape(b_s, t_s * N_HEADS, KV_LORA)
    self_c = jnp.zeros((b_s, PAGE, KV_LORA), F32).at[:, :t_s].set(c[n_p:].reshape(b_s, t_s, KV_LORA))
    self_kt = jnp.zeros((b_s, ROPE, PAGE), F32).at[:, :, :t_s].set(
        jnp.swapaxes(kpe[n_p:].reshape(b_s, t_s, ROPE), 1, 2))
    o_lat = _decode(q_abs, q_pe_s, self_c, self_kt, cache_lat, jnp.swapaxes(cache_kr, 2, 3), page_table, e)
    w_uvh = jnp.transpose(w_uv, (1, 0, 2)).astype(BF16)
    o_s = _headmm(o_lat.reshape(n_s, N_HEADS * KV_LORA), w_uvh, BF16, "mla_o_up")

    o_all = jnp.concatenate([o_p, o_s], axis=0)
    kw = w_out.shape[1] // 2
    x = _mmw([(a_out, 0, kw), (o_all, 0, kw)], [[(w_out, e, 0), (w_out, e, 1)]], [(x, "ij")],
             functools.partial(_epi_resid, scale=1.0), F32, 1088, 256, "even_out")
    return x, v_rows, c, kpe


def _shift_rows(x, j):
    if j == 0:
        return x
    row = lax.broadcasted_iota(jnp.int32, x.shape, 0)
    return jnp.where(row >= j, pltpu.roll(x, j, 0), 0.0)


def _qkv_factor(y, j, tiles_per_part):
    nrm = lax.rsqrt(jnp.sum(y * y, axis=-1, keepdims=True) + EPS)
    part = j // tiles_per_part
    return jnp.where(part == 0, nrm * DK ** -0.5, jnp.where(part == 1, nrm, 1.0))


def _conv_c_prompt_body(b_ref, c_ref, x_ref, w_ref, y_ref, tail_ref):
    xg = c_ref[...] * x_ref[...]
    w = w_ref[...]
    nw = w.shape[0]
    conv = w[nw - 1:nw] * xg
    for j in range(1, nw):
        conv = conv + w[nw - 1 - j:nw - j] * _shift_rows(xg, j)
    y_ref[...] = (b_ref[...] * conv).astype(y_ref.dtype)
    n = xg.shape[0]
    tail_ref[...] = xg[n - 8:, :]


def _conv_c_prompt(z, w, n_seq, seq_len, d_c):
    tc = 256
    nb = d_c // tc
    return pl.pallas_call(
        _conv_c_prompt_body,
        grid=(n_seq, nb),
        in_specs=[pl.BlockSpec((seq_len, tc), lambda b, j: (b, j)),
                  pl.BlockSpec((seq_len, tc), lambda b, j: (b, nb + j)),
                  pl.BlockSpec((seq_len, tc), lambda b, j: (b, 2 * nb + j)),
                  pl.BlockSpec((w.shape[0], tc), lambda b, j: (0, j))],
        out_specs=[pl.BlockSpec((seq_len, tc), lambda b, j: (b, j)),
                   pl.BlockSpec((None, 8, tc), lambda b, j: (b, 0, j))],
        out_shape=[jax.ShapeDtypeStruct((n_seq * seq_len, d_c), BF16),
                   jax.ShapeDtypeStruct((n_seq, 8, d_c), F32)],
        compiler_params=_params(("parallel", "parallel"), 0),
        name="conv_c_prompt",
    )(z, z, z, w)


def _conv_d_prompt_body(x_ref, w_ref, y_ref, tail_ref, *, tiles_per_part):
    j = pl.program_id(1)
    x = x_ref[...]
    w = w_ref[...]
    nw = w.shape[0]
    conv = w[nw - 1:nw] * x
    for s in range(1, nw):
        conv = conv + w[nw - 1 - s:nw - s] * _shift_rows(x, s)
    y = _silu(conv)
    y_ref[...] = y * _qkv_factor(y, j, tiles_per_part)
    n = x.shape[0]
    tail_ref[...] = x[n - 8:, :]


def _conv_d_prompt(z, w, n_seq, seq_len, col0, d_qkv):
    tc = DK
    nb = d_qkv // tc
    off = col0 // tc
    return pl.pallas_call(
        functools.partial(_conv_d_prompt_body, tiles_per_part=nb // 3),
        grid=(n_seq, nb),
        in_specs=[pl.BlockSpec((seq_len, tc), lambda b, j: (b, off + j)),
                  pl.BlockSpec((w.shape[0], tc), lambda b, j: (0, j))],
        out_specs=[pl.BlockSpec((seq_len, tc), lambda b, j: (b, j)),
                   pl.BlockSpec((None, 8, tc), lambda b, j: (b, 0, j))],
        out_shape=[jax.ShapeDtypeStruct((n_seq * seq_len, d_qkv), F32),
                   jax.ShapeDtypeStruct((n_seq, 8, d_qkv), F32)],
        compiler_params=_params(("parallel", "parallel"), 0),
        name="conv_d_prompt",
    )(z, w)


def _conv_c_sample_body(b_ref, c_ref, x_ref, buf_ref, w_ref, y_ref, nbuf_ref):
    t_new = x_ref.shape[0]
    w = w_ref[...]
    nw = w.shape[0]
    xp = [buf_ref[s] for s in range(nw - 1)] + [c_ref[t] * x_ref[t] for t in range(t_new)]
    for t in range(t_new):
        conv = w[0:1] * xp[t]
        for s in range(1, nw):
            conv = conv + w[s:s + 1] * xp[t + s]
        y_ref[t] = (b_ref[t] * conv).astype(y_ref.dtype)
    for s in range(nw - 1):
        nbuf_ref[s] = xp[t_new + s]


def _conv_c_sample(zt, buf_t, w, d_c):
    t_new, n_seq, _ = zt.shape
    tc = 512
    nb = d_c // tc
    nw = w.shape[0]
    return pl.pallas_call(
        _conv_c_sample_body,
        grid=(nb,),
        in_specs=[pl.BlockSpec((t_new, n_seq, tc), lambda j: (0, 0, j)),
                  pl.BlockSpec((t_new, n_seq, tc), lambda j: (0, 0, nb + j)),
                  pl.BlockSpec((t_new, n_seq, tc), lambda j: (0, 0, 2 * nb + j)),
                  pl.BlockSpec((nw - 1, n_seq, tc), lambda j: (0, 0, j)),
                  pl.BlockSpec((nw, tc), lambda j: (0, j))],
        out_specs=[pl.BlockSpec((t_new, n_seq, tc), lambda j: (0, 0, j)),
                   pl.BlockSpec((nw - 1, n_seq, tc), lambda j: (0, 0, j))],
        out_shape=[jax.ShapeDtypeStruct((t_new, n_seq, d_c), BF16),
                   jax.ShapeDtypeStruct((nw - 1, n_seq, d_c), F32)],
        compiler_params=_params(("parallel",), 0),
        name="conv_c_sample",
    )(zt, zt, zt, buf_t, w)


def _conv_d_sample_body(x_ref, buf_ref, w_ref, y_ref, nbuf_ref, *, tiles_per_part):
    j = pl.program_id(0)
    t_new = x_ref.shape[0]
    w = w_ref[...]
    nw = w.shape[0]
    xp = [buf_ref[s] for s in range(nw - 1)] + [x_ref[t] for t in range(t_new)]
    for t in range(t_new):
        conv = w[0:1] * xp[t]
        for s in range(1, nw):
            conv = conv + w[s:s + 1] * xp[t + s]
        y = _silu(conv)
        y_ref[t] = y * _qkv_factor(y, j, tiles_per_part)
    for s in range(nw - 1):
        nbuf_ref[s] = xp[t_new + s]


def _conv_d_sample(zt, buf_t, w, col0, d_qkv):
    t_new, n_seq, _ = zt.shape
    tc = DK
    nb = d_qkv // tc
    off = col0 // tc
    nw = w.shape[0]
    return pl.pallas_call(
        functools.partial(_conv_d_sample_body, tiles_per_part=nb // 3),
        grid=(nb,),
        in_specs=[pl.BlockSpec((t_new, n_seq, tc), lambda j: (0, 0, off + j)),
                  pl.BlockSpec((nw - 1, n_seq, tc), lambda j: (0, 0, j)),
                  pl.BlockSpec((nw, tc), lambda j: (0, j))],
        out_specs=[pl.BlockSpec((t_new, n_seq, tc), lambda j: (0, 0, j)),
                   pl.BlockSpec((nw - 1, n_seq, tc), lambda j: (0, 0, j))],
        out_shape=[jax.ShapeDtypeStruct((t_new, n_seq, d_qkv), F32),
                   jax.ShapeDtypeStruct((nw - 1, n_seq, d_qkv), F32)],
        compiler_params=_params(("parallel",), 0),
        name="conv_d_sample",
    )(zt, buf_t, w)


def _cumsum_rows(x):
    n = x.shape[0]
    row = lax.broadcasted_iota(jnp.int32, x.shape, 0)
    s = 1
    while s < n:
        x = x + jnp.where(row >= s, pltpu.roll(x, s, 0), 0.0)
        s *= 2
    return x


def _split2(x):
    hi = x.astype(BF16)
    return hi, (x - hi.astype(F32)).astype(BF16)


def _hp_dup(a_parts, b_parts):
    a_hi, a_lo = a_parts
    b_hi, b_lo = b_parts
    lhs = jnp.concatenate([a_hi, a_lo], axis=1)
    rhs = jnp.concatenate([b_hi, b_lo, b_hi, jnp.zeros_like(b_hi)], axis=0)
    return _dot(lhs, rhs)


def _delta_body(q_ref, k_ref, v_ref, zg_ref, gate_ref, alog_ref, dt_ref, on_ref, s0_ref,
                o_ref, sout_ref, s_ref, *, chunk, n_valid, group):
    c = pl.program_id(1)
    nc = pl.num_programs(1)
    dup = 2 * chunk == 128
    width = 2 * chunk if dup else chunk

    @pl.when(c == 0)
    def _():
        s_ref[...] = s0_ref[...]

    gate = gate_ref[...]
    beta_all = _sigmoid(gate)
    x = gate + dt_ref[...]
    softplus = jnp.maximum(x, 0.0) + jnp.log(1.0 + jnp.exp(-jnp.abs(x)))
    g_all = -jnp.exp(alog_ref[...]) * softplus
    row128 = lax.broadcasted_iota(jnp.int32, (chunk, 128), 0)
    g_all = jnp.where(row128 < n_valid, g_all, 0.0)
    gcum = _cumsum_rows(g_all)
    eye = (lax.broadcasted_iota(jnp.int32, (128, 128), 0)
           == lax.broadcasted_iota(jnp.int32, (128, 128), 1)).astype(BF16)
    g_rows = jnp.concatenate([gcum, gcum], axis=0) if dup else gcum
    g_hi, g_mid, g_lo = _split3(g_rows)
    gcum_t = _dot_nt(eye, g_hi) + (_dot_nt(eye, g_mid) + _dot_nt(eye, g_lo))

    ri = lax.broadcasted_iota(jnp.int32, (chunk, width), 0)
    ci = lax.broadcasted_iota(jnp.int32, (chunk, width), 1)
    ci = jnp.where(ci >= chunk, ci - chunk, ci)
    incl = ri >= ci
    strict = ri > ci
    ident = (ri == ci).astype(F32)
    on = on_ref[...]
    nil = 1
    while nil < n_valid:
        nil *= 2
    nil = min(nil, chunk)

    for g0 in range(0, N_HEADS, group):
        heads = list(range(g0, g0 + group))
        pw, tm, qk = {}, {}, {}
        for h in heads:
            hs = slice(h * DK, (h + 1) * DK)
            q = q_ref[:, hs]
            k = k_ref[:, hs]
            beta = beta_all[:, h:h + 1]
            gc = gcum[:, N_HEADS + h:N_HEADS + h + 1]
            gr = gcum_t[N_HEADS + h:N_HEADS + h + 1, :]
            decay = jnp.exp(jnp.where(incl, gc - gr, -jnp.inf))
            k_b = k.astype(BF16)
            rhs = jnp.concatenate([k_b, k_b], axis=0) if dup else k_b
            lhs = jnp.concatenate([k * beta, q], axis=0).astype(BF16)
            r = _dot_nt(lhs, rhs)
            a = jnp.where(strict, r[:chunk] * decay, 0.0)
            qk[h] = jnp.where(incl[:, :chunk], r[chunk:, :chunk] * decay[:, :chunk], 0.0)
            pw[h] = -a
            tm[h] = ident + pw[h]
        p = 1
        while 2 * p < nil:
            for h in heads:
                if dup:
                    p2 = _split2(pw[h])
                    pw[h] = _hp_dup(p2, p2)
                else:
                    pw[h] = _dot_hp(pw[h], pw[h])
            for h in heads:
                if dup:
                    tm[h] = tm[h] + _hp_dup(_split2(tm[h]), _split2(pw[h]))
                else:
                    tm[h] = tm[h] + _dot_hp(tm[h], pw[h])
            p *= 2
        uw = {}
        for h in heads:
            hs = slice(h * DK, (h + 1) * DK)
            beta = beta_all[:, h:h + 1]
            gc = gcum[:, N_HEADS + h:N_HEADS + h + 1]
            kb = k_ref[:, hs] * beta
            rhs = jnp.concatenate([v_ref[:, hs] * beta, kb * jnp.exp(gc)], axis=1).astype(BF16)
            uw[h] = _dot(tm[h][:, :chunk].astype(BF16), rhs)
        ws = {}
        for h in heads:
            hs = slice(h * DK, (h + 1) * DK)
            gc = gcum[:, N_HEADS + h:N_HEADS + h + 1]
            lhs = jnp.concatenate([uw[h][:, DV:], q_ref[:, hs] * jnp.exp(gc)], axis=0).astype(BF16)
            ws[h] = _dot(lhs, s_ref[h].astype(BF16))
        for h in heads:
            hs = slice(h * DK, (h + 1) * DK)
            gc = gcum[:, N_HEADS + h:N_HEADS + h + 1]
            g_last = gcum[chunk - 1:chunk, N_HEADS + h:N_HEADS + h + 1]
            v_new = (uw[h][:, :DV] - ws[h][:chunk]).astype(BF16)
            o = ws[h][chunk:] + _dot(qk[h].astype(BF16), v_new)
            k_dec = k_ref[:, hs] * jnp.exp(g_last - gc)
            s_ref[h] = s_ref[h] * jnp.exp(g_last) + _dot_tn(k_dec.astype(BF16), v_new)
            o = o * lax.rsqrt(jnp.mean(o * o, axis=-1, keepdims=True) + EPS) * on
            o_ref[:, hs] = (o * _silu(zg_ref[:, hs])).astype(o_ref.dtype)

    @pl.when(c == nc - 1)
    def _():
        sout_ref[...] = s_ref[...]


def _delta(qkv, zg, zg_blk, gate, a_log, dt_bias, o_norm, s0, n_seq, chunk, n_valid):
    rows = qkv.shape[0]
    nc = rows // (n_seq * chunk)
    hw = N_HEADS * DK

    def rmap(blk):
        return lambda s, c: (s * nc + c, blk)

    alog = jnp.zeros((1, 128), F32).at[0, N_HEADS:2 * N_HEADS].set(a_log)
    dtb = jnp.zeros((1, 128), F32).at[0, N_HEADS:2 * N_HEADS].set(dt_bias)
    fix = lambda s, c: (0, 0)
    return pl.pallas_call(
        functools.partial(_delta_body, chunk=chunk, n_valid=n_valid, group=DELTA_GROUP),
        grid=(n_seq, nc),
        in_specs=[pl.BlockSpec((chunk, hw), rmap(0)),
                  pl.BlockSpec((chunk, hw), rmap(1)),
                  pl.BlockSpec((chunk, hw), rmap(2)),
                  pl.BlockSpec((chunk, hw), rmap(zg_blk)),
                  pl.BlockSpec((chunk, 128), rmap(0)),
                  pl.BlockSpec((1, 128), fix),
                  pl.BlockSpec((1, 128), fix),
                  pl.BlockSpec((1, DV), fix),
                  pl.BlockSpec((None, N_HEADS, DK, DV), lambda s, c: (s, 0, 0, 0))],
        out_specs=[pl.BlockSpec((chunk, hw), rmap(0)),
                   pl.BlockSpec((None, N_HEADS, DK, DV), lambda s, c: (s, 0, 0, 0))],
        out_shape=[jax.ShapeDtypeStruct((rows, hw), BF16),
                   jax.ShapeDtypeStruct((n_seq, N_HEADS, DK, DV), F32)],
        scratch_shapes=[pltpu.VMEM((N_HEADS, DK, DV), F32)],
        compiler_params=_params(("parallel", "arbitrary"), 0),
        name="gated_delta",
    )(qkv, qkv, qkv, zg, gate, alog, dtb, o_norm.reshape(1, DV), s0)


def _odd_mixer(x, dims, buf_c, buf_d, s0_s, w_in, conv_c_w, conv_d_w, a_log, dt_bias, o_norm,
               w_out, layer_idx, mix_norm):
    n_p, b_p, l_p, b_s, t_s = dims
    d = x.shape[1]
    d_c = conv_c_w.shape[1]
    d_qkv = conv_d_w.shape[1]
    n_in = w_in.shape[1]
    n_pad = -(-n_in // 512) * 512
    o_qkv = 3 * d_c
    o_gate = o_qkv + d_qkv
    o_tail = o_gate + N_HEADS * DV
    assert o_tail % 128 == 0 and n_pad - o_tail >= 128 and o_gate % (N_HEADS * DV) == 0

    h = _rms(x, mix_norm, BF16)
    w_b = jnp.pad(w_in.astype(BF16), ((0, 0), (0, n_pad - n_in)))
    z = _mm(h, [w_b], [], _epi_plain, F32, 1088, 512, "odd_in")

    y_c_p, tail_c_p = _conv_c_prompt(z, conv_c_w, b_p, l_p, d_c)
    qkv_p, tail_d_p = _conv_d_prompt(z, conv_d_w, b_p, l_p, o_qkv, d_qkv)
    chunk_p = math.gcd(l_p, DN_CHUNK)
    gate_p = z[:n_p, o_tail:o_tail + 128]
    s0_p = jnp.zeros((b_p, N_HEADS, DK, DV), F32)
    o_p, st_p = _delta(qkv_p, z, o_gate // (N_HEADS * DV), gate_p, a_log, dt_bias, o_norm, s0_p,
                       b_p, chunk_p, chunk_p)

    zt = jnp.transpose(z[n_p:].reshape(b_s, t_s, n_pad), (1, 0, 2))
    y_c_t, nbuf_c_t = _conv_c_sample(zt, jnp.transpose(buf_c, (1, 0, 2)), conv_c_w, d_c)
    qkv_t, nbuf_d_t = _conv_d_sample(zt, jnp.transpose(buf_d, (1, 0, 2)), conv_d_w, o_qkv, d_qkv)
    pad_t = SAMPLE_CHUNK - t_s
    assert pad_t >= 0

    def to_seq(a_t):
        a = jnp.transpose(a_t, (1, 0, 2))
        a = jnp.pad(a, ((0, 0), (0, pad_t), (0, 0)))
        return a.reshape(b_s * SAMPLE_CHUNK, a.shape[2])

    qkv_s = to_seq(qkv_t)
    zs = z[n_p:].reshape(b_s, t_s, n_pad)
    zg_s = jnp.pad(zs[:, :, o_gate:o_tail], ((0, 0), (0, pad_t), (0, 0))).reshape(b_s * SAMPLE_CHUNK, -1)
    gate_s = jnp.pad(zs[:, :, o_tail:o_tail + 128], ((0, 0), (0, pad_t), (0, 0))).reshape(b_s * SAMPLE_CHUNK, 128)
    o_s8, st_s = _delta(qkv_s, zg_s, 0, gate_s, a_log, dt_bias, o_norm, s0_s, b_s, SAMPLE_CHUNK, t_s)
    o_s = o_s8.reshape(b_s, SAMPLE_CHUNK, -1)[:, :t_s].reshape(b_s * t_s, -1)
    y_c_s = jnp.transpose(y_c_t, (1, 0, 2)).reshape(b_s * t_s, d_c)

    y_c = jnp.concatenate([y_c_p, y_c_s], axis=0)
    o_all = jnp.concatenate([o_p, o_s], axis=0)
    kw = w_out.shape[1] // 2
    x = _mmw([(y_c, 0, kw), (o_all, 0, kw)], [[(w_out, layer_idx, 0), (w_out, layer_idx, 1)]],
             [(x, "ij")], functools.partial(_epi_resid, scale=1.0), F32, 1088, 256, "odd_out")
    nw_c = conv_c_w.shape[0] - 1
    nw_d = conv_d_w.shape[0] - 1
    outs = (tail_c_p[:, 8 - nw_c:], jnp.transpose(nbuf_c_t, (1, 0, 2)),
            tail_d_p[:, 8 - nw_d:], jnp.transpose(nbuf_d_t, (1, 0, 2)), st_p, st_s)
    return x, outs


def _rope_tables(pos):
    half = ROPE // 2
    inv = ROPE_THETA ** (-jnp.arange(half, dtype=F32) / half)
    ang = pos[:, None] * inv[None, :]
    cos = jnp.cos(ang)
    sin = jnp.sin(ang)
    n = pos.shape[0]
    cos_f = jnp.concatenate([cos, cos], axis=1)
    sin_f = jnp.concatenate([-sin, sin], axis=1)
    z = jnp.zeros((n, 128 - ROPE), F32)
    cos_k = jnp.concatenate([cos_f, z], axis=1)
    sin_k = jnp.concatenate([sin_f, z], axis=1)
    cos_q = jnp.concatenate([jnp.ones((n, NOPE), F32), cos_f, z], axis=1)
    sin_q = jnp.concatenate([jnp.zeros((n, NOPE), F32), sin_f, z], axis=1)
    return cos_k, sin_k, cos_q, sin_q


def kernel(x_prompt, x_sample, cache_mla_latent, cache_mla_krope, state_conv_c, state_conv_d, state_delta, page_table, ffn1_norm, ffn1_w_gate, ffn1_w_up, ffn1_w_down, mix_norm, ffn2_norm, ffn2_w_gate, ffn2_w_up, ffn2_w_down, even_w_in, gmlp_v_norm, gmlp_ws, gmlp_bs, mla_q_norm, mla_w_uq, mla_kv_norm, mla_w_uk, mla_w_uv, even_w_out, odd_w_in, conv_c_w, conv_d_w, delta_a_log, delta_dt_bias, delta_o_norm, odd_w_out, final_norm):
    b_p, l_p, d = x_prompt.shape
    b_s, t_s, _ = x_sample.shape
    n_p = b_p * l_p
    n_s = b_s * t_s
    depth = ffn1_norm.shape[0]
    dims = (n_p, b_p, l_p, b_s, t_s)
    past_len = page_table.shape[1] * PAGE
    pos = jnp.concatenate([jnp.tile(jnp.arange(l_p, dtype=F32), b_p),
                           jnp.tile(jnp.arange(t_s, dtype=F32) + past_len, b_s)])
    tabs = _rope_tables(pos)

    x = jnp.concatenate([x_prompt.reshape(n_p, d), x_sample.reshape(n_s, d)], axis=0)
    lat, kr, vrow = [], [], []
    cc_p, cc_s, cd_p, cd_s, sd_p, sd_s = [], [], [], [], [], []
    for layer in range(depth):
        x = _ffn_half(x, ffn1_norm[layer], ffn1_w_gate, ffn1_w_up, ffn1_w_down, layer)
        if layer % 2 == 0:
            e = layer // 2
            x, v_rows, c, kpe = _even_mixer(
                x, dims, tabs, cache_mla_latent, cache_mla_krope, page_table, e, even_w_in[e],
                gmlp_v_norm[e], gmlp_ws[e], gmlp_bs[e], mla_q_norm[e], mla_w_uq[e], mla_kv_norm[e],
                mla_w_uk[e], mla_w_uv[e], even_w_out, mix_norm[layer])
            lat.append(c)
            kr.append(kpe)
            vrow.append(v_rows[n_p:])
        else:
            o = layer // 2
            x, outs = _odd_mixer(
                x, dims, state_conv_c[o], state_conv_d[o], state_delta[o], odd_w_in[o], conv_c_w[o],
                conv_d_w[o], delta_a_log[o], delta_dt_bias[o], delta_o_norm[o], odd_w_out, o,
                mix_norm[layer])
            cc_p.append(outs[0]); cc_s.append(outs[1]); cd_p.append(outs[2]); cd_s.append(outs[3])
            sd_p.append(outs[4]); sd_s.append(outs[5])
        x = _ffn_half(x, ffn2_norm[layer], ffn2_w_gate, ffn2_w_up, ffn2_w_down, layer)

    y = _rms(x, final_norm, F32)
    lat = jnp.stack(lat)
    kr = jnp.stack(kr)
    n_e = lat.shape[0]
    return (y[:n_p].reshape(b_p, l_p, d), y[n_p:].reshape(b_s, t_s, d),
            lat[:, :n_p].reshape(n_e, b_p, l_p, -1), kr[:, :n_p].reshape(n_e, b_p, l_p, -1),
            lat[:, n_p:].reshape(n_e, b_s, t_s, -1), kr[:, n_p:].reshape(n_e, b_s, t_s, -1),
            jnp.stack(vrow).reshape(n_e, b_s, t_s, -1),
            jnp.stack(cc_p), jnp.stack(cc_s), jnp.stack(cd_p), jnp.stack(cd_s),
            jnp.stack(sd_p), jnp.stack(sd_s))
```

```python
import functools
import math

import jax
import jax.numpy as jnp
from jax import lax
from jax.experimental import pallas as pl
from jax.experimental.pallas import tpu as pltpu

F32 = jnp.float32
BF16 = jnp.bfloat16
EPS = 1e-6

PAGE = 128
GROUP = 128
N_HEADS = 16
NOPE = 128
ROPE = 64
HEAD_PAD = 256
KV_LORA = 512
V_DIM = 128
DK = 128
DV = 128
ROPE_THETA = 10000.0
ATT_SCALE = (NOPE + ROPE) ** -0.5
DN_CHUNK = 64
SAMPLE_CHUNK = 8
PAGES_PER_STEP = 32
PAGES_PER_GROUP = 4
DELTA_GROUP = 16

VMEM_CAP_BYTES = 60 * 1024 * 1024


def _pick(n, target, mult=8):
    for d in range(min(n, target), 0, -1):
        if n % d == 0 and d % mult == 0:
            return d
    return n


def _params(sem, vmem_bytes):
    return pltpu.CompilerParams(
        dimension_semantics=sem,
        vmem_limit_bytes=int(min(VMEM_CAP_BYTES, max(vmem_bytes, 32 * 1024 * 1024))))


def _sigmoid(x):
    return 1.0 / (1.0 + jnp.exp(-x))


def _silu(x):
    return x * _sigmoid(x)


def _dot(a, b):
    return jnp.dot(a, b, preferred_element_type=F32)


def _dot_nt(a, b):
    return lax.dot_general(a, b, (((1,), (1,)), ((), ())), preferred_element_type=F32)


def _dot_tn(a, b):
    return lax.dot_general(a, b, (((0,), (0,)), ((), ())), preferred_element_type=F32)


def _split3(x):
    hi = x.astype(BF16)
    r = x - hi.astype(F32)
    mid = r.astype(BF16)
    lo = (r - mid.astype(F32)).astype(BF16)
    return hi, mid, lo


def _dot_hp(a, b):
    a_hi, a_mid, _ = _split3(a)
    b_hi, b_mid, _ = _split3(b)
    return _dot(a_hi, b_hi) + (_dot(a_hi, b_mid) + _dot(a_mid, b_hi))


def _rms_body(x_ref, g_ref, o_ref):
    x = x_ref[...]
    y = x * lax.rsqrt(jnp.mean(x * x, axis=-1, keepdims=True) + EPS)
    o_ref[...] = (y * g_ref[...]).astype(o_ref.dtype)


def _rms(x, g, out_dtype):
    m, d = x.shape
    tm = _pick(m, 256, 16)
    return pl.pallas_call(
        _rms_body,
        grid=(m // tm,),
        in_specs=[pl.BlockSpec((tm, d), lambda i: (i, 0)),
                  pl.BlockSpec((1, d), lambda i: (0, 0))],
        out_specs=pl.BlockSpec((tm, d), lambda i: (i, 0)),
        out_shape=jax.ShapeDtypeStruct((m, d), out_dtype),
        compiler_params=_params(("parallel",), 6 * tm * d * 4),
        name="rmsnorm",
    )(x, g.reshape(1, d))


def _mm_body(*refs, n_w, n_lhs, n_x, epi):
    lhs = [refs[l][...] for l in range(n_lhs)]
    accs = []
    for k in range(n_w):
        acc = _dot(lhs[0], refs[n_lhs + k * n_lhs][...])
        for l in range(1, n_lhs):
            acc = acc + _dot(lhs[l], refs[n_lhs + k * n_lhs + l][...])
        accs.append(acc)
    base = n_lhs + n_w * n_lhs
    xs = [refs[base + k][...] for k in range(n_x)]
    o_ref = refs[base + n_x]
    o_ref[...] = epi(accs, xs).astype(o_ref.dtype)


def _epi_plain(accs, xs):
    return accs[0]


def _epi_swiglu(accs, xs):
    return _silu(accs[0]) * accs[1]


def _epi_resid(accs, xs, scale):
    return xs[0] + scale * accs[0]


def _epi_rope(accs, xs):
    return accs[0] * xs[0] + accs[1] * xs[1]


def _mm(lhs, ws, extras, epi, out_dtype, tm_target, tn, name):
    if not isinstance(lhs, (list, tuple)):
        lhs, ws = [lhs], [[w] for w in ws]
    m = lhs[0].shape[0]
    n = ws[0][0].shape[1]
    tm = _pick(m, tm_target, 16)
    assert n % tn == 0, (n, tn)
    ktot = sum(a.shape[1] for a in lhs)
    in_specs = [pl.BlockSpec((tm, a.shape[1]), lambda i, j: (i, 0)) for a in lhs]
    for wl in ws:
        in_specs += [pl.BlockSpec((w.shape[0], tn), lambda i, j: (0, j)) for w in wl]
    for _, kind in extras:
        if kind == "ij":
            in_specs.append(pl.BlockSpec((tm, tn), lambda i, j: (i, j)))
        else:
            in_specs.append(pl.BlockSpec((tm, tn), lambda i, j: (i, 0)))
    vmem = 2 * (tm * ktot * 2 + len(ws) * ktot * tn * 2 + (len(extras) + 1) * tm * tn * 4)
    vmem += (len(ws) + 1) * tm * tn * 4 + (4 << 20)
    return pl.pallas_call(
        functools.partial(_mm_body, n_w=len(ws), n_lhs=len(lhs), n_x=len(extras), epi=epi),
        grid=(m // tm, n // tn),
        in_specs=in_specs,
        out_specs=pl.BlockSpec((tm, tn), lambda i, j: (i, j)),
        out_shape=jax.ShapeDtypeStruct((m, n), out_dtype),
        compiler_params=_params(("parallel", "arbitrary"), vmem),
        name=name,
    )(*lhs, *[w for wl in ws for w in wl], *[a for a, _ in extras])


def _headmm_body(l_ref, w_ref, o_ref):
    o_ref[...] = _dot(l_ref[...], w_ref[...]).astype(o_ref.dtype)


def _headmm(lhs, w, out_dtype, name):
    m = lhs.shape[0]
    nh, kd, nd = w.shape
    return pl.pallas_call(
        _headmm_body,
        grid=(nh,),
        in_specs=[pl.BlockSpec((m, kd), lambda h: (0, h)),
                  pl.BlockSpec((None, kd, nd), lambda h: (h, 0, 0))],
        out_specs=pl.BlockSpec((m, nd), lambda h: (0, h)),
        out_shape=jax.ShapeDtypeStruct((m, nh * nd), out_dtype),
        compiler_params=_params(("parallel",), 0),
        name=name,
    )(lhs, w)


def _mmw_body(*refs, n_acc, n_lhs, n_x, epi):
    n_w = n_acc * n_lhs
    lhs_refs = refs[:n_lhs]
    w_refs = refs[n_lhs:n_lhs + n_w]
    x_refs = refs[n_lhs + n_w:n_lhs + n_w + n_x]
    o_ref = refs[n_lhs + n_w + n_x]
    wb_refs = refs[n_lhs + n_w + n_x + 1:]

    @pl.when(pl.program_id(1) == 0)
    def _():
        for w_ref, wb_ref in zip(w_refs, wb_refs):
            wb_ref[...] = w_ref[...].astype(BF16)

    lhs = [r[...] for r in lhs_refs]
    accs = []
    for a in range(n_acc):
        acc = _dot(lhs[0], wb_refs[a * n_lhs][...])
        for l in range(1, n_lhs):
            acc = acc + _dot(lhs[l], wb_refs[a * n_lhs + l][...])
        accs.append(acc)
    o_ref[...] = epi(accs, [r[...] for r in x_refs]).astype(o_ref.dtype)


def _mmw(lhs_list, w_list, extras, epi, out_dtype, tm_target, tn, name):
    m = lhs_list[0][0].shape[0]
    n = w_list[0][0][0].shape[2]
    tm = _pick(m, tm_target, 16)
    assert n % tn == 0, (n, tn)
    n_lhs = len(lhs_list)
    in_specs, args = [], []
    vmem = 0
    for arr, cb, k in lhs_list:
        in_specs.append(pl.BlockSpec((tm, k), lambda j, i, cb=cb: (i, cb)))
        args.append(arr)
        vmem += 2 * tm * k * arr.dtype.itemsize
    scratch = []
    for ws in w_list:
        assert len(ws) == n_lhs
        for (arr, layer, rb), (_, _, k) in zip(ws, lhs_list):
            in_specs.append(pl.BlockSpec((None, k, tn), lambda j, i, layer=layer, rb=rb: (layer, rb, j)))
            args.append(arr)
            scratch.append(pltpu.VMEM((k, tn), BF16))
            vmem += 2 * k * tn * 4 + k * tn * 2
    for arr, kind in extras:
        if kind == "ij":
            in_specs.append(pl.BlockSpec((tm, tn), lambda j, i: (i, j)))
        else:
            in_specs.append(pl.BlockSpec((tm, tn), lambda j, i: (i, 0)))
        args.append(arr)
    vmem += (2 * (len(extras) + 1) + len(w_list) + 1) * tm * tn * 4 + (4 << 20)
    return pl.pallas_call(
        functools.partial(_mmw_body, n_acc=len(w_list), n_lhs=n_lhs, n_x=len(extras), epi=epi),
        grid=(n // tn, m // tm),
        in_specs=in_specs,
        out_specs=pl.BlockSpec((tm, tn), lambda j, i: (i, j)),
        out_shape=jax.ShapeDtypeStruct((m, n), out_dtype),
        scratch_shapes=scratch,
        compiler_params=_params(("parallel", "arbitrary"), vmem),
        name=name,
    )(*args)


def _ffn_half(x, g, wg, wu, wd, layer):
    d, f = wg.shape[1:]
    assert f % 256 == 0
    kh = f // 2
    assert kh % 128 == 0
    h = _rms(x, g, BF16)
    a = _mmw([(h, 0, d)], [[(wg, layer, 0)], [(wu, layer, 0)]], [], _epi_swiglu, BF16, 1088, 256,
             "ffn_gate_up")
    half = functools.partial(_epi_resid, scale=0.5)
    x = _mmw([(a, 0, kh)], [[(wd, layer, 0)]], [(x, "ij")], half, F32, 544, 512, "ffn_down_lo")
    return _mmw([(a, 1, kh)], [[(wd, layer, 1)]], [(x, "ij")], half, F32, 544, 512, "ffn_down_hi")


def _gmlp_body(z_ref, w_ref, b_ref, vn_ref, a_ref, v_ref, *, n_prompt_tiles, t_sample, d_a):
    i = pl.program_id(0)
    z = z_ref[...]
    c0 = math.sqrt(2.0 / math.pi)
    a = 0.5 * z * (1.0 + jnp.tanh(c0 * (z + 0.044715 * (z * z * z))))
    u = a[:, :d_a]
    v = a[:, d_a:]
    v = v * lax.rsqrt(jnp.mean(v * v, axis=-1, keepdims=True) + EPS) * vn_ref[...]
    v_ref[...] = v
    row = lax.broadcasted_iota(jnp.int32, (GROUP, GROUP), 0)
    col = lax.broadcasted_iota(jnp.int32, (GROUP, GROUP), 1)
    same_seq = (row // t_sample) == (col // t_sample)
    mask = (col <= row) & (same_seq | (i < n_prompt_tiles))
    for g in range(d_a // GROUP):
        sl = slice(g * GROUP, (g + 1) * GROUP)
        w = jnp.where(mask, w_ref[g], 0.0).astype(BF16)
        s = _dot(w, v[:, sl].astype(BF16)) + b_ref[:, sl]
        a_ref[:, sl] = (u[:, sl] * s).astype(a_ref.dtype)


def _gmlp(z_a, ws, bs, v_norm, n_prompt, t_sample):
    m = z_a.shape[0]
    d_a = z_a.shape[1] // 2
    ng = d_a // GROUP
    assert n_prompt % GROUP == 0 and (m - n_prompt) % GROUP == 0 and GROUP % t_sample == 0
    npt = n_prompt // GROUP
    rep = GROUP // t_sample
    w_all = jnp.stack([ws, jnp.tile(ws[:, :t_sample, :t_sample], (1, rep, rep))])
    b_p = jnp.repeat(bs.T, GROUP, axis=1)
    b_s = jnp.repeat(jnp.tile(bs[:, :t_sample].T, (rep, 1)), GROUP, axis=1)
    b_all = jnp.stack([b_p, b_s])

    def sel(i):
        return jnp.where(i < npt, 0, 1)

    return pl.pallas_call(
        functools.partial(_gmlp_body, n_prompt_tiles=npt, t_sample=t_sample, d_a=d_a),
        grid=(m // GROUP,),
        in_specs=[pl.BlockSpec((GROUP, 2 * d_a), lambda i: (i, 0)),
                  pl.BlockSpec((None, ng, GROUP, GROUP), lambda i: (sel(i), 0, 0, 0)),
                  pl.BlockSpec((None, GROUP, d_a), lambda i: (sel(i), 0, 0)),
                  pl.BlockSpec((1, d_a), lambda i: (0, 0))],
        out_specs=[pl.BlockSpec((GROUP, d_a), lambda i: (i, 0)),
                   pl.BlockSpec((GROUP, d_a), lambda i: (i, 0))],
        out_shape=[jax.ShapeDtypeStruct((m, d_a), BF16),
                   jax.ShapeDtypeStruct((m, d_a), F32)],
        compiler_params=_params(("parallel",), 0),
        name="gmlp",
    )(z_a, w_all, b_all, v_norm.reshape(1, d_a))


def _mla_prep_body(z_ref, qg_ref, kg_ref, cos_ref, sin_ref, qn_ref, c_ref, kpe_ref, ck_ref, *, q_lora):
    z = z_ref[...]
    zq = z[:, :q_lora]
    qn_ref[...] = (zq * lax.rsqrt(jnp.mean(zq * zq, axis=-1, keepdims=True) + EPS)
                   * qg_ref[...]).astype(qn_ref.dtype)
    zkv = z[:, q_lora:q_lora + KV_LORA]
    c = zkv * lax.rsqrt(jnp.mean(zkv * zkv, axis=-1, keepdims=True) + EPS) * kg_ref[...]
    c_ref[...] = c
    r0 = q_lora + KV_LORA
    kpe = z[:, r0:r0 + 128] * cos_ref[...] + z[:, r0 + 128:r0 + 256] * sin_ref[...]
    kpe_ref[...] = kpe[:, :ROPE]
    ck_ref[:, :KV_LORA] = c.astype(ck_ref.dtype)
    one = (lax.broadcasted_iota(jnp.int32, kpe.shape, 1) == ROPE).astype(F32)
    ck_ref[:, KV_LORA:] = (kpe + one).astype(ck_ref.dtype)


def _mla_prep(z_m, q_norm, kv_norm, cos128, sin128, q_lora):
    m, nz = z_m.shape
    tm = _pick(m, 544, 16)
    row = lambda i: (i, 0)
    fix = lambda i: (0, 0)
    return pl.pallas_call(
        functools.partial(_mla_prep_body, q_lora=q_lora),
        grid=(m // tm,),
        in_specs=[pl.BlockSpec((tm, nz), row),
                  pl.BlockSpec((1, q_lora), fix),
                  pl.BlockSpec((1, KV_LORA), fix),
                  pl.BlockSpec((tm, 128), row),
                  pl.BlockSpec((tm, 128), row)],
        out_specs=[pl.BlockSpec((tm, q_lora), row),
                   pl.BlockSpec((tm, KV_LORA), row),
                   pl.BlockSpec((tm, ROPE), row),
                   pl.BlockSpec((tm, KV_LORA + 128), row)],
        out_shape=[jax.ShapeDtypeStruct((m, q_lora), BF16),
                   jax.ShapeDtypeStruct((m, KV_LORA), F32),
                   jax.ShapeDtypeStruct((m, ROPE), F32),
                   jax.ShapeDtypeStruct((m, KV_LORA + 128), BF16)],
        compiler_params=_params(("parallel",), 0),
        name="mla_prep",
    )(z_m, q_norm.reshape(1, q_lora), kv_norm.reshape(1, KV_LORA), cos128, sin128)


def _attn_prompt_body(q_ref, k_ref, v_ref, o_ref, m_ref, l_ref, acc_ref, *, tq, tk):
    qi = pl.program_id(1)
    ki = pl.program_id(2)
    nk = pl.num_programs(2)

    @pl.when(ki == 0)
    def _():
        m_ref[...] = jnp.full(m_ref.shape, -jnp.inf, F32)
        l_ref[...] = jnp.zeros(l_ref.shape, F32)
        acc_ref[...] = jnp.zeros(acc_ref.shape, F32)

    def scores(h):
        qs = slice(h * HEAD_PAD, (h + 1) * HEAD_PAD)
        return _dot_nt(q_ref[:, qs], k_ref[:, qs]) * ATT_SCALE

    def process(diagonal):
        if diagonal:
            mask = (lax.broadcasted_iota(jnp.int32, (tq, tk), 1)
                    <= lax.broadcasted_iota(jnp.int32, (tq, tk), 0))
        s_next = scores(0)
        for h in range(N_HEADS):
            vs = slice(h * V_DIM, (h + 1) * V_DIM)
            s = s_next
            if h + 1 < N_HEADS:
                s_next = scores(h + 1)
            if diagonal:
                s = jnp.where(mask, s, -jnp.inf)
            m_old = m_ref[h]
            m_new = jnp.maximum(m_old, jnp.max(s, axis=-1, keepdims=True))
            corr = jnp.exp(m_old - m_new)
            p = jnp.concatenate([jnp.exp(s[:, t * 128:(t + 1) * 128] - m_new) for t in range(tk // 128)],
                                axis=1).astype(BF16)
            pv = _dot(p, v_ref[:, 2 * h * V_DIM:2 * (h + 1) * V_DIM])
            l_ref[h] = l_ref[h] * corr + pv[:, V_DIM:]
            acc_ref[:, vs] = acc_ref[:, vs] * corr + pv[:, :V_DIM]
            m_ref[h] = m_new

    @pl.when(ki < qi)
    def _():
        process(False)

    @pl.when(ki == qi)
    def _():
        process(True)

    @pl.when(ki == nk - 1)
    def _():
        for h in range(N_HEADS):
            vs = slice(h * V_DIM, (h + 1) * V_DIM)
            o_ref[:, vs] = (acc_ref[:, vs] / l_ref[h]).astype(o_ref.dtype)


def _attn_prompt(q_cat, kv, n_seq, seq_len):
    tq = _pick(seq_len, 256, 128)
    tk = tq
    nq = seq_len // tq
    qw = N_HEADS * HEAD_PAD
    vw = N_HEADS * V_DIM
    assert V_DIM == 128 and tk % 128 == 0 and qw == 2 * vw
    v_blk = 1

    def q_map(b, qi, ki):
        return (b * nq + qi, 0)

    def k_map(b, qi, ki):
        return (b * nq + jnp.minimum(ki, qi), 0)

    def v_map(b, qi, ki):
        return (b * nq + jnp.minimum(ki, qi), v_blk)

    return pl.pallas_call(
        functools.partial(_attn_prompt_body, tq=tq, tk=tk),
        grid=(n_seq, nq, nq),
        in_specs=[pl.BlockSpec((tq, qw), q_map),
                  pl.BlockSpec((tk, qw), k_map),
                  pl.BlockSpec((tk, 2 * vw), v_map)],
        out_specs=pl.BlockSpec((tq, vw), q_map),
        out_shape=jax.ShapeDtypeStruct((n_seq * seq_len, vw), BF16),
        scratch_shapes=[pltpu.VMEM((N_HEADS, tq, 128), F32),
                        pltpu.VMEM((N_HEADS, tq, 128), F32),
                        pltpu.VMEM((tq, vw), F32)],
        compiler_params=_params(("parallel", "parallel", "arbitrary"), 0),
        name="attn_prompt",
    )(q_cat, kv, kv)


def _decode_body(pt_ref, qa_ref, qp_ref, sc_ref, sk_ref, lat_hbm, kr_hbm, o_ref,
                 m_ref, l_ref, acc_ref, lat_buf, kr_buf, sem, *, pps, gsz, layer):
    b = pl.program_id(0)
    c = pl.program_id(1)
    nb = pl.num_programs(0)
    nc = pl.num_programs(1)
    step = b * nc + c
    slot = step % 2
    qa = qa_ref[...]
    qp = qp_ref[...]
    nrow = qa.shape[0]

    def page_copies(page, slot_, k):
        return (pltpu.make_async_copy(lat_hbm.at[layer, page], lat_buf.at[slot_, k], sem.at[0, slot_]),
                pltpu.make_async_copy(kr_hbm.at[layer, page], kr_buf.at[slot_, k], sem.at[1, slot_]))

    def start_step(bb, cc, slot_):
        for k in range(pps):
            for cp in page_copies(pt_ref[bb, cc * pps + k], slot_, k):
                cp.start()

    @pl.when(step == 0)
    def _():
        start_step(0, 0, 0)

    @pl.when(step + 1 < nb * nc)
    def _():
        wrap = c + 1 == nc
        start_step(jnp.where(wrap, b + 1, b), jnp.where(wrap, 0, c + 1), 1 - slot)

    for k in range(pps):
        for cp in page_copies(0, slot, k):
            cp.wait()
    lat_refs = [lat_buf.at[slot, k] for k in range(pps)]
    kr_refs = [kr_buf.at[slot, k] for k in range(pps)]

    def scores(kc, kpt):
        return (_dot_nt(qa, kc) + _dot(qp, kpt)) * ATT_SCALE

    def update(state, s, kc):
        m_old, l_old, acc = state
        m_new = jnp.maximum(m_old, jnp.max(s, axis=-1, keepdims=True))
        corr = jnp.exp(m_old - m_new)
        p = jnp.exp(s - m_new)
        l_new = l_old * corr + jnp.sum(p, axis=-1, keepdims=True)
        return m_new, l_new, acc * corr + _dot(p.astype(BF16), kc)

    def store(state):
        m_ref[...] = jnp.broadcast_to(state[0], m_ref.shape)
        l_ref[...] = jnp.broadcast_to(state[1], l_ref.shape)
        acc_ref[...] = state[2]

    @pl.when(c == 0)
    def _():
        key = lax.broadcasted_iota(jnp.int32, (nrow, PAGE), 1)
        tok = lax.broadcasted_iota(jnp.int32, (nrow, PAGE), 0) // N_HEADS
        kc = sc_ref[...].astype(BF16)
        s = jnp.where(key <= tok, scores(kc, sk_ref[...].astype(BF16)), -jnp.inf)
        init = (jnp.full((nrow, 1), -jnp.inf, F32), jnp.zeros((nrow, 1), F32),
                jnp.zeros((nrow, KV_LORA), F32))
        store(update(init, s, kc))

    def load_group(g):
        ks = range(g * gsz, (g + 1) * gsz)
        kc = jnp.concatenate([lat_refs[k][...].astype(BF16) for k in ks], axis=0)
        kpt = jnp.concatenate([kr_refs[k][...].astype(BF16) for k in ks], axis=1)
        return kc, kpt

    state = (m_ref[:, :1], l_ref[:, :1], acc_ref[...])
    kc, kpt = load_group(0)
    s = scores(kc, kpt)
    for g in range(pps // gsz):
        if g + 1 < pps // gsz:
            kc_next, kpt_next = load_group(g + 1)
            s_next = scores(kc_next, kpt_next)
        state = update(state, s, kc)
        if g + 1 < pps // gsz:
            kc, s = kc_next, s_next
    store(state)

    @pl.when(c == nc - 1)
    def _():
        o_ref[...] = (acc_ref[...] / l_ref[:, :1]).astype(o_ref.dtype)


def _decode(q_abs, q_pe, self_c, self_kt, cache_lat, cache_krt, page_table, layer):
    bs, nrow, _ = q_abs.shape
    n_pages = page_table.shape[1]
    pps = _pick(n_pages, PAGES_PER_STEP, 1)
    gsz = _pick(pps, PAGES_PER_GROUP, 1)
    nc = n_pages // pps

    def fix(b, c, pt):
        return (b, 0, 0)

    in_specs = [pl.BlockSpec((None, nrow, KV_LORA), fix),
                pl.BlockSpec((None, nrow, ROPE), fix),
                pl.BlockSpec((None, PAGE, KV_LORA), fix),
                pl.BlockSpec((None, ROPE, PAGE), fix),
                pl.BlockSpec(memory_space=pl.ANY),
                pl.BlockSpec(memory_space=pl.ANY)]
    grid_spec = pltpu.PrefetchScalarGridSpec(
        num_scalar_prefetch=1,
        grid=(bs, nc),
        in_specs=in_specs,
        out_specs=pl.BlockSpec((None, nrow, KV_LORA), fix),
        scratch_shapes=[pltpu.VMEM((nrow, 128), F32),
                        pltpu.VMEM((nrow, 128), F32),
                        pltpu.VMEM((nrow, KV_LORA), F32),
                        pltpu.VMEM((2, pps, PAGE, KV_LORA), F32),
                        pltpu.VMEM((2, pps, ROPE, PAGE), F32),
                        pltpu.SemaphoreType.DMA((2, 2))])
    vmem = 2 * pps * (PAGE * KV_LORA + ROPE * PAGE) * 4 + (16 << 20)
    return pl.pallas_call(
        functools.partial(_decode_body, pps=pps, gsz=gsz, layer=layer),
        grid_spec=grid_spec,
        out_shape=jax.ShapeDtypeStruct((bs, nrow, KV_LORA), BF16),
        compiler_params=_params(("arbitrary", "arbitrary"), vmem),
        name="attn_decode",
    )(page_table, q_abs, q_pe, self_c, self_kt, cache_lat, cache_krt)


def _even_mixer(x, dims, tabs, cache_lat, cache_kr, page_table, e, w_in, v_norm, ws, bs,
                q_norm, w_uq, kv_norm, w_uk, w_uv, w_out, mix_norm):
    n_p, b_p, l_p, b_s, t_s = dims
    cos_k, sin_k, cos_q, sin_q = tabs
    d = x.shape[1]
    d_a = v_norm.shape[0]
    q_lora = q_norm.shape[0]
    n_s = x.shape[0] - n_p
    half = ROPE // 2
    perm = jnp.concatenate([jnp.arange(half, ROPE), jnp.arange(0, half)])

    h = _rms(x, mix_norm, BF16)
    o_q = 2 * d_a
    o_kv = o_q + q_lora
    o_r = o_kv + KV_LORA
    w_r = w_in[:, o_r:o_r + ROPE]
    z64 = jnp.zeros((d, 128 - ROPE), F32)
    w_m = jnp.concatenate([w_in[:, o_q:o_r], w_r, z64, w_r[:, perm], z64], axis=1).astype(BF16)
    z_a = _mm(h, [w_in[:, :o_q].astype(BF16)], [], _epi_plain, F32, 1088, 512, "even_in_a")
    z_m = _mm(h, [w_m], [], _epi_plain, F32, 1088, 512, "even_in_m")

    a_out, v_rows = _gmlp(z_a, ws, bs, v_norm, n_p, t_s)
    qn, c, kpe, ck = _mla_prep(z_m, q_norm, kv_norm, cos_k, sin_k, q_lora)

    zpad = jnp.zeros((q_lora, N_HEADS, HEAD_PAD - NOPE - ROPE), F32)
    w1 = jnp.concatenate([w_uq, zpad], axis=2).reshape(q_lora, N_HEADS * HEAD_PAD).astype(BF16)
    w2 = jnp.concatenate([jnp.zeros((q_lora, N_HEADS, NOPE), F32), w_uq[:, :, NOPE:][:, :, perm], zpad],
                         axis=2).reshape(q_lora, N_HEADS * HEAD_PAD).astype(BF16)
    q_cat = _mm(qn, [w1, w2], [(cos_q, "i0"), (sin_q, "i0")], _epi_rope, BF16, 1088, HEAD_PAD, "mla_q")

    ckw = ck.shape[1]
    eye = jnp.eye(ROPE, dtype=F32)
    wk = jnp.zeros((ckw, N_HEADS, HEAD_PAD), F32)
    wk = wk.at[:KV_LORA, :, :NOPE].set(w_uk)
    wk = wk.at[KV_LORA:KV_LORA + ROPE, :, NOPE:NOPE + ROPE].set(jnp.broadcast_to(eye[:, None, :], (ROPE, N_HEADS, ROPE)))
    wv = jnp.zeros((ckw, N_HEADS, 2 * V_DIM), F32)
    wv = wv.at[:KV_LORA, :, :V_DIM].set(w_uv)
    wv = wv.at[KV_LORA + ROPE, :, V_DIM:].set(1.0)
    w_kv = jnp.concatenate([wk.reshape(ckw, N_HEADS * HEAD_PAD), wv.reshape(ckw, N_HEADS * 2 * V_DIM)],
                           axis=1).astype(BF16)
    kv_p = _mm(ck[:n_p], [w_kv], [], _epi_plain, BF16, 1024, 512, "mla_kv_up")
    o_p = _attn_prompt(q_cat, kv_p, b_p, l_p)

    qs = q_cat[n_p:].reshape(n_s, N_HEADS, HEAD_PAD)
    q_nope_s = qs[:, :, :NOPE].reshape(n_s, N_HEADS * NOPE)
    q_pe_s = qs[:, :, NOPE:NOPE + ROPE].reshape(b_s, t_s * N_HEADS, ROPE)
    w_ukt = jnp.transpose(w_uk, (1, 2, 0)).astype(BF16)
    q_abs = _headmm(q_nope_s, w_ukt, BF16, "mla_q_absorb").reshape(b_s, t_s * N_HEADS, KV_LORA)
    self_c = jnp.zeros((b_s, PAGE, KV_LORA), F32).at[:, :t_s].set(c[n_p:].reshape(b_s, t_s, KV_LORA))
    self_kt = jnp.zeros((b_s, ROPE, PAGE), F32).at[:, :, :t_s].set(
        jnp.swapaxes(kpe[n_p:].reshape(b_s, t_s, ROPE), 1, 2))
    o_lat = _decode(q_abs, q_pe_s, self_c, self_kt, cache_lat, jnp.swapaxes(cache_kr, 2, 3), page_table, e)
    w_uvh = jnp.transpose(w_uv, (1, 0, 2)).astype(BF16)
    o_s = _headmm(o_lat.reshape(n_s, N_HEADS * KV_LORA), w_uvh, BF16, "mla_o_up")

    o_all = jnp.concatenate([o_p, o_s], axis=0)
    kw = a_out.shape[1]
    x = _mm([a_out, o_all], [[w_out[e, :kw].astype(BF16), w_out[e, kw:].astype(BF16)]], [(x, "ij")],
            functools.partial(_epi_resid, scale=1.0), F32, 1088, 512, "even_out")
    return x, v_rows, c, kpe


def _shift_rows(x, j):
    if j == 0:
        return x
    row = lax.broadcasted_iota(jnp.int32, x.shape, 0)
    return jnp.where(row >= j, pltpu.roll(x, j, 0), 0.0)


def _qkv_factor(y, j, tiles_per_part):
    nrm = lax.rsqrt(jnp.sum(y * y, axis=-1, keepdims=True) + EPS)
    part = j // tiles_per_part
    return jnp.where(part == 0, nrm * DK ** -0.5, jnp.where(part == 1, nrm, 1.0))


def _conv_c_prompt_body(b_ref, c_ref, x_ref, w_ref, y_ref, tail_ref):
    xg = c_ref[...] * x_ref[...]
    w = w_ref[...]
    nw = w.shape[0]
    conv = w[nw - 1:nw] * xg
    for j in range(1, nw):
        conv = conv + w[nw - 1 - j:nw - j] * _shift_rows(xg, j)
    y_ref[...] = (b_ref[...] * conv).astype(y_ref.dtype)
    n = xg.shape[0]
    tail_ref[...] = xg[n - 8:, :]


def _conv_c_prompt(z, w, n_seq, seq_len, d_c):
    tc = 256
    nb = d_c // tc
    return pl.pallas_call(
        _conv_c_prompt_body,
        grid=(n_seq, nb),
        in_specs=[pl.BlockSpec((seq_len, tc), lambda b, j: (b, j)),
                  pl.BlockSpec((seq_len, tc), lambda b, j: (b, nb + j)),
                  pl.BlockSpec((seq_len, tc), lambda b, j: (b, 2 * nb + j)),
                  pl.BlockSpec((w.shape[0], tc), lambda b, j: (0, j))],
        out_specs=[pl.BlockSpec((seq_len, tc), lambda b, j: (b, j)),
                   pl.BlockSpec((None, 8, tc), lambda b, j: (b, 0, j))],
        out_shape=[jax.ShapeDtypeStruct((n_seq * seq_len, d_c), BF16),
                   jax.ShapeDtypeStruct((n_seq, 8, d_c), F32)],
        compiler_params=_params(("parallel", "parallel"), 0),
        name="conv_c_prompt",
    )(z, z, z, w)


def _conv_d_prompt_body(x_ref, w_ref, y_ref, tail_ref, *, tiles_per_part):
    j = pl.program_id(1)
    x = x_ref[...]
    w = w_ref[...]
    nw = w.shape[0]
    conv = w[nw - 1:nw] * x
    for s in range(1, nw):
        conv = conv + w[nw - 1 - s:nw - s] * _shift_rows(x, s)
    y = _silu(conv)
    y_ref[...] = y * _qkv_factor(y, j, tiles_per_part)
    n = x.shape[0]
    tail_ref[...] = x[n - 8:, :]


def _conv_d_prompt(z, w, n_seq, seq_len, col0, d_qkv):
    tc = DK
    nb = d_qkv // tc
    off = col0 // tc
    return pl.pallas_call(
        functools.partial(_conv_d_prompt_body, tiles_per_part=nb // 3),
        grid=(n_seq, nb),
        in_specs=[pl.BlockSpec((seq_len, tc), lambda b, j: (b, off + j)),
                  pl.BlockSpec((w.shape[0], tc), lambda b, j: (0, j))],
        out_specs=[pl.BlockSpec((seq_len, tc), lambda b, j: (b, j)),
                   pl.BlockSpec((None, 8, tc), lambda b, j: (b, 0, j))],
        out_shape=[jax.ShapeDtypeStruct((n_seq * seq_len, d_qkv), F32),
                   jax.ShapeDtypeStruct((n_seq, 8, d_qkv), F32)],
        compiler_params=_params(("parallel", "parallel"), 0),
        name="conv_d_prompt",
    )(z, w)


def _conv_c_sample_body(b_ref, c_ref, x_ref, buf_ref, w_ref, y_ref, nbuf_ref):
    t_new = x_ref.shape[0]
    w = w_ref[...]
    nw = w.shape[0]
    xp = [buf_ref[s] for s in range(nw - 1)] + [c_ref[t] * x_ref[t] for t in range(t_new)]
    for t in range(t_new):
        conv = w[0:1] * xp[t]
        for s in range(1, nw):
            conv = conv + w[s:s + 1] * xp[t + s]
        y_ref[t] = (b_ref[t] * conv).astype(y_ref.dtype)
    for s in range(nw - 1):
        nbuf_ref[s] = xp[t_new + s]


def _conv_c_sample(zt, buf_t, w, d_c):
    t_new, n_seq, _ = zt.shape
    tc = 512
    nb = d_c // tc
    nw = w.shape[0]
    return pl.pallas_call(
        _conv_c_sample_body,
        grid=(nb,),
        in_specs=[pl.BlockSpec((t_new, n_seq, tc), lambda j: (0, 0, j)),
                  pl.BlockSpec((t_new, n_seq, tc), lambda j: (0, 0, nb + j)),
                  pl.BlockSpec((t_new, n_seq, tc), lambda j: (0, 0, 2 * nb + j)),
                  pl.BlockSpec((nw - 1, n_seq, tc), lambda j: (0, 0, j)),
                  pl.BlockSpec((nw, tc), lambda j: (0, j))],
        out_specs=[pl.BlockSpec((t_new, n_seq, tc), lambda j: (0, 0, j)),
                   pl.BlockSpec((nw - 1, n_seq, tc), lambda j: (0, 0, j))],
        out_shape=[jax.ShapeDtypeStruct((t_new, n_seq, d_c), BF16),
                   jax.ShapeDtypeStruct((nw - 1, n_seq, d_c), F32)],
        compiler_params=_params(("parallel",), 0),
        name="conv_c_sample",
    )(zt, zt, zt, buf_t, w)


def _conv_d_sample_body(x_ref, buf_ref, w_ref, y_ref, nbuf_ref, *, tiles_per_part):
    j = pl.program_id(0)
    t_new = x_ref.shape[0]
    w = w_ref[...]
    nw = w.shape[0]
    xp = [buf_ref[s] for s in range(nw - 1)] + [x_ref[t] for t in range(t_new)]
    for t in range(t_new):
        conv = w[0:1] * xp[t]
        for s in range(1, nw):
            conv = conv + w[s:s + 1] * xp[t + s]
        y = _silu(conv)
        y_ref[t] = y * _qkv_factor(y, j, tiles_per_part)
    for s in range(nw - 1):
        nbuf_ref[s] = xp[t_new + s]


def _conv_d_sample(zt, buf_t, w, col0, d_qkv):
    t_new, n_seq, _ = zt.shape
    tc = DK
    nb = d_qkv // tc
    off = col0 // tc
    nw = w.shape[0]
    return pl.pallas_call(
        functools.partial(_conv_d_sample_body, tiles_per_part=nb // 3),
        grid=(nb,),
        in_specs=[pl.BlockSpec((t_new, n_seq, tc), lambda j: (0, 0, off + j)),
                  pl.BlockSpec((nw - 1, n_seq, tc), lambda j: (0, 0, j)),
                  pl.BlockSpec((nw, tc), lambda j: (0, j))],
        out_specs=[pl.BlockSpec((t_new, n_seq, tc), lambda j: (0, 0, j)),
                   pl.BlockSpec((nw - 1, n_seq, tc), lambda j: (0, 0, j))],
        out_shape=[jax.ShapeDtypeStruct((t_new, n_seq, d_qkv), F32),
                   jax.ShapeDtypeStruct((nw - 1, n_seq, d_qkv), F32)],
        compiler_params=_params(("parallel",), 0),
        name="conv_d_sample",
    )(zt, buf_t, w)


def _cumsum_rows(x):
    n = x.shape[0]
    row = lax.broadcasted_iota(jnp.int32, x.shape, 0)
    s = 1
    while s < n:
        x = x + jnp.where(row >= s, pltpu.roll(x, s, 0), 0.0)
        s *= 2
    return x


def _split2(x):
    hi = x.astype(BF16)
    return hi, (x - hi.astype(F32)).astype(BF16)


def _hp_dup(a_parts, b_parts):
    a_hi, a_lo = a_parts
    b_hi, b_lo = b_parts
    lhs = jnp.concatenate([a_hi, a_lo], axis=1)
    rhs = jnp.concatenate([b_hi, b_lo, b_hi, jnp.zeros_like(b_hi)], axis=0)
    return _dot(lhs, rhs)


def _delta_body(q_ref, k_ref, v_ref, zg_ref, gate_ref, alog_ref, dt_ref, on_ref, s0_ref,
                o_ref, sout_ref, s_ref, *, chunk, n_valid, group):
    c = pl.program_id(1)
    nc = pl.num_programs(1)
    dup = 2 * chunk == 128
    width = 2 * chunk if dup else chunk

    @pl.when(c == 0)
    def _():
        s_ref[...] = s0_ref[...]

    gate = gate_ref[...]
    beta_all = _sigmoid(gate)
    x = gate + dt_ref[...]
    softplus = jnp.maximum(x, 0.0) + jnp.log(1.0 + jnp.exp(-jnp.abs(x)))
    g_all = -jnp.exp(alog_ref[...]) * softplus
    row128 = lax.broadcasted_iota(jnp.int32, (chunk, 128), 0)
    g_all = jnp.where(row128 < n_valid, g_all, 0.0)
    gcum = _cumsum_rows(g_all)
    eye = (lax.broadcasted_iota(jnp.int32, (128, 128), 0)
           == lax.broadcasted_iota(jnp.int32, (128, 128), 1)).astype(BF16)
    g_rows = jnp.concatenate([gcum, gcum], axis=0) if dup else gcum
    g_hi, g_mid, g_lo = _split3(g_rows)
    gcum_t = _dot_nt(eye, g_hi) + (_dot_nt(eye, g_mid) + _dot_nt(eye, g_lo))

    ri = lax.broadcasted_iota(jnp.int32, (chunk, width), 0)
    ci = lax.broadcasted_iota(jnp.int32, (chunk, width), 1)
    ci = jnp.where(ci >= chunk, ci - chunk, ci)
    incl = ri >= ci
    strict = ri > ci
    ident = (ri == ci).astype(F32)
    on = on_ref[...]
    nil = 1
    while nil < n_valid:
        nil *= 2
    nil = min(nil, chunk)

    for g0 in range(0, N_HEADS, group):
        heads = list(range(g0, g0 + group))
        pw, tm, qk = {}, {}, {}
        for h in heads:
            hs = slice(h * DK, (h + 1) * DK)
            q = q_ref[:, hs]
            k = k_ref[:, hs]
            beta = beta_all[:, h:h + 1]
            gc = gcum[:, N_HEADS + h:N_HEADS + h + 1]
            gr = gcum_t[N_HEADS + h:N_HEADS + h + 1, :]
            decay = jnp.exp(jnp.where(incl, gc - gr, -jnp.inf))
            k_b = k.astype(BF16)
            rhs = jnp.concatenate([k_b, k_b], axis=0) if dup else k_b
            lhs = jnp.concatenate([k * beta, q], axis=0).astype(BF16)
            r = _dot_nt(lhs, rhs)
            a = jnp.where(strict, r[:chunk] * decay, 0.0)
            qk[h] = jnp.where(incl[:, :chunk], r[chunk:, :chunk] * decay[:, :chunk], 0.0)
            pw[h] = -a
            tm[h] = ident + pw[h]
        p = 1
        while 2 * p < nil:
            for h in heads:
                if dup:
                    p2 = _split2(pw[h])
                    pw[h] = _hp_dup(p2, p2)
                else:
                    pw[h] = _dot_hp(pw[h], pw[h])
            for h in heads:
                if dup:
                    tm[h] = tm[h] + _hp_dup(_split2(tm[h]), _split2(pw[h]))
                else:
                    tm[h] = tm[h] + _dot_hp(tm[h], pw[h])
            p *= 2
        uw = {}
        for h in heads:
            hs = slice(h * DK, (h + 1) * DK)
            beta = beta_all[:, h:h + 1]
            gc = gcum[:, N_HEADS + h:N_HEADS + h + 1]
            kb = k_ref[:, hs] * beta
            rhs = jnp.concatenate([v_ref[:, hs] * beta, kb * jnp.exp(gc)], axis=1).astype(BF16)
            uw[h] = _dot(tm[h][:, :chunk].astype(BF16), rhs)
        ws = {}
        for h in heads:
            hs = slice(h * DK, (h + 1) * DK)
            gc = gcum[:, N_HEADS + h:N_HEADS + h + 1]
            lhs = jnp.concatenate([uw[h][:, DV:], q_ref[:, hs] * jnp.exp(gc)], axis=0).astype(BF16)
            ws[h] = _dot(lhs, s_ref[h].astype(BF16))
        for h in heads:
            hs = slice(h * DK, (h + 1) * DK)
            gc = gcum[:, N_HEADS + h:N_HEADS + h + 1]
            g_last = gcum[chunk - 1:chunk, N_HEADS + h:N_HEADS + h + 1]
            v_new = (uw[h][:, :DV] - ws[h][:chunk]).astype(BF16)
            o = ws[h][chunk:] + _dot(qk[h].astype(BF16), v_new)
            k_dec = k_ref[:, hs] * jnp.exp(g_last - gc)
            s_ref[h] = s_ref[h] * jnp.exp(g_last) + _dot_tn(k_dec.astype(BF16), v_new)
            o = o * lax.rsqrt(jnp.mean(o * o, axis=-1, keepdims=True) + EPS) * on
            o_ref[:, hs] = (o * _silu(zg_ref[:, hs])).astype(o_ref.dtype)

    @pl.when(c == nc - 1)
    def _():
        sout_ref[...] = s_ref[...]


def _delta(qkv, zg, zg_blk, gate, a_log, dt_bias, o_norm, s0, n_seq, chunk, n_valid):
    rows = qkv.shape[0]
    nc = rows // (n_seq * chunk)
    hw = N_HEADS * DK

    def rmap(blk):
        return lambda s, c: (s * nc + c, blk)

    alog = jnp.zeros((1, 128), F32).at[0, N_HEADS:2 * N_HEADS].set(a_log)
    dtb = jnp.zeros((1, 128), F32).at[0, N_HEADS:2 * N_HEADS].set(dt_bias)
    fix = lambda s, c: (0, 0)
    return pl.pallas_call(
        functools.partial(_delta_body, chunk=chunk, n_valid=n_valid, group=DELTA_GROUP),
        grid=(n_seq, nc),
        in_specs=[pl.BlockSpec((chunk, hw), rmap(0)),
                  pl.BlockSpec((chunk, hw), rmap(1)),
                  pl.BlockSpec((chunk, hw), rmap(2)),
                  pl.BlockSpec((chunk, hw), rmap(zg_blk)),
                  pl.BlockSpec((chunk, 128), rmap(0)),
                  pl.BlockSpec((1, 128), fix),
                  pl.BlockSpec((1, 128), fix),
                  pl.BlockSpec((1, DV), fix),
                  pl.BlockSpec((None, N_HEADS, DK, DV), lambda s, c: (s, 0, 0, 0))],
        out_specs=[pl.BlockSpec((chunk, hw), rmap(0)),
                   pl.BlockSpec((None, N_HEADS, DK, DV), lambda s, c: (s, 0, 0, 0))],
        out_shape=[jax.ShapeDtypeStruct((rows, hw), BF16),
                   jax.ShapeDtypeStruct((n_seq, N_HEADS, DK, DV), F32)],
        scratch_shapes=[pltpu.VMEM((N_HEADS, DK, DV), F32)],
        compiler_params=_params(("parallel", "arbitrary"), 0),
        name="gated_delta",
    )(qkv, qkv, qkv, zg, gate, alog, dtb, o_norm.reshape(1, DV), s0)


def _odd_mixer(x, dims, buf_c, buf_d, s0_s, w_in, conv_c_w, conv_d_w, a_log, dt_bias, o_norm,
               w_out, layer_idx, mix_norm):
    n_p, b_p, l_p, b_s, t_s = dims
    d = x.shape[1]
    d_c = conv_c_w.shape[1]
    d_qkv = conv_d_w.shape[1]
    n_in = w_in.shape[1]
    n_pad = -(-n_in // 512) * 512
    o_qkv = 3 * d_c
    o_gate = o_qkv + d_qkv
    o_tail = o_gate + N_HEADS * DV
    assert o_tail % 128 == 0 and n_pad - o_tail >= 128 and o_gate % (N_HEADS * DV) == 0

    h = _rms(x, mix_norm, BF16)
    w_b = jnp.pad(w_in.astype(BF16), ((0, 0), (0, n_pad - n_in)))
    z = _mm(h, [w_b], [], _epi_plain, F32, 1088, 512, "odd_in")

    y_c_p, tail_c_p = _conv_c_prompt(z, conv_c_w, b_p, l_p, d_c)
    qkv_p, tail_d_p = _conv_d_prompt(z, conv_d_w, b_p, l_p, o_qkv, d_qkv)
    chunk_p = math.gcd(l_p, DN_CHUNK)
    gate_p = z[:n_p, o_tail:o_tail + 128]
    s0_p = jnp.zeros((b_p, N_HEADS, DK, DV), F32)
    o_p, st_p = _delta(qkv_p, z, o_gate // (N_HEADS * DV), gate_p, a_log, dt_bias, o_norm, s0_p,
                       b_p, chunk_p, chunk_p)

    zt = jnp.transpose(z[n_p:].reshape(b_s, t_s, n_pad), (1, 0, 2))
    y_c_t, nbuf_c_t = _conv_c_sample(zt, jnp.transpose(buf_c, (1, 0, 2)), conv_c_w, d_c)
    qkv_t, nbuf_d_t = _conv_d_sample(zt, jnp.transpose(buf_d, (1, 0, 2)), conv_d_w, o_qkv, d_qkv)
    pad_t = SAMPLE_CHUNK - t_s
    assert pad_t >= 0

    def to_seq(a_t):
        a = jnp.transpose(a_t, (1, 0, 2))
        a = jnp.pad(a, ((0, 0), (0, pad_t), (0, 0)))
        return a.reshape(b_s * SAMPLE_CHUNK, a.shape[2])

    qkv_s = to_seq(qkv_t)
    zs = z[n_p:].reshape(b_s, t_s, n_pad)
    zg_s = jnp.pad(zs[:, :, o_gate:o_tail], ((0, 0), (0, pad_t), (0, 0))).reshape(b_s * SAMPLE_CHUNK, -1)
    gate_s = jnp.pad(zs[:, :, o_tail:o_tail + 128], ((0, 0), (0, pad_t), (0, 0))).reshape(b_s * SAMPLE_CHUNK, 128)
    o_s8, st_s = _delta(qkv_s, zg_s, 0, gate_s, a_log, dt_bias, o_norm, s0_s, b_s, SAMPLE_CHUNK, t_s)
    o_s = o_s8.reshape(b_s, SAMPLE_CHUNK, -1)[:, :t_s].reshape(b_s * t_s, -1)
    y_c_s = jnp.transpose(y_c_t, (1, 0, 2)).reshape(b_s * t_s, d_c)

    y_c = jnp.concatenate([y_c_p, y_c_s], axis=0)
    o_all = jnp.concatenate([o_p, o_s], axis=0)
    kw = y_c.shape[1]
    x = _mm([y_c, o_all], [[w_out[layer_idx, :kw].astype(BF16), w_out[layer_idx, kw:].astype(BF16)]],
            [(x, "ij")], functools.partial(_epi_resid, scale=1.0), F32, 1088, 512, "odd_out")
    nw_c = conv_c_w.shape[0] - 1
    nw_d = conv_d_w.shape[0] - 1
    outs = (tail_c_p[:, 8 - nw_c:], jnp.transpose(nbuf_c_t, (1, 0, 2)),
            tail_d_p[:, 8 - nw_d:], jnp.transpose(nbuf_d_t, (1, 0, 2)), st_p, st_s)
    return x, outs


def _rope_tables(pos):
    half = ROPE // 2
    inv = ROPE_THETA ** (-jnp.arange(half, dtype=F32) / half)
    ang = pos[:, None] * inv[None, :]
    cos = jnp.cos(ang)
    sin = jnp.sin(ang)
    n = pos.shape[0]
    cos_f = jnp.concatenate([cos, cos], axis=1)
    sin_f = jnp.concatenate([-sin, sin], axis=1)
    z = jnp.zeros((n, 128 - ROPE), F32)
    cos_k = jnp.concatenate([cos_f, z], axis=1)
    sin_k = jnp.concatenate([sin_f, z], axis=1)
    cos_q = jnp.concatenate([jnp.ones((n, NOPE), F32), cos_f, z], axis=1)
    sin_q = jnp.concatenate([jnp.zeros((n, NOPE), F32), sin_f, z], axis=1)
    return cos_k, sin_k, cos_q, sin_q


def kernel(x_prompt, x_sample, cache_mla_latent, cache_mla_krope, state_conv_c, state_conv_d, state_delta, page_table, ffn1_norm, ffn1_w_gate, ffn1_w_up, ffn1_w_down, mix_norm, ffn2_norm, ffn2_w_gate, ffn2_w_up, ffn2_w_down, even_w_in, gmlp_v_norm, gmlp_ws, gmlp_bs, mla_q_norm, mla_w_uq, mla_kv_norm, mla_w_uk, mla_w_uv, even_w_out, odd_w_in, conv_c_w, conv_d_w, delta_a_log, delta_dt_bias, delta_o_norm, odd_w_out, final_norm):
    b_p, l_p, d = x_prompt.shape
    b_s, t_s, _ = x_sample.shape
    n_p = b_p * l_p
    n_s = b_s * t_s
    depth = ffn1_norm.shape[0]
    dims = (n_p, b_p, l_p, b_s, t_s)
    past_len = page_table.shape[1] * PAGE
    pos = jnp.concatenate([jnp.tile(jnp.arange(l_p, dtype=F32), b_p),
                           jnp.tile(jnp.arange(t_s, dtype=F32) + past_len, b_s)])
    tabs = _rope_tables(pos)

    x = jnp.concatenate([x_prompt.reshape(n_p, d), x_sample.reshape(n_s, d)], axis=0)
    lat, kr, vrow = [], [], []
    cc_p, cc_s, cd_p, cd_s, sd_p, sd_s = [], [], [], [], [], []
    for layer in range(depth):
        x = _ffn_half(x, ffn1_norm[layer], ffn1_w_gate, ffn1_w_up, ffn1_w_down, layer)
        if layer % 2 == 0:
            e = layer // 2
            x, v_rows, c, kpe = _even_mixer(
                x, dims, tabs, cache_mla_latent, cache_mla_krope, page_table, e, even_w_in[e],
                gmlp_v_norm[e], gmlp_ws[e], gmlp_bs[e], mla_q_norm[e], mla_w_uq[e], mla_kv_norm[e],
                mla_w_uk[e], mla_w_uv[e], even_w_out, mix_norm[layer])
            lat.append(c)
            kr.append(kpe)
            vrow.append(v_rows[n_p:])
        else:
            o = layer // 2
            x, outs = _odd_mixer(
                x, dims, state_conv_c[o], state_conv_d[o], state_delta[o], odd_w_in[o], conv_c_w[o],
                conv_d_w[o], delta_a_log[o], delta_dt_bias[o], delta_o_norm[o], odd_w_out, o,
                mix_norm[layer])
            cc_p.append(outs[0]); cc_s.append(outs[1]); cd_p.append(outs[2]); cd_s.append(outs[3])
            sd_p.append(outs[4]); sd_s.append(outs[5])
        x = _ffn_half(x, ffn2_norm[layer], ffn2_w_gate, ffn2_w_up, ffn2_w_down, layer)

    y = _rms(x, final_norm, F32)
    lat = jnp.stack(lat)
    kr = jnp.stack(kr)
    n_e = lat.shape[0]
    return (y[:n_p].reshape(b_p, l_p, d), y[n_p:].reshape(b_s, t_s, d),
            lat[:, :n_p].reshape(n_e, b_p, l_p, -1), kr[:, :n_p].reshape(n_e, b_p, l_p, -1),
            lat[:, n_p:].reshape(n_e, b_s, t_s, -1), kr[:, n_p:].reshape(n_e, b_s, t_s, -1),
            jnp.stack(vrow).reshape(n_e, b_s, t_s, -1),
            jnp.stack(cc_p), jnp.stack(cc_s), jnp.stack(cd_p), jnp.stack(cd_s),
            jnp.stack(sd_p), jnp.stack(sd_s))
```

```python
import functools
import math

import jax
import jax.numpy as jnp
from jax import lax
from jax.experimental import pallas as pl
from jax.experimental.pallas import tpu as pltpu

F32 = jnp.float32
BF16 = jnp.bfloat16
EPS = 1e-6

PAGE = 128
GROUP = 128
N_HEADS = 16
NOPE = 128
ROPE = 64
HEAD_PAD = 256
KV_LORA = 512
V_DIM = 128
DK = 128
DV = 128
ROPE_THETA = 10000.0
ATT_SCALE = (NOPE + ROPE) ** -0.5
DN_CHUNK = 64
SAMPLE_CHUNK = 8
PAGES_PER_STEP = 32
PAGES_PER_GROUP = 16
DELTA_GROUP = 16

VMEM_CAP_BYTES = 60 * 1024 * 1024


def _pick(n, target, mult=8):
    for d in range(min(n, target), 0, -1):
        if n % d == 0 and d % mult == 0:
            return d
    return n


def _params(sem, vmem_bytes):
    return pltpu.CompilerParams(
        dimension_semantics=sem,
        vmem_limit_bytes=int(min(VMEM_CAP_BYTES, max(vmem_bytes, 32 * 1024 * 1024))))


def _sigmoid(x):
    return 1.0 / (1.0 + jnp.exp(-x))


def _silu(x):
    return x * _sigmoid(x)


def _dot(a, b):
    return jnp.dot(a, b, preferred_element_type=F32)


def _dot_nt(a, b):
    return lax.dot_general(a, b, (((1,), (1,)), ((), ())), preferred_element_type=F32)


def _dot_tn(a, b):
    return lax.dot_general(a, b, (((0,), (0,)), ((), ())), preferred_element_type=F32)


def _split3(x):
    hi = x.astype(BF16)
    r = x - hi.astype(F32)
    mid = r.astype(BF16)
    lo = (r - mid.astype(F32)).astype(BF16)
    return hi, mid, lo


def _dot_hp(a, b):
    a_hi, a_mid, _ = _split3(a)
    b_hi, b_mid, _ = _split3(b)
    return _dot(a_hi, b_hi) + (_dot(a_hi, b_mid) + _dot(a_mid, b_hi))


def _rms_body(x_ref, g_ref, o_ref):
    x = x_ref[...]
    y = x * lax.rsqrt(jnp.mean(x * x, axis=-1, keepdims=True) + EPS)
    o_ref[...] = (y * g_ref[...]).astype(o_ref.dtype)


def _rms(x, g, out_dtype, row0=0, rows=None):
    d = x.shape[1]
    m = x.shape[0] if rows is None else rows
    tm = _pick(math.gcd(m, row0) if row0 else m, 256, 16)
    blk0 = row0 // tm
    return pl.pallas_call(
        _rms_body,
        grid=(m // tm,),
        in_specs=[pl.BlockSpec((tm, d), lambda i: (blk0 + i, 0)),
                  pl.BlockSpec((1, d), lambda i: (0, 0))],
        out_specs=pl.BlockSpec((tm, d), lambda i: (i, 0)),
        out_shape=jax.ShapeDtypeStruct((m, d), out_dtype),
        compiler_params=_params(("parallel",), 6 * tm * d * 4),
        name="rmsnorm",
    )(x, g.reshape(1, d))


def _mm_body(*refs, n_w, n_lhs, n_x, epi):
    lhs = [refs[l][...] for l in range(n_lhs)]
    accs = []
    for k in range(n_w):
        acc = _dot(lhs[0], refs[n_lhs + k * n_lhs][...])
        for l in range(1, n_lhs):
            acc = acc + _dot(lhs[l], refs[n_lhs + k * n_lhs + l][...])
        accs.append(acc)
    base = n_lhs + n_w * n_lhs
    xs = [refs[base + k][...] for k in range(n_x)]
    o_ref = refs[base + n_x]
    o_ref[...] = epi(accs, xs).astype(o_ref.dtype)


def _epi_plain(accs, xs):
    return accs[0]


def _epi_swiglu(accs, xs):
    return _silu(accs[0]) * accs[1]


def _epi_resid(accs, xs, scale):
    return xs[0] + scale * accs[0]


def _epi_rope(accs, xs):
    return accs[0] * xs[0] + accs[1] * xs[1]


def _mm(lhs, ws, extras, epi, out_dtype, tm_target, tn, name):
    if not isinstance(lhs, (list, tuple)):
        lhs, ws = [lhs], [[w] for w in ws]
    m = lhs[0].shape[0]
    n = ws[0][0].shape[1]
    tm = _pick(m, tm_target, 16)
    assert n % tn == 0, (n, tn)
    ktot = sum(a.shape[1] for a in lhs)
    in_specs = [pl.BlockSpec((tm, a.shape[1]), lambda i, j: (i, 0)) for a in lhs]
    for wl in ws:
        in_specs += [pl.BlockSpec((w.shape[0], tn), lambda i, j: (0, j)) for w in wl]
    for _, kind in extras:
        if kind == "ij":
            in_specs.append(pl.BlockSpec((tm, tn), lambda i, j: (i, j)))
        else:
            in_specs.append(pl.BlockSpec((tm, tn), lambda i, j: (i, 0)))
    vmem = 2 * (tm * ktot * 2 + len(ws) * ktot * tn * 2 + (len(extras) + 1) * tm * tn * 4)
    vmem += (len(ws) + 1) * tm * tn * 4 + (4 << 20)
    return pl.pallas_call(
        functools.partial(_mm_body, n_w=len(ws), n_lhs=len(lhs), n_x=len(extras), epi=epi),
        grid=(m // tm, n // tn),
        in_specs=in_specs,
        out_specs=pl.BlockSpec((tm, tn), lambda i, j: (i, j)),
        out_shape=jax.ShapeDtypeStruct((m, n), out_dtype),
        compiler_params=_params(("parallel", "arbitrary"), vmem),
        name=name,
    )(*lhs, *[w for wl in ws for w in wl], *[a for a, _ in extras])


def _headmm_body(l_ref, w_ref, o_ref):
    o_ref[...] = _dot(l_ref[...], w_ref[...]).astype(o_ref.dtype)


def _headmm(lhs, w, out_dtype, name):
    m = lhs.shape[0]
    nh, kd, nd = w.shape
    return pl.pallas_call(
        _headmm_body,
        grid=(nh,),
        in_specs=[pl.BlockSpec((m, kd), lambda h: (0, h)),
                  pl.BlockSpec((None, kd, nd), lambda h: (h, 0, 0))],
        out_specs=pl.BlockSpec((m, nd), lambda h: (0, h)),
        out_shape=jax.ShapeDtypeStruct((m, nh * nd), out_dtype),
        compiler_params=_params(("parallel",), 0),
        name=name,
    )(lhs, w)


def _mmw_body(*refs, n_acc, n_lhs, n_x, epi, w_t):
    n_w = n_acc * n_lhs
    lhs_refs = refs[:n_lhs]
    w_refs = refs[n_lhs:n_lhs + n_w]
    x_refs = refs[n_lhs + n_w:n_lhs + n_w + n_x]
    o_ref = refs[n_lhs + n_w + n_x]
    wb_refs = refs[n_lhs + n_w + n_x + 1:]

    @pl.when(pl.program_id(1) == 0)
    def _():
        for w_ref, wb_ref in zip(w_refs, wb_refs):
            wb_ref[...] = w_ref[...].astype(BF16)

    dot = _dot_nt if w_t else _dot
    lhs = [r[...] for r in lhs_refs]
    accs = []
    for a in range(n_acc):
        acc = dot(lhs[0], wb_refs[a * n_lhs][...])
        for l in range(1, n_lhs):
            acc = acc + dot(lhs[l], wb_refs[a * n_lhs + l][...])
        accs.append(acc)
    o_ref[...] = epi(accs, [r[...] for r in x_refs]).astype(o_ref.dtype)


def _mmw(lhs_list, w_list, extras, epi, out_dtype, tm_target, tn, name, w_t=False):
    m = lhs_list[0][0].shape[0]
    n = w_list[0][0][0].shape[1 if w_t else 2]
    tm = _pick(m, tm_target, 16)
    n_lhs = len(lhs_list)
    in_specs, args = [], []
    vmem = 0
    for arr, cb, k in lhs_list:
        in_specs.append(pl.BlockSpec((tm, k), lambda j, i, cb=cb: (i, cb)))
        args.append(arr)
        vmem += 2 * tm * k * arr.dtype.itemsize
    scratch = []
    for ws in w_list:
        assert len(ws) == n_lhs
        for (arr, layer, rb), (_, _, k) in zip(ws, lhs_list):
            if w_t:
                in_specs.append(pl.BlockSpec((None, tn, k), lambda j, i, layer=layer, rb=rb: (layer, j, rb)))
                scratch.append(pltpu.VMEM((tn, k), BF16))
            else:
                in_specs.append(pl.BlockSpec((None, k, tn), lambda j, i, layer=layer, rb=rb: (layer, rb, j)))
                scratch.append(pltpu.VMEM((k, tn), BF16))
            args.append(arr)
            vmem += 2 * k * tn * 4 + k * tn * 2
    for arr, kind in extras:
        if kind == "ij":
            in_specs.append(pl.BlockSpec((tm, tn), lambda j, i: (i, j)))
        else:
            in_specs.append(pl.BlockSpec((tm, tn), lambda j, i: (i, 0)))
        args.append(arr)
    vmem += (2 * (len(extras) + 1) + len(w_list) + 1) * tm * tn * 4 + (4 << 20)
    return pl.pallas_call(
        functools.partial(_mmw_body, n_acc=len(w_list), n_lhs=n_lhs, n_x=len(extras), epi=epi, w_t=w_t),
        grid=(pl.cdiv(n, tn), m // tm),
        in_specs=in_specs,
        out_specs=pl.BlockSpec((tm, tn), lambda j, i: (i, j)),
        out_shape=jax.ShapeDtypeStruct((m, n), out_dtype),
        scratch_shapes=scratch,
        compiler_params=_params(("parallel", "arbitrary"), vmem),
        name=name,
    )(*args)


def _ffn_half(x, g, wg, wu, wd, layer):
    d, f = wg.shape[1:]
    assert f % 256 == 0
    kh = f // 2
    assert kh % 128 == 0
    h = _rms(x, g, BF16)
    a = _mmw([(h, 0, d)], [[(wg, layer, 0)], [(wu, layer, 0)]], [], _epi_swiglu, BF16, 1088, 256,
             "ffn_gate_up")
    half = functools.partial(_epi_resid, scale=0.5)
    x = _mmw([(a, 0, kh)], [[(wd, layer, 0)]], [(x, "ij")], half, F32, 544, 512, "ffn_down_lo")
    return _mmw([(a, 1, kh)], [[(wd, layer, 1)]], [(x, "ij")], half, F32, 544, 512, "ffn_down_hi")


def _gmlp_body(z_ref, w_ref, b_ref, vn_ref, a_ref, v_ref, *, n_prompt_tiles, t_sample, d_a):
    i = pl.program_id(0)
    z = z_ref[...]
    c0 = math.sqrt(2.0 / math.pi)
    a = 0.5 * z * (1.0 + jnp.tanh(c0 * (z + 0.044715 * (z * z * z))))
    u = a[:, :d_a]
    v = a[:, d_a:]
    v = v * lax.rsqrt(jnp.mean(v * v, axis=-1, keepdims=True) + EPS) * vn_ref[...]
    v_ref[...] = v
    row = lax.broadcasted_iota(jnp.int32, (GROUP, GROUP), 0)
    col = lax.broadcasted_iota(jnp.int32, (GROUP, GROUP), 1)
    same_seq = (row // t_sample) == (col // t_sample)
    mask = (col <= row) & (same_seq | (i < n_prompt_tiles))
    for g in range(d_a // GROUP):
        sl = slice(g * GROUP, (g + 1) * GROUP)
        w = jnp.where(mask, w_ref[g], 0.0).astype(BF16)
        s = _dot(w, v[:, sl].astype(BF16)) + b_ref[:, sl]
        a_ref[:, sl] = (u[:, sl] * s).astype(a_ref.dtype)


def _gmlp(z_a, ws, bs, v_norm, n_prompt, t_sample):
    m = z_a.shape[0]
    d_a = z_a.shape[1] // 2
    ng = d_a // GROUP
    assert n_prompt % GROUP == 0 and (m - n_prompt) % GROUP == 0 and GROUP % t_sample == 0
    npt = n_prompt // GROUP
    rep = GROUP // t_sample
    w_all = jnp.stack([ws, jnp.tile(ws[:, :t_sample, :t_sample], (1, rep, rep))])
    b_p = jnp.repeat(bs.T, GROUP, axis=1)
    b_s = jnp.repeat(jnp.tile(bs[:, :t_sample].T, (rep, 1)), GROUP, axis=1)
    b_all = jnp.stack([b_p, b_s])

    def sel(i):
        return jnp.where(i < npt, 0, 1)

    return pl.pallas_call(
        functools.partial(_gmlp_body, n_prompt_tiles=npt, t_sample=t_sample, d_a=d_a),
        grid=(m // GROUP,),
        in_specs=[pl.BlockSpec((GROUP, 2 * d_a), lambda i: (i, 0)),
                  pl.BlockSpec((None, ng, GROUP, GROUP), lambda i: (sel(i), 0, 0, 0)),
                  pl.BlockSpec((None, GROUP, d_a), lambda i: (sel(i), 0, 0)),
                  pl.BlockSpec((1, d_a), lambda i: (0, 0))],
        out_specs=[pl.BlockSpec((GROUP, d_a), lambda i: (i, 0)),
                   pl.BlockSpec((GROUP, d_a), lambda i: (i, 0))],
        out_shape=[jax.ShapeDtypeStruct((m, d_a), BF16),
                   jax.ShapeDtypeStruct((m, d_a), F32)],
        compiler_params=_params(("parallel",), 0),
        name="gmlp",
    )(z_a, w_all, b_all, v_norm.reshape(1, d_a))


def _mla_prep_body(z_ref, qg_ref, kg_ref, cos_ref, sin_ref, qn_ref, c_ref, kpe_ref, ck_ref, *, q_lora):
    z = z_ref[...]
    zq = z[:, :q_lora]
    qn_ref[...] = (zq * lax.rsqrt(jnp.mean(zq * zq, axis=-1, keepdims=True) + EPS)
                   * qg_ref[...]).astype(qn_ref.dtype)
    zkv = z[:, q_lora:q_lora + KV_LORA]
    c = zkv * lax.rsqrt(jnp.mean(zkv * zkv, axis=-1, keepdims=True) + EPS) * kg_ref[...]
    c_ref[...] = c
    r0 = q_lora + KV_LORA
    kpe = z[:, r0:r0 + 128] * cos_ref[...] + z[:, r0 + 128:r0 + 256] * sin_ref[...]
    kpe_ref[...] = kpe[:, :ROPE]
    ck_ref[:, :KV_LORA] = c.astype(ck_ref.dtype)
    one = (lax.broadcasted_iota(jnp.int32, kpe.shape, 1) == ROPE).astype(F32)
    ck_ref[:, KV_LORA:] = (kpe + one).astype(ck_ref.dtype)


def _mla_prep(z_m, q_norm, kv_norm, cos128, sin128, q_lora):
    m, nz = z_m.shape
    tm = _pick(m, 544, 16)
    row = lambda i: (i, 0)
    fix = lambda i: (0, 0)
    return pl.pallas_call(
        functools.partial(_mla_prep_body, q_lora=q_lora),
        grid=(m // tm,),
        in_specs=[pl.BlockSpec((tm, nz), row),
                  pl.BlockSpec((1, q_lora), fix),
                  pl.BlockSpec((1, KV_LORA), fix),
                  pl.BlockSpec((tm, 128), row),
                  pl.BlockSpec((tm, 128), row)],
        out_specs=[pl.BlockSpec((tm, q_lora), row),
                   pl.BlockSpec((tm, KV_LORA), row),
                   pl.BlockSpec((tm, ROPE), row),
                   pl.BlockSpec((tm, KV_LORA + 128), row)],
        out_shape=[jax.ShapeDtypeStruct((m, q_lora), BF16),
                   jax.ShapeDtypeStruct((m, KV_LORA), F32),
                   jax.ShapeDtypeStruct((m, ROPE), F32),
                   jax.ShapeDtypeStruct((m, KV_LORA + 128), BF16)],
        compiler_params=_params(("parallel",), 0),
        name="mla_prep",
    )(z_m, q_norm.reshape(1, q_lora), kv_norm.reshape(1, KV_LORA), cos128, sin128)


def _attn_prompt_body(q_ref, k_ref, v_ref, o_ref, m_ref, l_ref, acc_ref, *, tq, tk):
    qi = pl.program_id(1)
    ki = pl.program_id(2)
    nk = pl.num_programs(2)

    @pl.when(ki == 0)
    def _():
        m_ref[...] = jnp.full(m_ref.shape, -jnp.inf, F32)
        l_ref[...] = jnp.zeros(l_ref.shape, F32)
        acc_ref[...] = jnp.zeros(acc_ref.shape, F32)

    def scores(h):
        qs = slice(h * HEAD_PAD, (h + 1) * HEAD_PAD)
        return _dot_nt(q_ref[:, qs], k_ref[:, qs]) * ATT_SCALE

    def process(diagonal):
        if diagonal:
            mask = (lax.broadcasted_iota(jnp.int32, (tq, tk), 1)
                    <= lax.broadcasted_iota(jnp.int32, (tq, tk), 0))
        s_next = scores(0)
        for h in range(N_HEADS):
            vs = slice(h * V_DIM, (h + 1) * V_DIM)
            s = s_next
            if h + 1 < N_HEADS:
                s_next = scores(h + 1)
            if diagonal:
                s = jnp.where(mask, s, -jnp.inf)
            m_old = m_ref[h]
            m_new = jnp.maximum(m_old, jnp.max(s, axis=-1, keepdims=True))
            corr = jnp.exp(m_old - m_new)
            p = jnp.concatenate([jnp.exp(s[:, t * 128:(t + 1) * 128] - m_new) for t in range(tk // 128)],
                                axis=1).astype(BF16)
            pv = _dot(p, v_ref[:, 2 * h * V_DIM:2 * (h + 1) * V_DIM])
            l_ref[h] = l_ref[h] * corr + pv[:, V_DIM:]
            acc_ref[:, vs] = acc_ref[:, vs] * corr + pv[:, :V_DIM]
            m_ref[h] = m_new

    @pl.when(ki < qi)
    def _():
        process(False)

    @pl.when(ki == qi)
    def _():
        process(True)

    @pl.when(ki == nk - 1)
    def _():
        for h in range(N_HEADS):
            vs = slice(h * V_DIM, (h + 1) * V_DIM)
            o_ref[:, vs] = (acc_ref[:, vs] / l_ref[h]).astype(o_ref.dtype)


def _attn_prompt(q_cat, kv, n_seq, seq_len):
    tq = _pick(seq_len, 256, 128)
    tk = tq
    nq = seq_len // tq
    qw = N_HEADS * HEAD_PAD
    vw = N_HEADS * V_DIM
    assert V_DIM == 128 and tk % 128 == 0 and qw == 2 * vw
    v_blk = 1

    def q_map(b, qi, ki):
        return (b * nq + qi, 0)

    def k_map(b, qi, ki):
        return (b * nq + jnp.minimum(ki, qi), 0)

    def v_map(b, qi, ki):
        return (b * nq + jnp.minimum(ki, qi), v_blk)

    return pl.pallas_call(
        functools.partial(_attn_prompt_body, tq=tq, tk=tk),
        grid=(n_seq, nq, nq),
        in_specs=[pl.BlockSpec((tq, qw), q_map),
                  pl.BlockSpec((tk, qw), k_map),
                  pl.BlockSpec((tk, 2 * vw), v_map)],
        out_specs=pl.BlockSpec((tq, vw), q_map),
        out_shape=jax.ShapeDtypeStruct((n_seq * seq_len, vw), BF16),
        scratch_shapes=[pltpu.VMEM((N_HEADS, tq, 128), F32),
                        pltpu.VMEM((N_HEADS, tq, 128), F32),
                        pltpu.VMEM((tq, vw), F32)],
        compiler_params=_params(("parallel", "parallel", "arbitrary"), 0),
        name="attn_prompt",
    )(q_cat, kv, kv)


def _decode_body(pt_ref, qa_ref, qp_ref, sc_ref, sk_ref, lat_hbm, kr_hbm, o_ref,
                 m_ref, l_ref, acc_ref, lat_buf, kr_buf, sem, *, pps, gsz, layer):
    b = pl.program_id(0)
    c = pl.program_id(1)
    nb = pl.num_programs(0)
    nc = pl.num_programs(1)
    step = b * nc + c
    slot = step % 2
    qa = qa_ref[...]
    qp = qp_ref[...]
    nrow = qa.shape[0]

    def page_copies(page, slot_, k):
        return (pltpu.make_async_copy(lat_hbm.at[layer, page], lat_buf.at[slot_, k], sem.at[0, slot_]),
                pltpu.make_async_copy(kr_hbm.at[layer, page], kr_buf.at[slot_, k], sem.at[1, slot_]))

    def start_step(bb, cc, slot_):
        for k in range(pps):
            for cp in page_copies(pt_ref[bb, cc * pps + k], slot_, k):
                cp.start()

    @pl.when(step == 0)
    def _():
        start_step(0, 0, 0)

    @pl.when(step + 1 < nb * nc)
    def _():
        wrap = c + 1 == nc
        start_step(jnp.where(wrap, b + 1, b), jnp.where(wrap, 0, c + 1), 1 - slot)

    for k in range(pps):
        for cp in page_copies(0, slot, k):
            cp.wait()
    lat_refs = [lat_buf.at[slot, k] for k in range(pps)]
    kr_refs = [kr_buf.at[slot, k] for k in range(pps)]

    def scores(kc, kpt):
        return (_dot_nt(qa, kc) + _dot(qp, kpt)) * ATT_SCALE

    def update(state, s, kc):
        m_old, l_old, acc = state
        m_new = jnp.maximum(m_old, jnp.max(s, axis=-1, keepdims=True))
        corr = jnp.exp(m_old - m_new)
        p = jnp.exp(s - m_new)
        l_new = l_old * corr + jnp.sum(p, axis=-1, keepdims=True)
        return m_new, l_new, acc * corr + _dot(p.astype(BF16), kc)

    def store(state):
        m_ref[...] = jnp.broadcast_to(state[0], m_ref.shape)
        l_ref[...] = jnp.broadcast_to(state[1], l_ref.shape)
        acc_ref[...] = state[2]

    @pl.when(c == 0)
    def _():
        key = lax.broadcasted_iota(jnp.int32, (nrow, PAGE), 1)
        tok = lax.broadcasted_iota(jnp.int32, (nrow, PAGE), 0) // N_HEADS
        kc = sc_ref[...].astype(BF16)
        s = jnp.where(key <= tok, scores(kc, sk_ref[...].astype(BF16)), -jnp.inf)
        init = (jnp.full((nrow, 1), -jnp.inf, F32), jnp.zeros((nrow, 1), F32),
                jnp.zeros((nrow, KV_LORA), F32))
        store(update(init, s, kc))

    def load_group(g):
        ks = range(g * gsz, (g + 1) * gsz)
        kc = jnp.concatenate([lat_refs[k][...].astype(BF16) for k in ks], axis=0)
        kpt = jnp.concatenate([kr_refs[k][...].astype(BF16) for k in ks], axis=1)
        return kc, kpt

    state = (m_ref[:, :1], l_ref[:, :1], acc_ref[...])
    kc, kpt = load_group(0)
    s = scores(kc, kpt)
    for g in range(pps // gsz):
        if g + 1 < pps // gsz:
            kc_next, kpt_next = load_group(g + 1)
            s_next = scores(kc_next, kpt_next)
        state = update(state, s, kc)
        if g + 1 < pps // gsz:
            kc, s = kc_next, s_next
    store(state)

    @pl.when(c == nc - 1)
    def _():
        o_ref[...] = (acc_ref[...] / l_ref[:, :1]).astype(o_ref.dtype)


def _decode(q_abs, q_pe, self_c, self_kt, cache_lat, cache_krt, page_table, layer):
    bs, nrow, _ = q_abs.shape
    n_pages = page_table.shape[1]
    pps = _pick(n_pages, PAGES_PER_STEP, 1)
    gsz = _pick(pps, PAGES_PER_GROUP, 1)
    nc = n_pages // pps

    def fix(b, c, pt):
        return (b, 0, 0)

    in_specs = [pl.BlockSpec((None, nrow, KV_LORA), fix),
                pl.BlockSpec((None, nrow, ROPE), fix),
                pl.BlockSpec((None, PAGE, KV_LORA), fix),
                pl.BlockSpec((None, ROPE, PAGE), fix),
                pl.BlockSpec(memory_space=pl.ANY),
                pl.BlockSpec(memory_space=pl.ANY)]
    grid_spec = pltpu.PrefetchScalarGridSpec(
        num_scalar_prefetch=1,
        grid=(bs, nc),
        in_specs=in_specs,
        out_specs=pl.BlockSpec((None, nrow, KV_LORA), fix),
        scratch_shapes=[pltpu.VMEM((nrow, 128), F32),
                        pltpu.VMEM((nrow, 128), F32),
                        pltpu.VMEM((nrow, KV_LORA), F32),
                        pltpu.VMEM((2, pps, PAGE, KV_LORA), F32),
                        pltpu.VMEM((2, pps, ROPE, PAGE), F32),
                        pltpu.SemaphoreType.DMA((2, 2))])
    vmem = 2 * pps * (PAGE * KV_LORA + ROPE * PAGE) * 4 + (16 << 20)
    return pl.pallas_call(
        functools.partial(_decode_body, pps=pps, gsz=gsz, layer=layer),
        grid_spec=grid_spec,
        out_shape=jax.ShapeDtypeStruct((bs, nrow, KV_LORA), BF16),
        compiler_params=_params(("arbitrary", "arbitrary"), vmem),
        name="attn_decode",
    )(page_table, q_abs, q_pe, self_c, self_kt, cache_lat, cache_krt)


def _even_mixer(x, dims, tabs, cache_lat, cache_kr, page_table, e, w_in, v_norm, ws, bs,
                q_norm, w_uq, kv_norm, w_uk, w_uv, w_out, mix_norm):
    n_p, b_p, l_p, b_s, t_s = dims
    cos_k, sin_k, cos_q, sin_q = tabs
    d = x.shape[1]
    d_a = v_norm.shape[0]
    q_lora = q_norm.shape[0]
    n_s = x.shape[0] - n_p
    half = ROPE // 2
    perm = jnp.concatenate([jnp.arange(half, ROPE), jnp.arange(0, half)])

    h = _rms(x, mix_norm, BF16)
    o_q = 2 * d_a
    o_kv = o_q + q_lora
    o_r = o_kv + KV_LORA
    w_r = w_in[:, o_r:o_r + ROPE]
    z64 = jnp.zeros((d, 128 - ROPE), F32)
    w_m = jnp.concatenate([w_in[:, o_q:o_r], w_r, z64, w_r[:, perm], z64], axis=1).astype(BF16)
    z_a = _mm(h, [w_in[:, :o_q].astype(BF16)], [], _epi_plain, F32, 1088, 512, "even_in_a")
    z_m = _mm(h, [w_m], [], _epi_plain, F32, 1088, 512, "even_in_m")

    a_out, v_rows = _gmlp(z_a, ws, bs, v_norm, n_p, t_s)
    qn, c, kpe, ck = _mla_prep(z_m, q_norm, kv_norm, cos_k, sin_k, q_lora)

    zpad = jnp.zeros((q_lora, N_HEADS, HEAD_PAD - NOPE - ROPE), F32)
    w1 = jnp.concatenate([w_uq, zpad], axis=2).reshape(q_lora, N_HEADS * HEAD_PAD).astype(BF16)
    w2 = jnp.concatenate([jnp.zeros((q_lora, N_HEADS, NOPE), F32), w_uq[:, :, NOPE:][:, :, perm], zpad],
                         axis=2).reshape(q_lora, N_HEADS * HEAD_PAD).astype(BF16)
    q_cat = _mm(qn, [w1, w2], [(cos_q, "i0"), (sin_q, "i0")], _epi_rope, BF16, 2176, HEAD_PAD, "mla_q")

    ckw = ck.shape[1]
    eye = jnp.eye(ROPE, dtype=F32)
    wk = jnp.zeros((ckw, N_HEADS, HEAD_PAD), F32)
    wk = wk.at[:KV_LORA, :, :NOPE].set(w_uk)
    wk = wk.at[KV_LORA:KV_LORA + ROPE, :, NOPE:NOPE + ROPE].set(jnp.broadcast_to(eye[:, None, :], (ROPE, N_HEADS, ROPE)))
    wv = jnp.zeros((ckw, N_HEADS, 2 * V_DIM), F32)
    wv = wv.at[:KV_LORA, :, :V_DIM].set(w_uv)
    wv = wv.at[KV_LORA + ROPE, :, V_DIM:].set(1.0)
    w_kv = jnp.concatenate([wk.reshape(ckw, N_HEADS * HEAD_PAD), wv.reshape(ckw, N_HEADS * 2 * V_DIM)],
                           axis=1).astype(BF16)
    kv_p = _mm(ck[:n_p], [w_kv], [], _epi_plain, BF16, 2048, 1024, "mla_kv_up")
    o_p = _attn_prompt(q_cat, kv_p, b_p, l_p)

    qs = q_cat[n_p:].reshape(n_s, N_HEADS, HEAD_PAD)
    q_nope_s = qs[:, :, :NOPE].reshape(n_s, N_HEADS * NOPE)
    q_pe_s = qs[:, :, NOPE:NOPE + ROPE].reshape(b_s, t_s * N_HEADS, ROPE)
    w_ukt = jnp.transpose(w_uk, (1, 2, 0)).astype(BF16)
    q_abs = _headmm(q_nope_s, w_ukt, BF16, "mla_q_absorb").reshape(b_s, t_s * N_HEADS, KV_LORA)
    self_c = jnp.zeros((b_s, PAGE, KV_LORA), F32).at[:, :t_s].set(c[n_p:].reshape(b_s, t_s, KV_LORA))
    self_kt = jnp.zeros((b_s, ROPE, PAGE), F32).at[:, :, :t_s].set(
        jnp.swapaxes(kpe[n_p:].reshape(b_s, t_s, ROPE), 1, 2))
    o_lat = _decode(q_abs, q_pe_s, self_c, self_kt, cache_lat, jnp.swapaxes(cache_kr, 2, 3), page_table, e)
    w_uvh = jnp.transpose(w_uv, (1, 0, 2)).astype(BF16)
    o_s = _headmm(o_lat.reshape(n_s, N_HEADS * KV_LORA), w_uvh, BF16, "mla_o_up")

    o_all = jnp.concatenate([o_p, o_s], axis=0)
    kw = a_out.shape[1]
    x = _mm([a_out, o_all], [[w_out[e, :kw].astype(BF16), w_out[e, kw:].astype(BF16)]], [(x, "ij")],
            functools.partial(_epi_resid, scale=1.0), F32, 1088, 512, "even_out")
    return x, v_rows, c, kpe


def _shift_rows(x, j):
    if j == 0:
        return x
    row = lax.broadcasted_iota(jnp.int32, x.shape, 0)
    return jnp.where(row >= j, pltpu.roll(x, j, 0), 0.0)


def _qkv_factor(y, j, tiles_per_part):
    nrm = lax.rsqrt(jnp.sum(y * y, axis=-1, keepdims=True) + EPS)
    part = j // tiles_per_part
    return jnp.where(part == 0, nrm * DK ** -0.5, jnp.where(part == 1, nrm, 1.0))


def _conv_c_prompt_body(b_ref, c_ref, x_ref, w_ref, y_ref, tail_ref):
    xg = c_ref[...] * x_ref[...]
    w = w_ref[...]
    nw = w.shape[0]
    conv = w[nw - 1:nw] * xg
    for j in range(1, nw):
        conv = conv + w[nw - 1 - j:nw - j] * _shift_rows(xg, j)
    y_ref[...] = (b_ref[...] * conv).astype(y_ref.dtype)
    n = xg.shape[0]
    tail_ref[...] = xg[n - 8:, :]


def _conv_c_prompt(z, w, n_seq, seq_len, d_c):
    tc = 256
    nb = d_c // tc
    return pl.pallas_call(
        _conv_c_prompt_body,
        grid=(n_seq, nb),
        in_specs=[pl.BlockSpec((seq_len, tc), lambda b, j: (b, j)),
                  pl.BlockSpec((seq_len, tc), lambda b, j: (b, nb + j)),
                  pl.BlockSpec((seq_len, tc), lambda b, j: (b, 2 * nb + j)),
                  pl.BlockSpec((w.shape[0], tc), lambda b, j: (0, j))],
        out_specs=[pl.BlockSpec((seq_len, tc), lambda b, j: (b, j)),
                   pl.BlockSpec((None, 8, tc), lambda b, j: (b, 0, j))],
        out_shape=[jax.ShapeDtypeStruct((n_seq * seq_len, d_c), BF16),
                   jax.ShapeDtypeStruct((n_seq, 8, d_c), F32)],
        compiler_params=_params(("parallel", "parallel"), 0),
        name="conv_c_prompt",
    )(z, z, z, w)


def _conv_d_prompt_body(x_ref, w_ref, y_ref, tail_ref, *, tiles_per_part):
    j = pl.program_id(1)
    x = x_ref[...]
    w = w_ref[...]
    nw = w.shape[0]
    conv = w[nw - 1:nw] * x
    for s in range(1, nw):
        conv = conv + w[nw - 1 - s:nw - s] * _shift_rows(x, s)
    y = _silu(conv)
    y_ref[...] = y * _qkv_factor(y, j, tiles_per_part)
    n = x.shape[0]
    tail_ref[...] = x[n - 8:, :]


def _conv_d_prompt(z, w, n_seq, seq_len, col0, d_qkv):
    tc = DK
    nb = d_qkv // tc
    off = col0 // tc
    return pl.pallas_call(
        functools.partial(_conv_d_prompt_body, tiles_per_part=nb // 3),
        grid=(n_seq, nb),
        in_specs=[pl.BlockSpec((seq_len, tc), lambda b, j: (b, off + j)),
                  pl.BlockSpec((w.shape[0], tc), lambda b, j: (0, j))],
        out_specs=[pl.BlockSpec((seq_len, tc), lambda b, j: (b, j)),
                   pl.BlockSpec((None, 8, tc), lambda b, j: (b, 0, j))],
        out_shape=[jax.ShapeDtypeStruct((n_seq * seq_len, d_qkv), F32),
                   jax.ShapeDtypeStruct((n_seq, 8, d_qkv), F32)],
        compiler_params=_params(("parallel", "parallel"), 0),
        name="conv_d_prompt",
    )(z, w)


def _conv_c_sample_body(b_ref, c_ref, x_ref, buf_ref, w_ref, y_ref, nbuf_ref):
    t_new = x_ref.shape[0]
    w = w_ref[...]
    nw = w.shape[0]
    xp = [buf_ref[s] for s in range(nw - 1)] + [c_ref[t] * x_ref[t] for t in range(t_new)]
    for t in range(t_new):
        conv = w[0:1] * xp[t]
        for s in range(1, nw):
            conv = conv + w[s:s + 1] * xp[t + s]
        y_ref[t] = (b_ref[t] * conv).astype(y_ref.dtype)
    for s in range(nw - 1):
        nbuf_ref[s] = xp[t_new + s]


def _conv_c_sample(zt, buf_t, w, d_c):
    t_new, n_seq, _ = zt.shape
    tc = 512
    nb = d_c // tc
    nw = w.shape[0]
    return pl.pallas_call(
        _conv_c_sample_body,
        grid=(nb,),
        in_specs=[pl.BlockSpec((t_new, n_seq, tc), lambda j: (0, 0, j)),
                  pl.BlockSpec((t_new, n_seq, tc), lambda j: (0, 0, nb + j)),
                  pl.BlockSpec((t_new, n_seq, tc), lambda j: (0, 0, 2 * nb + j)),
                  pl.BlockSpec((nw - 1, n_seq, tc), lambda j: (0, 0, j)),
                  pl.BlockSpec((nw, tc), lambda j: (0, j))],
        out_specs=[pl.BlockSpec((t_new, n_seq, tc), lambda j: (0, 0, j)),
                   pl.BlockSpec((nw - 1, n_seq, tc), lambda j: (0, 0, j))],
        out_shape=[jax.ShapeDtypeStruct((t_new, n_seq, d_c), BF16),
                   jax.ShapeDtypeStruct((nw - 1, n_seq, d_c), F32)],
        compiler_params=_params(("parallel",), 0),
        name="conv_c_sample",
    )(zt, zt, zt, buf_t, w)


def _conv_d_sample_body(x_ref, buf_ref, w_ref, y_ref, nbuf_ref, *, tiles_per_part):
    j = pl.program_id(0)
    t_new = x_ref.shape[0]
    w = w_ref[...]
    nw = w.shape[0]
    xp = [buf_ref[s] for s in range(nw - 1)] + [x_ref[t] for t in range(t_new)]
    for t in range(t_new):
        conv = w[0:1] * xp[t]
        for s in range(1, nw):
            conv = conv + w[s:s + 1] * xp[t + s]
        y = _silu(conv)
        y_ref[t] = y * _qkv_factor(y, j, tiles_per_part)
    for s in range(nw - 1):
        nbuf_ref[s] = xp[t_new + s]


def _conv_d_sample(zt, buf_t, w, col0, d_qkv):
    t_new, n_seq, _ = zt.shape
    tc = DK
    nb = d_qkv // tc
    off = col0 // tc
    nw = w.shape[0]
    return pl.pallas_call(
        functools.partial(_conv_d_sample_body, tiles_per_part=nb // 3),
        grid=(nb,),
        in_specs=[pl.BlockSpec((t_new, n_seq, tc), lambda j: (0, 0, off + j)),
                  pl.BlockSpec((nw - 1, n_seq, tc), lambda j: (0, 0, j)),
                  pl.BlockSpec((nw, tc), lambda j: (0, j))],
        out_specs=[pl.BlockSpec((t_new, n_seq, tc), lambda j: (0, 0, j)),
                   pl.BlockSpec((nw - 1, n_seq, tc), lambda j: (0, 0, j))],
        out_shape=[jax.ShapeDtypeStruct((t_new, n_seq, d_qkv), F32),
                   jax.ShapeDtypeStruct((nw - 1, n_seq, d_qkv), F32)],
        compiler_params=_params(("parallel",), 0),
        name="conv_d_sample",
    )(zt, buf_t, w)


def _cumsum_rows(x):
    n = x.shape[0]
    row = lax.broadcasted_iota(jnp.int32, x.shape, 0)
    s = 1
    while s < n:
        x = x + jnp.where(row >= s, pltpu.roll(x, s, 0), 0.0)
        s *= 2
    return x


def _split2(x):
    hi = x.astype(BF16)
    return hi, (x - hi.astype(F32)).astype(BF16)


def _hp_dup(a_parts, b_parts):
    a_hi, a_lo = a_parts
    b_hi, b_lo = b_parts
    lhs = jnp.concatenate([a_hi, a_lo], axis=1)
    rhs = jnp.concatenate([b_hi, b_lo, b_hi, jnp.zeros_like(b_hi)], axis=0)
    return _dot(lhs, rhs)


def _delta_body(q_ref, k_ref, v_ref, zg_ref, gate_ref, alog_ref, dt_ref, on_ref, s0_ref,
                o_ref, sout_ref, s_ref, *, chunk, n_valid, group):
    c = pl.program_id(1)
    nc = pl.num_programs(1)
    dup = 2 * chunk == 128
    width = 2 * chunk if dup else chunk

    @pl.when(c == 0)
    def _():
        s_ref[...] = s0_ref[...]

    gate = gate_ref[...]
    beta_all = _sigmoid(gate)
    x = gate + dt_ref[...]
    softplus = jnp.maximum(x, 0.0) + jnp.log(1.0 + jnp.exp(-jnp.abs(x)))
    g_all = -jnp.exp(alog_ref[...]) * softplus
    row128 = lax.broadcasted_iota(jnp.int32, (chunk, 128), 0)
    g_all = jnp.where(row128 < n_valid, g_all, 0.0)
    gcum = _cumsum_rows(g_all)
    eye = (lax.broadcasted_iota(jnp.int32, (128, 128), 0)
           == lax.broadcasted_iota(jnp.int32, (128, 128), 1)).astype(BF16)
    g_rows = jnp.concatenate([gcum, gcum], axis=0) if dup else gcum
    g_hi, g_mid, g_lo = _split3(g_rows)
    gcum_t = _dot_nt(eye, g_hi) + (_dot_nt(eye, g_mid) + _dot_nt(eye, g_lo))

    ri = lax.broadcasted_iota(jnp.int32, (chunk, width), 0)
    ci = lax.broadcasted_iota(jnp.int32, (chunk, width), 1)
    ci = jnp.where(ci >= chunk, ci - chunk, ci)
    incl = ri >= ci
    strict = ri > ci
    ident = (ri == ci).astype(F32)
    on = on_ref[...]
    nil = 1
    while nil < n_valid:
        nil *= 2
    nil = min(nil, chunk)

    for g0 in range(0, N_HEADS, group):
        heads = list(range(g0, g0 + group))
        pw, tm, qk = {}, {}, {}
        for h in heads:
            hs = slice(h * DK, (h + 1) * DK)
            q = q_ref[:, hs]
            k = k_ref[:, hs]
            beta = beta_all[:, h:h + 1]
            gc = gcum[:, N_HEADS + h:N_HEADS + h + 1]
            gr = gcum_t[N_HEADS + h:N_HEADS + h + 1, :]
            decay = jnp.exp(jnp.where(incl, gc - gr, -jnp.inf))
            k_b = k.astype(BF16)
            rhs = jnp.concatenate([k_b, k_b], axis=0) if dup else k_b
            lhs = jnp.concatenate([k * beta, q], axis=0).astype(BF16)
            r = _dot_nt(lhs, rhs)
            a = jnp.where(strict, r[:chunk] * decay, 0.0)
            qk[h] = jnp.where(incl[:, :chunk], r[chunk:, :chunk] * decay[:, :chunk], 0.0)
            pw[h] = -a
            tm[h] = ident + pw[h]
        p = 1
        while 2 * p < nil:
            for h in heads:
                if dup:
                    p2 = _split2(pw[h])
                    pw[h] = _hp_dup(p2, p2)
                else:
                    pw[h] = _dot_hp(pw[h], pw[h])
            for h in heads:
                if dup:
                    tm[h] = tm[h] + _hp_dup(_split2(tm[h]), _split2(pw[h]))
                else:
                    tm[h] = tm[h] + _dot_hp(tm[h], pw[h])
            p *= 2
        uw = {}
        for h in heads:
            hs = slice(h * DK, (h + 1) * DK)
            beta = beta_all[:, h:h + 1]
            gc = gcum[:, N_HEADS + h:N_HEADS + h + 1]
            kb = k_ref[:, hs] * beta
            rhs = jnp.concatenate([v_ref[:, hs] * beta, kb * jnp.exp(gc)], axis=1).astype(BF16)
            uw[h] = _dot(tm[h][:, :chunk].astype(BF16), rhs)
        ws = {}
        for h in heads:
            hs = slice(h * DK, (h + 1) * DK)
            gc = gcum[:, N_HEADS + h:N_HEADS + h + 1]
            lhs = jnp.concatenate([uw[h][:, DV:], q_ref[:, hs] * jnp.exp(gc)], axis=0).astype(BF16)
            ws[h] = _dot(lhs, s_ref[h].astype(BF16))
        for h in heads:
            hs = slice(h * DK, (h + 1) * DK)
            gc = gcum[:, N_HEADS + h:N_HEADS + h + 1]
            g_last = gcum[chunk - 1:chunk, N_HEADS + h:N_HEADS + h + 1]
            v_new = (uw[h][:, :DV] - ws[h][:chunk]).astype(BF16)
            o = ws[h][chunk:] + _dot(qk[h].astype(BF16), v_new)
            k_dec = k_ref[:, hs] * jnp.exp(g_last - gc)
            s_ref[h] = s_ref[h] * jnp.exp(g_last) + _dot_tn(k_dec.astype(BF16), v_new)
            o = o * lax.rsqrt(jnp.mean(o * o, axis=-1, keepdims=True) + EPS) * on
            o_ref[:, hs] = (o * _silu(zg_ref[:, hs])).astype(o_ref.dtype)

    @pl.when(c == nc - 1)
    def _():
        sout_ref[...] = s_ref[...]


def _delta(qkv, zg, zg_blk, gate, a_log, dt_bias, o_norm, s0, n_seq, chunk, n_valid):
    rows = qkv.shape[0]
    nc = rows // (n_seq * chunk)
    hw = N_HEADS * DK

    def rmap(blk):
        return lambda s, c: (s * nc + c, blk)

    alog = jnp.zeros((1, 128), F32).at[0, N_HEADS:2 * N_HEADS].set(a_log)
    dtb = jnp.zeros((1, 128), F32).at[0, N_HEADS:2 * N_HEADS].set(dt_bias)
    fix = lambda s, c: (0, 0)
    return pl.pallas_call(
        functools.partial(_delta_body, chunk=chunk, n_valid=n_valid, group=DELTA_GROUP),
        grid=(n_seq, nc),
        in_specs=[pl.BlockSpec((chunk, hw), rmap(0)),
                  pl.BlockSpec((chunk, hw), rmap(1)),
                  pl.BlockSpec((chunk, hw), rmap(2)),
                  pl.BlockSpec((chunk, hw), rmap(zg_blk)),
                  pl.BlockSpec((chunk, 128), rmap(0)),
                  pl.BlockSpec((1, 128), fix),
                  pl.BlockSpec((1, 128), fix),
                  pl.BlockSpec((1, DV), fix),
                  pl.BlockSpec((None, N_HEADS, DK, DV), lambda s, c: (s, 0, 0, 0))],
        out_specs=[pl.BlockSpec((chunk, hw), rmap(0)),
                   pl.BlockSpec((None, N_HEADS, DK, DV), lambda s, c: (s, 0, 0, 0))],
        out_shape=[jax.ShapeDtypeStruct((rows, hw), BF16),
                   jax.ShapeDtypeStruct((n_seq, N_HEADS, DK, DV), F32)],
        scratch_shapes=[pltpu.VMEM((N_HEADS, DK, DV), F32)],
        compiler_params=_params(("parallel", "arbitrary"), 0),
        name="gated_delta",
    )(qkv, qkv, qkv, zg, gate, alog, dtb, o_norm.reshape(1, DV), s0)


def _odd_mixer(x, dims, buf_c, buf_d, s0_s, w_in, conv_c_w, conv_d_w, a_log, dt_bias, o_norm,
               w_out, layer_idx, mix_norm):
    n_p, b_p, l_p, b_s, t_s = dims
    d = x.shape[1]
    d_c = conv_c_w.shape[1]
    d_qkv = conv_d_w.shape[1]
    n_in = w_in.shape[2]
    o_qkv = 3 * d_c
    o_gate = o_qkv + d_qkv
    o_tail = o_gate + N_HEADS * DV
    n_tail = n_in - o_tail
    assert o_tail % 128 == 0 and n_tail == 2 * N_HEADS and o_gate % (N_HEADS * DV) == 0

    h = _rms(x, mix_norm, BF16)
    z = _mmw([(h, 0, d)], [[(jnp.swapaxes(w_in, 1, 2), layer_idx, 0)]], [], _epi_plain, F32, 1088, 512,
             "odd_in", w_t=True)

    y_c_p, tail_c_p = _conv_c_prompt(z, conv_c_w, b_p, l_p, d_c)
    qkv_p, tail_d_p = _conv_d_prompt(z, conv_d_w, b_p, l_p, o_qkv, d_qkv)
    chunk_p = math.gcd(l_p, DN_CHUNK)
    gate_p = jnp.pad(z[:n_p, o_tail:], ((0, 0), (0, 128 - n_tail)))
    s0_p = jnp.zeros((b_p, N_HEADS, DK, DV), F32)
    o_p, st_p = _delta(qkv_p, z, o_gate // (N_HEADS * DV), gate_p, a_log, dt_bias, o_norm, s0_p,
                       b_p, chunk_p, chunk_p)

    zt = jnp.transpose(z[n_p:].reshape(b_s, t_s, n_in), (1, 0, 2))
    y_c_t, nbuf_c_t = _conv_c_sample(zt, jnp.transpose(buf_c, (1, 0, 2)), conv_c_w, d_c)
    qkv_t, nbuf_d_t = _conv_d_sample(zt, jnp.transpose(buf_d, (1, 0, 2)), conv_d_w, o_qkv, d_qkv)
    pad_t = SAMPLE_CHUNK - t_s
    assert pad_t >= 0

    def to_seq(a_t):
        a = jnp.transpose(a_t, (1, 0, 2))
        a = jnp.pad(a, ((0, 0), (0, pad_t), (0, 0)))
        return a.reshape(b_s * SAMPLE_CHUNK, a.shape[2])

    qkv_s = to_seq(qkv_t)
    zs = z[n_p:].reshape(b_s, t_s, n_in)
    zg_s = jnp.pad(zs[:, :, o_gate:o_tail], ((0, 0), (0, pad_t), (0, 0))).reshape(b_s * SAMPLE_CHUNK, -1)
    gate_s = jnp.pad(zs[:, :, o_tail:], ((0, 0), (0, pad_t), (0, 128 - n_tail))).reshape(b_s * SAMPLE_CHUNK, 128)
    o_s8, st_s = _delta(qkv_s, zg_s, 0, gate_s, a_log, dt_bias, o_norm, s0_s, b_s, SAMPLE_CHUNK, t_s)
    o_s = o_s8.reshape(b_s, SAMPLE_CHUNK, -1)[:, :t_s].reshape(b_s * t_s, -1)
    y_c_s = jnp.transpose(y_c_t, (1, 0, 2)).reshape(b_s * t_s, d_c)

    y_c = jnp.concatenate([y_c_p, y_c_s], axis=0)
    o_all = jnp.concatenate([o_p, o_s], axis=0)
    kw = y_c.shape[1]
    x = _mm([y_c, o_all], [[w_out[layer_idx, :kw].astype(BF16), w_out[layer_idx, kw:].astype(BF16)]],
            [(x, "ij")], functools.partial(_epi_resid, scale=1.0), F32, 1088, 512, "odd_out")
    nw_c = conv_c_w.shape[0] - 1
    nw_d = conv_d_w.shape[0] - 1
    outs = (tail_c_p[:, 8 - nw_c:], jnp.transpose(nbuf_c_t, (1, 0, 2)),
            tail_d_p[:, 8 - nw_d:], jnp.transpose(nbuf_d_t, (1, 0, 2)), st_p, st_s)
    return x, outs


def _rope_tables(pos):
    half = ROPE // 2
    inv = ROPE_THETA ** (-jnp.arange(half, dtype=F32) / half)
    ang = pos[:, None] * inv[None, :]
    cos = jnp.cos(ang)
    sin = jnp.sin(ang)
    n = pos.shape[0]
    cos_f = jnp.concatenate([cos, cos], axis=1)
    sin_f = jnp.concatenate([-sin, sin], axis=1)
    z = jnp.zeros((n, 128 - ROPE), F32)
    cos_k = jnp.concatenate([cos_f, z], axis=1)
    sin_k = jnp.concatenate([sin_f, z], axis=1)
    cos_q = jnp.concatenate([jnp.ones((n, NOPE), F32), cos_f, z], axis=1)
    sin_q = jnp.concatenate([jnp.zeros((n, NOPE), F32), sin_f, z], axis=1)
    return cos_k, sin_k, cos_q, sin_q


def kernel(x_prompt, x_sample, cache_mla_latent, cache_mla_krope, state_conv_c, state_conv_d, state_delta, page_table, ffn1_norm, ffn1_w_gate, ffn1_w_up, ffn1_w_down, mix_norm, ffn2_norm, ffn2_w_gate, ffn2_w_up, ffn2_w_down, even_w_in, gmlp_v_norm, gmlp_ws, gmlp_bs, mla_q_norm, mla_w_uq, mla_kv_norm, mla_w_uk, mla_w_uv, even_w_out, odd_w_in, conv_c_w, conv_d_w, delta_a_log, delta_dt_bias, delta_o_norm, odd_w_out, final_norm):
    b_p, l_p, d = x_prompt.shape
    b_s, t_s, _ = x_sample.shape
    n_p = b_p * l_p
    n_s = b_s * t_s
    depth = ffn1_norm.shape[0]
    dims = (n_p, b_p, l_p, b_s, t_s)
    past_len = page_table.shape[1] * PAGE
    pos = jnp.concatenate([jnp.tile(jnp.arange(l_p, dtype=F32), b_p),
                           jnp.tile(jnp.arange(t_s, dtype=F32) + past_len, b_s)])
    tabs = _rope_tables(pos)

    x = jnp.concatenate([x_prompt.reshape(n_p, d), x_sample.reshape(n_s, d)], axis=0)
    lat, kr, vrow = [], [], []
    cc_p, cc_s, cd_p, cd_s, sd_p, sd_s = [], [], [], [], [], []
    for layer in range(depth):
        x = _ffn_half(x, ffn1_norm[layer], ffn1_w_gate, ffn1_w_up, ffn1_w_down, layer)
        if layer % 2 == 0:
            e = layer // 2
            x, v_rows, c, kpe = _even_mixer(
                x, dims, tabs, cache_mla_latent, cache_mla_krope, page_table, e, even_w_in[e],
                gmlp_v_norm[e], gmlp_ws[e], gmlp_bs[e], mla_q_norm[e], mla_w_uq[e], mla_kv_norm[e],
                mla_w_uk[e], mla_w_uv[e], even_w_out, mix_norm[layer])
            lat.append(c)
            kr.append(kpe)
            vrow.append(v_rows[n_p:])
        else:
            o = layer // 2
            x, outs = _odd_mixer(
                x, dims, state_conv_c[o], state_conv_d[o], state_delta[o], odd_w_in, conv_c_w[o],
                conv_d_w[o], delta_a_log[o], delta_dt_bias[o], delta_o_norm[o], odd_w_out, o,
                mix_norm[layer])
            cc_p.append(outs[0]); cc_s.append(outs[1]); cd_p.append(outs[2]); cd_s.append(outs[3])
            sd_p.append(outs[4]); sd_s.append(outs[5])
        x = _ffn_half(x, ffn2_norm[layer], ffn2_w_gate, ffn2_w_up, ffn2_w_down, layer)

    y_p = _rms(x, final_norm, F32, 0, n_p)
    y_s = _rms(x, final_norm, F32, n_p, n_s)
    lat = jnp.stack(lat)
    kr = jnp.stack(kr)
    n_e = lat.shape[0]
    return (y_p.reshape(b_p, l_p, d), y_s.reshape(b_s, t_s, d),
            lat[:, :n_p].reshape(n_e, b_p, l_p, -1), kr[:, :n_p].reshape(n_e, b_p, l_p, -1),
            lat[:, n_p:].reshape(n_e, b_s, t_s, -1), kr[:, n_p:].reshape(n_e, b_s, t_s, -1),
            jnp.stack(vrow).reshape(n_e, b_s, t_s, -1),
            jnp.stack(cc_p), jnp.stack(cc_s), jnp.stack(cd_p), jnp.stack(cd_s),
            jnp.stack(sd_p), jnp.stack(sd_s))
```

```python
import functools
import math

import jax
import jax.numpy as jnp
from jax import lax
from jax.experimental import pallas as pl
from jax.experimental.pallas import tpu as pltpu

F32 = jnp.float32
BF16 = jnp.bfloat16
EPS = 1e-6

PAGE = 128
GROUP = 128
N_HEADS = 16
NOPE = 128
ROPE = 64
HEAD_PAD = 256
KV_LORA = 512
V_DIM = 128
DK = 128
DV = 128
ROPE_THETA = 10000.0
ATT_SCALE = (NOPE + ROPE) ** -0.5
DN_CHUNK = 64
SAMPLE_CHUNK = 8
PAGES_PER_STEP = 32
PAGES_PER_GROUP = 16
PREP_BLK = 256
DELTA_GROUP = 16

VMEM_CAP_BYTES = 60 * 1024 * 1024


def _pick(n, target, mult=8):
    for d in range(min(n, target), 0, -1):
        if n % d == 0 and d % mult == 0:
            return d
    return n


def _params(sem, vmem_bytes):
    return pltpu.CompilerParams(
        dimension_semantics=sem,
        vmem_limit_bytes=int(min(VMEM_CAP_BYTES, max(vmem_bytes, 32 * 1024 * 1024))))


def _sigmoid(x):
    return 1.0 / (1.0 + jnp.exp(-x))


def _silu(x):
    return x * _sigmoid(x)


def _dot(a, b):
    return jnp.dot(a, b, preferred_element_type=F32)


def _dot_nt(a, b):
    return lax.dot_general(a, b, (((1,), (1,)), ((), ())), preferred_element_type=F32)


def _dot_tn(a, b):
    return lax.dot_general(a, b, (((0,), (0,)), ((), ())), preferred_element_type=F32)


def _split3(x):
    hi = x.astype(BF16)
    r = x - hi.astype(F32)
    mid = r.astype(BF16)
    lo = (r - mid.astype(F32)).astype(BF16)
    return hi, mid, lo


def _dot_hp(a, b):
    a_hi, a_mid, _ = _split3(a)
    b_hi, b_mid, _ = _split3(b)
    return _dot(a_hi, b_hi) + (_dot(a_hi, b_mid) + _dot(a_mid, b_hi))


def _rms_body(x_ref, g_ref, o_ref):
    x = x_ref[...]
    y = x * lax.rsqrt(jnp.mean(x * x, axis=-1, keepdims=True) + EPS)
    o_ref[...] = (y * g_ref[...]).astype(o_ref.dtype)


def _rms(x, g, out_dtype, row0=0, rows=None):
    d = x.shape[1]
    m = x.shape[0] if rows is None else rows
    tm = _pick(math.gcd(m, row0) if row0 else m, 256, 16)
    blk0 = row0 // tm
    return pl.pallas_call(
        _rms_body,
        grid=(m // tm,),
        in_specs=[pl.BlockSpec((tm, d), lambda i: (blk0 + i, 0)),
                  pl.BlockSpec((1, d), lambda i: (0, 0))],
        out_specs=pl.BlockSpec((tm, d), lambda i: (i, 0)),
        out_shape=jax.ShapeDtypeStruct((m, d), out_dtype),
        compiler_params=_params(("parallel",), 6 * tm * d * 4),
        name="rmsnorm",
    )(x, g.reshape(1, d))


def _mm_body(*refs, n_w, n_lhs, n_x, epi):
    lhs = [refs[l][...] for l in range(n_lhs)]
    accs = []
    for k in range(n_w):
        acc = _dot(lhs[0], refs[n_lhs + k * n_lhs][...])
        for l in range(1, n_lhs):
            acc = acc + _dot(lhs[l], refs[n_lhs + k * n_lhs + l][...])
        accs.append(acc)
    base = n_lhs + n_w * n_lhs
    xs = [refs[base + k][...] for k in range(n_x)]
    o_ref = refs[base + n_x]
    o_ref[...] = epi(accs, xs).astype(o_ref.dtype)


def _epi_plain(accs, xs):
    return accs[0]


def _epi_swiglu(accs, xs):
    return _silu(accs[0]) * accs[1]


def _epi_resid(accs, xs, scale):
    return xs[0] + scale * accs[0]


def _epi_rope(accs, xs):
    return accs[0] * xs[0] + accs[1] * xs[1]


def _mm(lhs, ws, extras, epi, out_dtype, tm_target, tn, name):
    if not isinstance(lhs, (list, tuple)):
        lhs, ws = [lhs], [[w] for w in ws]
    m = lhs[0].shape[0]
    n = ws[0][0].shape[1]
    tm = _pick(m, tm_target, 16)
    assert n % tn == 0, (n, tn)
    ktot = sum(a.shape[1] for a in lhs)
    in_specs = [pl.BlockSpec((tm, a.shape[1]), lambda i, j: (i, 0)) for a in lhs]
    for wl in ws:
        in_specs += [pl.BlockSpec((w.shape[0], tn), lambda i, j: (0, j)) for w in wl]
    for _, kind in extras:
        if kind == "ij":
            in_specs.append(pl.BlockSpec((tm, tn), lambda i, j: (i, j)))
        else:
            in_specs.append(pl.BlockSpec((tm, tn), lambda i, j: (i, 0)))
    vmem = 2 * (tm * ktot * 2 + len(ws) * ktot * tn * 2 + (len(extras) + 1) * tm * tn * 4)
    vmem += (len(ws) + 1) * tm * tn * 4 + (4 << 20)
    return pl.pallas_call(
        functools.partial(_mm_body, n_w=len(ws), n_lhs=len(lhs), n_x=len(extras), epi=epi),
        grid=(m // tm, n // tn),
        in_specs=in_specs,
        out_specs=pl.BlockSpec((tm, tn), lambda i, j: (i, j)),
        out_shape=jax.ShapeDtypeStruct((m, n), out_dtype),
        compiler_params=_params(("parallel", "arbitrary"), vmem),
        name=name,
    )(*lhs, *[w for wl in ws for w in wl], *[a for a, _ in extras])


def _headmm_body(l_ref, w_ref, o_ref):
    o_ref[...] = _dot(l_ref[...], w_ref[...]).astype(o_ref.dtype)


def _headmm(lhs, w, out_dtype, name):
    m = lhs.shape[0]
    nh, kd, nd = w.shape
    return pl.pallas_call(
        _headmm_body,
        grid=(nh,),
        in_specs=[pl.BlockSpec((m, kd), lambda h: (0, h)),
                  pl.BlockSpec((None, kd, nd), lambda h: (h, 0, 0))],
        out_specs=pl.BlockSpec((m, nd), lambda h: (0, h)),
        out_shape=jax.ShapeDtypeStruct((m, nh * nd), out_dtype),
        compiler_params=_params(("parallel",), 0),
        name=name,
    )(lhs, w)


def _mmw_body(*refs, n_acc, n_lhs, n_x, epi, w_t):
    n_w = n_acc * n_lhs
    lhs_refs = refs[:n_lhs]
    w_refs = refs[n_lhs:n_lhs + n_w]
    x_refs = refs[n_lhs + n_w:n_lhs + n_w + n_x]
    o_ref = refs[n_lhs + n_w + n_x]
    wb_refs = refs[n_lhs + n_w + n_x + 1:]

    @pl.when(pl.program_id(1) == 0)
    def _():
        for w_ref, wb_ref in zip(w_refs, wb_refs):
            wb_ref[...] = w_ref[...].astype(BF16)

    dot = _dot_nt if w_t else _dot
    lhs = [r[...] for r in lhs_refs]
    accs = []
    for a in range(n_acc):
        acc = dot(lhs[0], wb_refs[a * n_lhs][...])
        for l in range(1, n_lhs):
            acc = acc + dot(lhs[l], wb_refs[a * n_lhs + l][...])
        accs.append(acc)
    o_ref[...] = epi(accs, [r[...] for r in x_refs]).astype(o_ref.dtype)


def _mmw(lhs_list, w_list, extras, epi, out_dtype, tm_target, tn, name, w_t=False):
    m = lhs_list[0][0].shape[0]
    n = w_list[0][0][0].shape[1 if w_t else 2]
    tm = _pick(m, tm_target, 16)
    n_lhs = len(lhs_list)
    in_specs, args = [], []
    vmem = 0
    for arr, cb, k in lhs_list:
        in_specs.append(pl.BlockSpec((tm, k), lambda j, i, cb=cb: (i, cb)))
        args.append(arr)
        vmem += 2 * tm * k * arr.dtype.itemsize
    scratch = []
    for ws in w_list:
        assert len(ws) == n_lhs
        for (arr, layer, rb), (_, _, k) in zip(ws, lhs_list):
            if w_t:
                in_specs.append(pl.BlockSpec((None, tn, k), lambda j, i, layer=layer, rb=rb: (layer, j, rb)))
                scratch.append(pltpu.VMEM((tn, k), BF16))
            else:
                in_specs.append(pl.BlockSpec((None, k, tn), lambda j, i, layer=layer, rb=rb: (layer, rb, j)))
                scratch.append(pltpu.VMEM((k, tn), BF16))
            args.append(arr)
            vmem += 2 * k * tn * 4 + k * tn * 2
    for arr, kind in extras:
        if kind == "ij":
            in_specs.append(pl.BlockSpec((tm, tn), lambda j, i: (i, j)))
        else:
            in_specs.append(pl.BlockSpec((tm, tn), lambda j, i: (i, 0)))
        args.append(arr)
    vmem += (2 * (len(extras) + 1) + len(w_list) + 1) * tm * tn * 4 + (4 << 20)
    return pl.pallas_call(
        functools.partial(_mmw_body, n_acc=len(w_list), n_lhs=n_lhs, n_x=len(extras), epi=epi, w_t=w_t),
        grid=(pl.cdiv(n, tn), m // tm),
        in_specs=in_specs,
        out_specs=pl.BlockSpec((tm, tn), lambda j, i: (i, j)),
        out_shape=jax.ShapeDtypeStruct((m, n), out_dtype),
        scratch_shapes=scratch,
        compiler_params=_params(("parallel", "arbitrary"), vmem),
        name=name,
    )(*args)


def _ffn_half(x, g, wg, wu, wd, layer):
    d, f = wg.shape[1:]
    assert f % 256 == 0
    kh = f // 2
    assert kh % 128 == 0
    h = _rms(x, g, BF16)
    a = _mmw([(h, 0, d)], [[(wg, layer, 0)], [(wu, layer, 0)]], [], _epi_swiglu, BF16, 1088, 256,
             "ffn_gate_up")
    half = functools.partial(_epi_resid, scale=0.5)
    x = _mmw([(a, 0, kh)], [[(wd, layer, 0)]], [(x, "ij")], half, F32, 544, 512, "ffn_down_lo")
    return _mmw([(a, 1, kh)], [[(wd, layer, 1)]], [(x, "ij")], half, F32, 544, 512, "ffn_down_hi")


def _gmlp_body(z_ref, w_ref, b_ref, vn_ref, a_ref, v_ref, *, n_prompt_tiles, t_sample, d_a):
    i = pl.program_id(0)
    z = z_ref[...]
    c0 = math.sqrt(2.0 / math.pi)
    a = 0.5 * z * (1.0 + jnp.tanh(c0 * (z + 0.044715 * (z * z * z))))
    u = a[:, :d_a]
    v = a[:, d_a:]
    v = v * lax.rsqrt(jnp.mean(v * v, axis=-1, keepdims=True) + EPS) * vn_ref[...]
    v_ref[...] = v
    row = lax.broadcasted_iota(jnp.int32, (GROUP, GROUP), 0)
    col = lax.broadcasted_iota(jnp.int32, (GROUP, GROUP), 1)
    same_seq = (row // t_sample) == (col // t_sample)
    mask = (col <= row) & (same_seq | (i < n_prompt_tiles))
    for g in range(d_a // GROUP):
        sl = slice(g * GROUP, (g + 1) * GROUP)
        w = jnp.where(mask, w_ref[g], 0.0).astype(BF16)
        s = _dot(w, v[:, sl].astype(BF16)) + b_ref[:, sl]
        a_ref[:, sl] = (u[:, sl] * s).astype(a_ref.dtype)


def _gmlp(z_a, ws, bs, v_norm, n_prompt, t_sample, d_a):
    m = z_a.shape[0]
    ng = d_a // GROUP
    assert n_prompt % GROUP == 0 and (m - n_prompt) % GROUP == 0 and GROUP % t_sample == 0
    npt = n_prompt // GROUP
    rep = GROUP // t_sample
    w_all = jnp.stack([ws, jnp.tile(ws[:, :t_sample, :t_sample], (1, rep, rep))])
    b_p = jnp.repeat(bs.T, GROUP, axis=1)
    b_s = jnp.repeat(jnp.tile(bs[:, :t_sample].T, (rep, 1)), GROUP, axis=1)
    b_all = jnp.stack([b_p, b_s])

    def sel(i):
        return jnp.where(i < npt, 0, 1)

    return pl.pallas_call(
        functools.partial(_gmlp_body, n_prompt_tiles=npt, t_sample=t_sample, d_a=d_a),
        grid=(m // GROUP,),
        in_specs=[pl.BlockSpec((GROUP, 2 * d_a), lambda i: (i, 0)),
                  pl.BlockSpec((None, ng, GROUP, GROUP), lambda i: (sel(i), 0, 0, 0)),
                  pl.BlockSpec((None, GROUP, d_a), lambda i: (sel(i), 0, 0)),
                  pl.BlockSpec((1, d_a), lambda i: (0, 0))],
        out_specs=[pl.BlockSpec((GROUP, d_a), lambda i: (i, 0)),
                   pl.BlockSpec((GROUP, d_a), lambda i: (i, 0))],
        out_shape=[jax.ShapeDtypeStruct((m, d_a), BF16),
                   jax.ShapeDtypeStruct((m, d_a), F32)],
        compiler_params=_params(("parallel",), 0),
        name="gmlp",
    )(z_a, w_all, b_all, v_norm.reshape(1, d_a))


def _mla_prep_body(*refs, q_lora, n_blk):
    z_refs = refs[:n_blk]
    qg_ref, kg_ref, cos_ref, sin_ref, qn_ref, c_ref, kpe_ref, ck_ref = refs[n_blk:]
    nq = q_lora // PREP_BLK
    nkv = KV_LORA // PREP_BLK
    zq = jnp.concatenate([z_refs[b][...] for b in range(nq)], axis=1)
    qn_ref[...] = (zq * lax.rsqrt(jnp.mean(zq * zq, axis=-1, keepdims=True) + EPS)
                   * qg_ref[...]).astype(qn_ref.dtype)
    zkv = jnp.concatenate([z_refs[nq + b][...] for b in range(nkv)], axis=1)
    c = zkv * lax.rsqrt(jnp.mean(zkv * zkv, axis=-1, keepdims=True) + EPS) * kg_ref[...]
    c_ref[...] = c
    zr = z_refs[nq + nkv][:, :128]
    lane = lax.broadcasted_iota(jnp.int32, zr.shape, 1)
    zr = jnp.where(lane < ROPE, zr, 0.0)
    src = lax.broadcasted_iota(jnp.int32, (128, 128), 0)
    dst = lax.broadcasted_iota(jnp.int32, (128, 128), 1)
    half = ROPE // 2
    partner = jnp.where(dst < half, dst + half, dst - half)
    perm = ((src == partner) & (dst < ROPE)).astype(BF16)
    r_hi, r_mid, r_lo = _split3(zr)
    zrp = _dot(r_hi, perm) + (_dot(r_mid, perm) + _dot(r_lo, perm))
    kpe = zr * cos_ref[...] + zrp * sin_ref[...]
    kpe_ref[...] = kpe[:, :ROPE]
    ck_ref[:, :KV_LORA] = c.astype(ck_ref.dtype)
    one = (lax.broadcasted_iota(jnp.int32, kpe.shape, 1) == ROPE).astype(F32)
    ck_ref[:, KV_LORA:] = (kpe + one).astype(ck_ref.dtype)


def _mla_prep(z, col0, q_norm, kv_norm, cos128, sin128, q_lora):
    m = z.shape[0]
    tm = _pick(m, 544, 16)
    assert col0 % PREP_BLK == 0 and q_lora % PREP_BLK == 0 and KV_LORA % PREP_BLK == 0
    blk0 = col0 // PREP_BLK
    n_blk = (q_lora + KV_LORA) // PREP_BLK + 1
    row = lambda i: (i, 0)
    fix = lambda i: (0, 0)
    return pl.pallas_call(
        functools.partial(_mla_prep_body, q_lora=q_lora, n_blk=n_blk),
        grid=(m // tm,),
        in_specs=[pl.BlockSpec((tm, PREP_BLK), lambda i, b=b: (i, blk0 + b)) for b in range(n_blk)] + [
                  pl.BlockSpec((1, q_lora), fix),
                  pl.BlockSpec((1, KV_LORA), fix),
                  pl.BlockSpec((tm, 128), row),
                  pl.BlockSpec((tm, 128), row)],
        out_specs=[pl.BlockSpec((tm, q_lora), row),
                   pl.BlockSpec((tm, KV_LORA), row),
                   pl.BlockSpec((tm, ROPE), row),
                   pl.BlockSpec((tm, KV_LORA + 128), row)],
        out_shape=[jax.ShapeDtypeStruct((m, q_lora), BF16),
                   jax.ShapeDtypeStruct((m, KV_LORA), F32),
                   jax.ShapeDtypeStruct((m, ROPE), F32),
                   jax.ShapeDtypeStruct((m, KV_LORA + 128), BF16)],
        compiler_params=_params(("parallel",), 0),
        name="mla_prep",
    )(*([z] * n_blk), q_norm.reshape(1, q_lora), kv_norm.reshape(1, KV_LORA), cos128, sin128)


def _attn_prompt_body(q_ref, k_ref, v_ref, o_ref, m_ref, l_ref, acc_ref, *, tq, tk):
    qi = pl.program_id(1)
    ki = pl.program_id(2)
    nk = pl.num_programs(2)

    @pl.when(ki == 0)
    def _():
        m_ref[...] = jnp.full(m_ref.shape, -jnp.inf, F32)
        l_ref[...] = jnp.zeros(l_ref.shape, F32)
        acc_ref[...] = jnp.zeros(acc_ref.shape, F32)

    def scores(h):
        qs = slice(h * HEAD_PAD, (h + 1) * HEAD_PAD)
        return _dot_nt(q_ref[:, qs], k_ref[:, qs]) * ATT_SCALE

    def process(diagonal):
        if diagonal:
            mask = (lax.broadcasted_iota(jnp.int32, (tq, tk), 1)
                    <= lax.broadcasted_iota(jnp.int32, (tq, tk), 0))
        s_next = scores(0)
        for h in range(N_HEADS):
            vs = slice(h * V_DIM, (h + 1) * V_DIM)
            s = s_next
            if h + 1 < N_HEADS:
                s_next = scores(h + 1)
            if diagonal:
                s = jnp.where(mask, s, -jnp.inf)
            m_old = m_ref[h]
            m_new = jnp.maximum(m_old, jnp.max(s, axis=-1, keepdims=True))
            corr = jnp.exp(m_old - m_new)
            p = jnp.concatenate([jnp.exp(s[:, t * 128:(t + 1) * 128] - m_new) for t in range(tk // 128)],
                                axis=1).astype(BF16)
            pv = _dot(p, v_ref[:, 2 * h * V_DIM:2 * (h + 1) * V_DIM])
            l_ref[h] = l_ref[h] * corr + pv[:, V_DIM:]
            acc_ref[:, vs] = acc_ref[:, vs] * corr + pv[:, :V_DIM]
            m_ref[h] = m_new

    @pl.when(ki < qi)
    def _():
        process(False)

    @pl.when(ki == qi)
    def _():
        process(True)

    @pl.when(ki == nk - 1)
    def _():
        for h in range(N_HEADS):
            vs = slice(h * V_DIM, (h + 1) * V_DIM)
            o_ref[:, vs] = (acc_ref[:, vs] / l_ref[h]).astype(o_ref.dtype)


def _attn_prompt(q_cat, kv, n_seq, seq_len):
    tq = _pick(seq_len, 256, 128)
    tk = tq
    nq = seq_len // tq
    qw = N_HEADS * HEAD_PAD
    vw = N_HEADS * V_DIM
    assert V_DIM == 128 and tk % 128 == 0 and qw == 2 * vw
    v_blk = 1

    def q_map(b, qi, ki):
        return (b * nq + qi, 0)

    def k_map(b, qi, ki):
        return (b * nq + jnp.minimum(ki, qi), 0)

    def v_map(b, qi, ki):
        return (b * nq + jnp.minimum(ki, qi), v_blk)

    return pl.pallas_call(
        functools.partial(_attn_prompt_body, tq=tq, tk=tk),
        grid=(n_seq, nq, nq),
        in_specs=[pl.BlockSpec((tq, qw), q_map),
                  pl.BlockSpec((tk, qw), k_map),
                  pl.BlockSpec((tk, 2 * vw), v_map)],
        out_specs=pl.BlockSpec((tq, vw), q_map),
        out_shape=jax.ShapeDtypeStruct((n_seq * seq_len, vw), BF16),
        scratch_shapes=[pltpu.VMEM((N_HEADS, tq, 128), F32),
                        pltpu.VMEM((N_HEADS, tq, 128), F32),
                        pltpu.VMEM((tq, vw), F32)],
        compiler_params=_params(("parallel", "parallel", "arbitrary"), 0),
        name="attn_prompt",
    )(q_cat, kv, kv)


def _decode_body(pt_ref, qa_ref, qp_ref, sc_ref, sk_ref, lat_hbm, kr_hbm, o_ref,
                 m_ref, l_ref, acc_ref, lat_buf, kr_buf, sem, *, pps, gsz, layer):
    b = pl.program_id(0)
    c = pl.program_id(1)
    nb = pl.num_programs(0)
    nc = pl.num_programs(1)
    step = b * nc + c
    slot = step % 2
    qa = qa_ref[...]
    qp = qp_ref[...]
    nrow = qa.shape[0]

    def page_copies(page, slot_, k):
        return (pltpu.make_async_copy(lat_hbm.at[layer, page], lat_buf.at[slot_, k], sem.at[0, slot_]),
                pltpu.make_async_copy(kr_hbm.at[layer, page], kr_buf.at[slot_, k], sem.at[1, slot_]))

    def start_step(bb, cc, slot_):
        for k in range(pps):
            for cp in page_copies(pt_ref[bb, cc * pps + k], slot_, k):
                cp.start()

    @pl.when(step == 0)
    def _():
        start_step(0, 0, 0)

    @pl.when(step + 1 < nb * nc)
    def _():
        wrap = c + 1 == nc
        start_step(jnp.where(wrap, b + 1, b), jnp.where(wrap, 0, c + 1), 1 - slot)

    for k in range(pps):
        for cp in page_copies(0, slot, k):
            cp.wait()
    lat_refs = [lat_buf.at[slot, k] for k in range(pps)]
    kr_refs = [kr_buf.at[slot, k] for k in range(pps)]

    def scores(kc, kpt):
        return (_dot_nt(qa, kc) + _dot(qp, kpt)) * ATT_SCALE

    def update(state, s, kc):
        m_old, l_old, acc = state
        m_new = jnp.maximum(m_old, jnp.max(s, axis=-1, keepdims=True))
        corr = jnp.exp(m_old - m_new)
        p = jnp.exp(s - m_new)
        l_new = l_old * corr + jnp.sum(p, axis=-1, keepdims=True)
        return m_new, l_new, acc * corr + _dot(p.astype(BF16), kc)

    def store(state):
        m_ref[...] = jnp.broadcast_to(state[0], m_ref.shape)
        l_ref[...] = jnp.broadcast_to(state[1], l_ref.shape)
        acc_ref[...] = state[2]

    @pl.when(c == 0)
    def _():
        key = lax.broadcasted_iota(jnp.int32, (nrow, PAGE), 1)
        tok = lax.broadcasted_iota(jnp.int32, (nrow, PAGE), 0) // N_HEADS
        kc = sc_ref[...].astype(BF16)
        s = jnp.where(key <= tok, scores(kc, sk_ref[...].astype(BF16)), -jnp.inf)
        init = (jnp.full((nrow, 1), -jnp.inf, F32), jnp.zeros((nrow, 1), F32),
                jnp.zeros((nrow, KV_LORA), F32))
        store(update(init, s, kc))

    def load_group(g):
        ks = range(g * gsz, (g + 1) * gsz)
        kc = jnp.concatenate([lat_refs[k][...].astype(BF16) for k in ks], axis=0)
        kpt = jnp.concatenate([kr_refs[k][...].astype(BF16) for k in ks], axis=1)
        return kc, kpt

    state = (m_ref[:, :1], l_ref[:, :1], acc_ref[...])
    kc, kpt = load_group(0)
    s = scores(kc, kpt)
    for g in range(pps // gsz):
        if g + 1 < pps // gsz:
            kc_next, kpt_next = load_group(g + 1)
            s_next = scores(kc_next, kpt_next)
        state = update(state, s, kc)
        if g + 1 < pps // gsz:
            kc, s = kc_next, s_next
    store(state)

    @pl.when(c == nc - 1)
    def _():
        o_ref[...] = (acc_ref[...] / l_ref[:, :1]).astype(o_ref.dtype)


def _decode(q_abs, q_pe, self_c, self_kt, cache_lat, cache_krt, page_table, layer):
    bs, nrow, _ = q_abs.shape
    n_pages = page_table.shape[1]
    pps = _pick(n_pages, PAGES_PER_STEP, 1)
    gsz = _pick(pps, PAGES_PER_GROUP, 1)
    nc = n_pages // pps

    def fix(b, c, pt):
        return (b, 0, 0)

    in_specs = [pl.BlockSpec((None, nrow, KV_LORA), fix),
                pl.BlockSpec((None, nrow, ROPE), fix),
                pl.BlockSpec((None, PAGE, KV_LORA), fix),
                pl.BlockSpec((None, ROPE, PAGE), fix),
                pl.BlockSpec(memory_space=pl.ANY),
                pl.BlockSpec(memory_space=pl.ANY)]
    grid_spec = pltpu.PrefetchScalarGridSpec(
        num_scalar_prefetch=1,
        grid=(bs, nc),
        in_specs=in_specs,
        out_specs=pl.BlockSpec((None, nrow, KV_LORA), fix),
        scratch_shapes=[pltpu.VMEM((nrow, 128), F32),
                        pltpu.VMEM((nrow, 128), F32),
                        pltpu.VMEM((nrow, KV_LORA), F32),
                        pltpu.VMEM((2, pps, PAGE, KV_LORA), F32),
                        pltpu.VMEM((2, pps, ROPE, PAGE), F32),
                        pltpu.SemaphoreType.DMA((2, 2))])
    vmem = 2 * pps * (PAGE * KV_LORA + ROPE * PAGE) * 4 + (16 << 20)
    return pl.pallas_call(
        functools.partial(_decode_body, pps=pps, gsz=gsz, layer=layer),
        grid_spec=grid_spec,
        out_shape=jax.ShapeDtypeStruct((bs, nrow, KV_LORA), BF16),
        compiler_params=_params(("arbitrary", "arbitrary"), vmem),
        name="attn_decode",
    )(page_table, q_abs, q_pe, self_c, self_kt, cache_lat, cache_krt)


def _even_mixer(x, dims, tabs, cache_lat, cache_kr, page_table, e, w_in, v_norm, ws, bs,
                q_norm, w_uq, kv_norm, w_uk, w_uv, w_out, mix_norm):
    n_p, b_p, l_p, b_s, t_s = dims
    cos_k, sin_k, cos_q, sin_q = tabs
    d = x.shape[1]
    d_a = v_norm.shape[0]
    q_lora = q_norm.shape[0]
    n_s = x.shape[0] - n_p
    half = ROPE // 2
    perm = jnp.concatenate([jnp.arange(half, ROPE), jnp.arange(0, half)])

    h = _rms(x, mix_norm, BF16)
    o_q = 2 * d_a
    assert w_in.shape[2] == o_q + q_lora + KV_LORA + ROPE
    z = _mmw([(h, 0, d)], [[(jnp.swapaxes(w_in, 1, 2), e, 0)]], [], _epi_plain, F32, 1088, 512,
             "even_in", w_t=True)

    a_out, v_rows = _gmlp(z, ws, bs, v_norm, n_p, t_s, d_a)
    qn, c, kpe, ck = _mla_prep(z, o_q, q_norm, kv_norm, cos_k, sin_k, q_lora)

    zpad = jnp.zeros((q_lora, N_HEADS, HEAD_PAD - NOPE - ROPE), F32)
    w1 = jnp.concatenate([w_uq, zpad], axis=2).reshape(q_lora, N_HEADS * HEAD_PAD).astype(BF16)
    w2 = jnp.concatenate([jnp.zeros((q_lora, N_HEADS, NOPE), F32), w_uq[:, :, NOPE:][:, :, perm], zpad],
                         axis=2).reshape(q_lora, N_HEADS * HEAD_PAD).astype(BF16)
    q_cat = _mm(qn, [w1, w2], [(cos_q, "i0"), (sin_q, "i0")], _epi_rope, BF16, 2176, HEAD_PAD, "mla_q")

    ckw = ck.shape[1]
    eye = jnp.eye(ROPE, dtype=F32)
    wk = jnp.zeros((ckw, N_HEADS, HEAD_PAD), F32)
    wk = wk.at[:KV_LORA, :, :NOPE].set(w_uk)
    wk = wk.at[KV_LORA:KV_LORA + ROPE, :, NOPE:NOPE + ROPE].set(jnp.broadcast_to(eye[:, None, :], (ROPE, N_HEADS, ROPE)))
    wv = jnp.zeros((ckw, N_HEADS, 2 * V_DIM), F32)
    wv = wv.at[:KV_LORA, :, :V_DIM].set(w_uv)
    wv = wv.at[KV_LORA + ROPE, :, V_DIM:].set(1.0)
    w_kv = jnp.concatenate([wk.reshape(ckw, N_HEADS * HEAD_PAD), wv.reshape(ckw, N_HEADS * 2 * V_DIM)],
                           axis=1).astype(BF16)
    kv_p = _mm(ck[:n_p], [w_kv], [], _epi_plain, BF16, 2048, 1024, "mla_kv_up")
    o_p = _attn_prompt(q_cat, kv_p, b_p, l_p)

    qs = q_cat[n_p:].reshape(n_s, N_HEADS, HEAD_PAD)
    q_nope_s = qs[:, :, :NOPE].reshape(n_s, N_HEADS * NOPE)
    q_pe_s = qs[:, :, NOPE:NOPE + ROPE].reshape(b_s, t_s * N_HEADS, ROPE)
    w_ukt = jnp.transpose(w_uk, (1, 2, 0)).astype(BF16)
    q_abs = _headmm(q_nope_s, w_ukt, BF16, "mla_q_absorb").reshape(b_s, t_s * N_HEADS, KV_LORA)
    self_c = jnp.zeros((b_s, PAGE, KV_LORA), F32).at[:, :t_s].set(c[n_p:].reshape(b_s, t_s, KV_LORA))
    self_kt = jnp.zeros((b_s, ROPE, PAGE), F32).at[:, :, :t_s].set(
        jnp.swapaxes(kpe[n_p:].reshape(b_s, t_s, ROPE), 1, 2))
    o_lat = _decode(q_abs, q_pe_s, self_c, self_kt, cache_lat, jnp.swapaxes(cache_kr, 2, 3), page_table, e)
    w_uvh = jnp.transpose(w_uv, (1, 0, 2)).astype(BF16)
    o_s = _headmm(o_lat.reshape(n_s, N_HEADS * KV_LORA), w_uvh, BF16, "mla_o_up")

    o_all = jnp.concatenate([o_p, o_s], axis=0)
    kw = a_out.shape[1]
    x = _mm([a_out, o_all], [[w_out[e, :kw].astype(BF16), w_out[e, kw:].astype(BF16)]], [(x, "ij")],
            functools.partial(_epi_resid, scale=1.0), F32, 1088, 512, "even_out")
    return x, v_rows, c, kpe


def _shift_rows(x, j):
    if j == 0:
        return x
    row = lax.broadcasted_iota(jnp.int32, x.shape, 0)
    return jnp.where(row >= j, pltpu.roll(x, j, 0), 0.0)


def _qkv_factor(y, j, tiles_per_part):
    nrm = lax.rsqrt(jnp.sum(y * y, axis=-1, keepdims=True) + EPS)
    part = j // tiles_per_part
    return jnp.where(part == 0, nrm * DK ** -0.5, jnp.where(part == 1, nrm, 1.0))


def _conv_c_prompt_body(b_ref, c_ref, x_ref, w_ref, y_ref, tail_ref):
    xg = c_ref[...] * x_ref[...]
    w = w_ref[...]
    nw = w.shape[0]
    conv = w[nw - 1:nw] * xg
    for j in range(1, nw):
        conv = conv + w[nw - 1 - j:nw - j] * _shift_rows(xg, j)
    y_ref[...] = (b_ref[...] * conv).astype(y_ref.dtype)
    n = xg.shape[0]
    tail_ref[...] = xg[n - 8:, :]


def _conv_c_prompt(z, w, n_seq, seq_len, d_c):
    tc = 256
    nb = d_c // tc
    return pl.pallas_call(
        _conv_c_prompt_body,
        grid=(n_seq, nb),
        in_specs=[pl.BlockSpec((seq_len, tc), lambda b, j: (b, j)),
                  pl.BlockSpec((seq_len, tc), lambda b, j: (b, nb + j)),
                  pl.BlockSpec((seq_len, tc), lambda b, j: (b, 2 * nb + j)),
                  pl.BlockSpec((w.shape[0], tc), lambda b, j: (0, j))],
        out_specs=[pl.BlockSpec((seq_len, tc), lambda b, j: (b, j)),
                   pl.BlockSpec((None, 8, tc), lambda b, j: (b, 0, j))],
        out_shape=[jax.ShapeDtypeStruct((n_seq * seq_len, d_c), BF16),
                   jax.ShapeDtypeStruct((n_seq, 8, d_c), F32)],
        compiler_params=_params(("parallel", "parallel"), 0),
        name="conv_c_prompt",
    )(z, z, z, w)


def _conv_d_prompt_body(x_ref, w_ref, y_ref, tail_ref, *, tiles_per_part):
    j = pl.program_id(1)
    x = x_ref[...]
    w = w_ref[...]
    nw = w.shape[0]
    conv = w[nw - 1:nw] * x
    for s in range(1, nw):
        conv = conv + w[nw - 1 - s:nw - s] * _shift_rows(x, s)
    y = _silu(conv)
    for hh in range(x.shape[1] // DK):
        seg = y[:, hh * DK:(hh + 1) * DK]
        y_ref[:, hh * DK:(hh + 1) * DK] = seg * _qkv_factor(seg, j, tiles_per_part)
    n = x.shape[0]
    tail_ref[...] = x[n - 8:, :]


def _conv_d_prompt(z, w, n_seq, seq_len, col0, d_qkv):
    tc = 2 * DK
    assert (d_qkv // 3) % tc == 0 and col0 % tc == 0
    nb = d_qkv // tc
    off = col0 // tc
    return pl.pallas_call(
        functools.partial(_conv_d_prompt_body, tiles_per_part=nb // 3),
        grid=(n_seq, nb),
        in_specs=[pl.BlockSpec((seq_len, tc), lambda b, j: (b, off + j)),
                  pl.BlockSpec((w.shape[0], tc), lambda b, j: (0, j))],
        out_specs=[pl.BlockSpec((seq_len, tc), lambda b, j: (b, j)),
                   pl.BlockSpec((None, 8, tc), lambda b, j: (b, 0, j))],
        out_shape=[jax.ShapeDtypeStruct((n_seq * seq_len, d_qkv), F32),
                   jax.ShapeDtypeStruct((n_seq, 8, d_qkv), F32)],
        compiler_params=_params(("parallel", "parallel"), 0),
        name="conv_d_prompt",
    )(z, w)


def _conv_c_sample_body(b_ref, c_ref, x_ref, buf_ref, w_ref, y_ref, nbuf_ref):
    t_new = x_ref.shape[0]
    w = w_ref[...]
    nw = w.shape[0]
    xp = [buf_ref[s] for s in range(nw - 1)] + [c_ref[t] * x_ref[t] for t in range(t_new)]
    for t in range(t_new):
        conv = w[0:1] * xp[t]
        for s in range(1, nw):
            conv = conv + w[s:s + 1] * xp[t + s]
        y_ref[t] = (b_ref[t] * conv).astype(y_ref.dtype)
    for s in range(nw - 1):
        nbuf_ref[s] = xp[t_new + s]


def _conv_c_sample(zt, buf_t, w, d_c):
    t_new, n_seq, _ = zt.shape
    tc = 512
    nb = d_c // tc
    nw = w.shape[0]
    return pl.pallas_call(
        _conv_c_sample_body,
        grid=(nb,),
        in_specs=[pl.BlockSpec((t_new, n_seq, tc), lambda j: (0, 0, j)),
                  pl.BlockSpec((t_new, n_seq, tc), lambda j: (0, 0, nb + j)),
                  pl.BlockSpec((t_new, n_seq, tc), lambda j: (0, 0, 2 * nb + j)),
                  pl.BlockSpec((nw - 1, n_seq, tc), lambda j: (0, 0, j)),
                  pl.BlockSpec((nw, tc), lambda j: (0, j))],
        out_specs=[pl.BlockSpec((t_new, n_seq, tc), lambda j: (0, 0, j)),
                   pl.BlockSpec((nw - 1, n_seq, tc), lambda j: (0, 0, j))],
        out_shape=[jax.ShapeDtypeStruct((t_new, n_seq, d_c), BF16),
                   jax.ShapeDtypeStruct((nw - 1, n_seq, d_c), F32)],
        compiler_params=_params(("parallel",), 0),
        name="conv_c_sample",
    )(zt, zt, zt, buf_t, w)


def _conv_d_sample_body(x_ref, buf_ref, w_ref, y_ref, nbuf_ref, *, tiles_per_part):
    j = pl.program_id(0)
    t_new = x_ref.shape[0]
    w = w_ref[...]
    nw = w.shape[0]
    xp = [buf_ref[s] for s in range(nw - 1)] + [x_ref[t] for t in range(t_new)]
    for t in range(t_new):
        conv = w[0:1] * xp[t]
        for s in range(1, nw):
            conv = conv + w[s:s + 1] * xp[t + s]
        y = _silu(conv)
        y_ref[t] = y * _qkv_factor(y, j, tiles_per_part)
    for s in range(nw - 1):
        nbuf_ref[s] = xp[t_new + s]


def _conv_d_sample(zt, buf_t, w, col0, d_qkv):
    t_new, n_seq, _ = zt.shape
    tc = DK
    nb = d_qkv // tc
    off = col0 // tc
    nw = w.shape[0]
    return pl.pallas_call(
        functools.partial(_conv_d_sample_body, tiles_per_part=nb // 3),
        grid=(nb,),
        in_specs=[pl.BlockSpec((t_new, n_seq, tc), lambda j: (0, 0, off + j)),
                  pl.BlockSpec((nw - 1, n_seq, tc), lambda j: (0, 0, j)),
                  pl.BlockSpec((nw, tc), lambda j: (0, j))],
        out_specs=[pl.BlockSpec((t_new, n_seq, tc), lambda j: (0, 0, j)),
                   pl.BlockSpec((nw - 1, n_seq, tc), lambda j: (0, 0, j))],
        out_shape=[jax.ShapeDtypeStruct((t_new, n_seq, d_qkv), F32),
                   jax.ShapeDtypeStruct((nw - 1, n_seq, d_qkv), F32)],
        compiler_params=_params(("parallel",), 0),
        name="conv_d_sample",
    )(zt, buf_t, w)


def _cumsum_rows(x):
    n = x.shape[0]
    row = lax.broadcasted_iota(jnp.int32, x.shape, 0)
    s = 1
    while s < n:
        x = x + jnp.where(row >= s, pltpu.roll(x, s, 0), 0.0)
        s *= 2
    return x


def _split2(x):
    hi = x.astype(BF16)
    return hi, (x - hi.astype(F32)).astype(BF16)


def _hp_dup(a_parts, b_parts):
    a_hi, a_lo = a_parts
    b_hi, b_lo = b_parts
    lhs = jnp.concatenate([a_hi, a_lo], axis=1)
    rhs = jnp.concatenate([b_hi, b_lo, b_hi, jnp.zeros_like(b_hi)], axis=0)
    return _dot(lhs, rhs)


def _delta_body(q_ref, k_ref, v_ref, zg_ref, gate_ref, alog_ref, dt_ref, on_ref, s0_ref,
                o_ref, sout_ref, s_ref, *, chunk, n_valid, group):
    c = pl.program_id(1)
    nc = pl.num_programs(1)
    dup = 2 * chunk == 128
    width = 2 * chunk if dup else chunk

    @pl.when(c == 0)
    def _():
        s_ref[...] = s0_ref[...]

    gate = gate_ref[...]
    beta_all = _sigmoid(gate)
    x = gate + dt_ref[...]
    softplus = jnp.maximum(x, 0.0) + jnp.log(1.0 + jnp.exp(-jnp.abs(x)))
    g_all = -jnp.exp(alog_ref[...]) * softplus
    row128 = lax.broadcasted_iota(jnp.int32, (chunk, 128), 0)
    g_all = jnp.where(row128 < n_valid, g_all, 0.0)
    gcum = _cumsum_rows(g_all)
    eye = (lax.broadcasted_iota(jnp.int32, (128, 128), 0)
           == lax.broadcasted_iota(jnp.int32, (128, 128), 1)).astype(BF16)
    g_rows = jnp.concatenate([gcum, gcum], axis=0) if dup else gcum
    g_hi, g_mid, g_lo = _split3(g_rows)
    gcum_t = _dot_nt(eye, g_hi) + (_dot_nt(eye, g_mid) + _dot_nt(eye, g_lo))

    ri = lax.broadcasted_iota(jnp.int32, (chunk, width), 0)
    ci = lax.broadcasted_iota(jnp.int32, (chunk, width), 1)
    ci = jnp.where(ci >= chunk, ci - chunk, ci)
    incl = ri >= ci
    strict = ri > ci
    ident = (ri == ci).astype(F32)
    on = on_ref[...]
    nil = 1
    while nil < n_valid:
        nil *= 2
    nil = min(nil, chunk)

    for g0 in range(0, N_HEADS, group):
        heads = list(range(g0, g0 + group))
        pw, tm, qk = {}, {}, {}
        for h in heads:
            hs = slice(h * DK, (h + 1) * DK)
            q = q_ref[:, hs]
            k = k_ref[:, hs]
            beta = beta_all[:, h:h + 1]
            gc = gcum[:, N_HEADS + h:N_HEADS + h + 1]
            gr = gcum_t[N_HEADS + h:N_HEADS + h + 1, :]
            decay = jnp.exp(jnp.where(incl, gc - gr, -jnp.inf))
            k_b = k.astype(BF16)
            rhs = jnp.concatenate([k_b, k_b], axis=0) if dup else k_b
            lhs = jnp.concatenate([k * beta, q], axis=0).astype(BF16)
            r = _dot_nt(lhs, rhs)
            a = jnp.where(strict, r[:chunk] * decay, 0.0)
            qk[h] = jnp.where(incl[:, :chunk], r[chunk:, :chunk] * decay[:, :chunk], 0.0)
            pw[h] = -a
            tm[h] = ident + pw[h]
        p = 1
        while 2 * p < nil:
            for h in heads:
                if dup:
                    p2 = _split2(pw[h])
                    pw[h] = _hp_dup(p2, p2)
                else:
                    pw[h] = _dot_hp(pw[h], pw[h])
            for h in heads:
                if dup:
                    tm[h] = tm[h] + _hp_dup(_split2(tm[h]), _split2(pw[h]))
                else:
                    tm[h] = tm[h] + _dot_hp(tm[h], pw[h])
            p *= 2
        uw = {}
        for h in heads:
            hs = slice(h * DK, (h + 1) * DK)
            beta = beta_all[:, h:h + 1]
            gc = gcum[:, N_HEADS + h:N_HEADS + h + 1]
            kb = k_ref[:, hs] * beta
            rhs = jnp.concatenate([v_ref[:, hs] * beta, kb * jnp.exp(gc)], axis=1).astype(BF16)
            uw[h] = _dot(tm[h][:, :chunk].astype(BF16), rhs)
        ws = {}
        for h in heads:
            hs = slice(h * DK, (h + 1) * DK)
            gc = gcum[:, N_HEADS + h:N_HEADS + h + 1]
            lhs = jnp.concatenate([uw[h][:, DV:], q_ref[:, hs] * jnp.exp(gc)], axis=0).astype(BF16)
            ws[h] = _dot(lhs, s_ref[h].astype(BF16))
        for h in heads:
            hs = slice(h * DK, (h + 1) * DK)
            gc = gcum[:, N_HEADS + h:N_HEADS + h + 1]
            g_last = gcum[chunk - 1:chunk, N_HEADS + h:N_HEADS + h + 1]
            v_new = (uw[h][:, :DV] - ws[h][:chunk]).astype(BF16)
            o = ws[h][chunk:] + _dot(qk[h].astype(BF16), v_new)
            k_dec = k_ref[:, hs] * jnp.exp(g_last - gc)
            s_ref[h] = s_ref[h] * jnp.exp(g_last) + _dot_tn(k_dec.astype(BF16), v_new)
            o = o * lax.rsqrt(jnp.mean(o * o, axis=-1, keepdims=True) + EPS) * on
            o_ref[:, hs] = (o * _silu(zg_ref[:, hs])).astype(o_ref.dtype)

    @pl.when(c == nc - 1)
    def _():
        sout_ref[...] = s_ref[...]


def _delta(qkv, zg, zg_blk, gate, a_log, dt_bias, o_norm, s0, n_seq, chunk, n_valid):
    rows = qkv.shape[0]
    nc = rows // (n_seq * chunk)
    hw = N_HEADS * DK

    def rmap(blk):
        return lambda s, c: (s * nc + c, blk)

    alog = jnp.zeros((1, 128), F32).at[0, N_HEADS:2 * N_HEADS].set(a_log)
    dtb = jnp.zeros((1, 128), F32).at[0, N_HEADS:2 * N_HEADS].set(dt_bias)
    fix = lambda s, c: (0, 0)
    return pl.pallas_call(
        functools.partial(_delta_body, chunk=chunk, n_valid=n_valid, group=DELTA_GROUP),
        grid=(n_seq, nc),
        in_specs=[pl.BlockSpec((chunk, hw), rmap(0)),
                  pl.BlockSpec((chunk, hw), rmap(1)),
                  pl.BlockSpec((chunk, hw), rmap(2)),
                  pl.BlockSpec((chunk, hw), rmap(zg_blk)),
                  pl.BlockSpec((chunk, 128), rmap(0)),
                  pl.BlockSpec((1, 128), fix),
                  pl.BlockSpec((1, 128), fix),
                  pl.BlockSpec((1, DV), fix),
                  pl.BlockSpec((None, N_HEADS, DK, DV), lambda s, c: (s, 0, 0, 0))],
        out_specs=[pl.BlockSpec((chunk, hw), rmap(0)),
                   pl.BlockSpec((None, N_HEADS, DK, DV), lambda s, c: (s, 0, 0, 0))],
        out_shape=[jax.ShapeDtypeStruct((rows, hw), BF16),
                   jax.ShapeDtypeStruct((n_seq, N_HEADS, DK, DV), F32)],
        scratch_shapes=[pltpu.VMEM((N_HEADS, DK, DV), F32)],
        compiler_params=_params(("parallel", "arbitrary"), 0),
        name="gated_delta",
    )(qkv, qkv, qkv, zg, gate, alog, dtb, o_norm.reshape(1, DV), s0)


def _odd_mixer(x, dims, buf_c, buf_d, s0_s, w_in, conv_c_w, conv_d_w, a_log, dt_bias, o_norm,
               w_out, layer_idx, mix_norm):
    n_p, b_p, l_p, b_s, t_s = dims
    d = x.shape[1]
    d_c = conv_c_w.shape[1]
    d_qkv = conv_d_w.shape[1]
    n_in = w_in.shape[2]
    o_qkv = 3 * d_c
    o_gate = o_qkv + d_qkv
    o_tail = o_gate + N_HEADS * DV
    n_tail = n_in - o_tail
    assert o_tail % 128 == 0 and n_tail == 2 * N_HEADS and o_gate % (N_HEADS * DV) == 0

    h = _rms(x, mix_norm, BF16)
    z = _mmw([(h, 0, d)], [[(jnp.swapaxes(w_in, 1, 2), layer_idx, 0)]], [], _epi_plain, F32, 1088, 512,
             "odd_in", w_t=True)

    y_c_p, tail_c_p = _conv_c_prompt(z, conv_c_w, b_p, l_p, d_c)
    qkv_p, tail_d_p = _conv_d_prompt(z, conv_d_w, b_p, l_p, o_qkv, d_qkv)
    chunk_p = math.gcd(l_p, DN_CHUNK)
    gate_p = jnp.pad(z[:n_p, o_tail:], ((0, 0), (0, 128 - n_tail)))
    s0_p = jnp.zeros((b_p, N_HEADS, DK, DV), F32)
    o_p, st_p = _delta(qkv_p, z, o_gate // (N_HEADS * DV), gate_p, a_log, dt_bias, o_norm, s0_p,
                       b_p, chunk_p, chunk_p)

    zt = jnp.transpose(z[n_p:].reshape(b_s, t_s, n_in), (1, 0, 2))
    y_c_t, nbuf_c_t = _conv_c_sample(zt, jnp.transpose(buf_c, (1, 0, 2)), conv_c_w, d_c)
    qkv_t, nbuf_d_t = _conv_d_sample(zt, jnp.transpose(buf_d, (1, 0, 2)), conv_d_w, o_qkv, d_qkv)
    pad_t = SAMPLE_CHUNK - t_s
    assert pad_t >= 0

    def to_seq(a_t):
        a = jnp.transpose(a_t, (1, 0, 2))
        a = jnp.pad(a, ((0, 0), (0, pad_t), (0, 0)))
        return a.reshape(b_s * SAMPLE_CHUNK, a.shape[2])

    qkv_s = to_seq(qkv_t)
    zs = z[n_p:].reshape(b_s, t_s, n_in)
    zg_s = jnp.pad(zs[:, :, o_gate:o_tail], ((0, 0), (0, pad_t), (0, 0))).reshape(b_s * SAMPLE_CHUNK, -1)
    gate_s = jnp.pad(zs[:, :, o_tail:], ((0, 0), (0, pad_t), (0, 128 - n_tail))).reshape(b_s * SAMPLE_CHUNK, 128)
    o_s8, st_s = _delta(qkv_s, zg_s, 0, gate_s, a_log, dt_bias, o_norm, s0_s, b_s, SAMPLE_CHUNK, t_s)
    o_s = o_s8.reshape(b_s, SAMPLE_CHUNK, -1)[:, :t_s].reshape(b_s * t_s, -1)
    y_c_s = jnp.transpose(y_c_t, (1, 0, 2)).reshape(b_s * t_s, d_c)

    y_c = jnp.concatenate([y_c_p, y_c_s], axis=0)
    o_all = jnp.concatenate([o_p, o_s], axis=0)
    kw = y_c.shape[1]
    x = _mm([y_c, o_all], [[w_out[layer_idx, :kw].astype(BF16), w_out[layer_idx, kw:].astype(BF16)]],
            [(x, "ij")], functools.partial(_epi_resid, scale=1.0), F32, 1088, 512, "odd_out")
    nw_c = conv_c_w.shape[0] - 1
    nw_d = conv_d_w.shape[0] - 1
    outs = (tail_c_p[:, 8 - nw_c:], jnp.transpose(nbuf_c_t, (1, 0, 2)),
            tail_d_p[:, 8 - nw_d:], jnp.transpose(nbuf_d_t, (1, 0, 2)), st_p, st_s)
    return x, outs


def _rope_tables(pos):
    half = ROPE // 2
    inv = ROPE_THETA ** (-jnp.arange(half, dtype=F32) / half)
    ang = pos[:, None] * inv[None, :]
    cos = jnp.cos(ang)
    sin = jnp.sin(ang)
    n = pos.shape[0]
    cos_f = jnp.concatenate([cos, cos], axis=1)
    sin_f = jnp.concatenate([-sin, sin], axis=1)
    z = jnp.zeros((n, 128 - ROPE), F32)
    cos_k = jnp.concatenate([cos_f, z], axis=1)
    sin_k = jnp.concatenate([sin_f, z], axis=1)
    cos_q = jnp.concatenate([jnp.ones((n, NOPE), F32), cos_f, z], axis=1)
    sin_q = jnp.concatenate([jnp.zeros((n, NOPE), F32), sin_f, z], axis=1)
    return cos_k, sin_k, cos_q, sin_q


def kernel(x_prompt, x_sample, cache_mla_latent, cache_mla_krope, state_conv_c, state_conv_d, state_delta, page_table, ffn1_norm, ffn1_w_gate, ffn1_w_up, ffn1_w_down, mix_norm, ffn2_norm, ffn2_w_gate, ffn2_w_up, ffn2_w_down, even_w_in, gmlp_v_norm, gmlp_ws, gmlp_bs, mla_q_norm, mla_w_uq, mla_kv_norm, mla_w_uk, mla_w_uv, even_w_out, odd_w_in, conv_c_w, conv_d_w, delta_a_log, delta_dt_bias, delta_o_norm, odd_w_out, final_norm):
    b_p, l_p, d = x_prompt.shape
    b_s, t_s, _ = x_sample.shape
    n_p = b_p * l_p
    n_s = b_s * t_s
    depth = ffn1_norm.shape[0]
    dims = (n_p, b_p, l_p, b_s, t_s)
    past_len = page_table.shape[1] * PAGE
    pos = jnp.concatenate([jnp.tile(jnp.arange(l_p, dtype=F32), b_p),
                           jnp.tile(jnp.arange(t_s, dtype=F32) + past_len, b_s)])
    tabs = _rope_tables(pos)

    x = jnp.concatenate([x_prompt.reshape(n_p, d), x_sample.reshape(n_s, d)], axis=0)
    lat, kr, vrow = [], [], []
    cc_p, cc_s, cd_p, cd_s, sd_p, sd_s = [], [], [], [], [], []
    for layer in range(depth):
        x = _ffn_half(x, ffn1_norm[layer], ffn1_w_gate, ffn1_w_up, ffn1_w_down, layer)
        if layer % 2 == 0:
            e = layer // 2
            x, v_rows, c, kpe = _even_mixer(
                x, dims, tabs, cache_mla_latent, cache_mla_krope, page_table, e, even_w_in,
                gmlp_v_norm[e], gmlp_ws[e], gmlp_bs[e], mla_q_norm[e], mla_w_uq[e], mla_kv_norm[e],
                mla_w_uk[e], mla_w_uv[e], even_w_out, mix_norm[layer])
            lat.append(c)
            kr.append(kpe)
            vrow.append(v_rows[n_p:])
        else:
            o = layer // 2
            x, outs = _odd_mixer(
                x, dims, state_conv_c[o], state_conv_d[o], state_delta[o], odd_w_in, conv_c_w[o],
                conv_d_w[o], delta_a_log[o], delta_dt_bias[o], delta_o_norm[o], odd_w_out, o,
                mix_norm[layer])
            cc_p.append(outs[0]); cc_s.append(outs[1]); cd_p.append(outs[2]); cd_s.append(outs[3])
            sd_p.append(outs[4]); sd_s.append(outs[5])
        x = _ffn_half(x, ffn2_norm[layer], ffn2_w_gate, ffn2_w_up, ffn2_w_down, layer)

    y_p = _rms(x, final_norm, F32, 0, n_p)
    y_s = _rms(x, final_norm, F32, n_p, n_s)
    lat = jnp.stack(lat)
    kr = jnp.stack(kr)
    n_e = lat.shape[0]
    return (y_p.reshape(b_p, l_p, d), y_s.reshape(b_s, t_s, d),
            lat[:, :n_p].reshape(n_e, b_p, l_p, -1), kr[:, :n_p].reshape(n_e, b_p, l_p, -1),
            lat[:, n_p:].reshape(n_e, b_s, t_s, -1), kr[:, n_p:].reshape(n_e, b_s, t_s, -1),
            jnp.stack(vrow).reshape(n_e, b_s, t_s, -1),
            jnp.stack(cc_p), jnp.stack(cc_s), jnp.stack(cd_p), jnp.stack(cd_s),
            jnp.stack(sd_p), jnp.stack(sd_s))
```

```python
import functools
import math

import jax
import jax.numpy as jnp
from jax import lax
from jax.experimental import pallas as pl
from jax.experimental.pallas import tpu as pltpu

F32 = jnp.float32
BF16 = jnp.bfloat16
EPS = 1e-6

PAGE = 128
GROUP = 128
N_HEADS = 16
NOPE = 128
ROPE = 64
HEAD_PAD = 256
KV_LORA = 512
V_DIM = 128
DK = 128
DV = 128
ROPE_THETA = 10000.0
ATT_SCALE = (NOPE + ROPE) ** -0.5
DN_CHUNK = 64
SAMPLE_CHUNK = 8
PAGES_PER_STEP = 64
PAGES_PER_GROUP = 16
PREP_BLK = 256
DELTA_GROUP = 16

VMEM_CAP_BYTES = 60 * 1024 * 1024


def _pick(n, target, mult=8):
    for d in range(min(n, target), 0, -1):
        if n % d == 0 and d % mult == 0:
            return d
    return n


def _params(sem, vmem_bytes):
    return pltpu.CompilerParams(
        dimension_semantics=sem,
        vmem_limit_bytes=int(min(VMEM_CAP_BYTES, max(vmem_bytes, 32 * 1024 * 1024))))


def _sigmoid(x):
    return 1.0 / (1.0 + jnp.exp(-x))


def _silu(x):
    return x * _sigmoid(x)


def _dot(a, b):
    return jnp.dot(a, b, preferred_element_type=F32)


def _dot_nt(a, b):
    return lax.dot_general(a, b, (((1,), (1,)), ((), ())), preferred_element_type=F32)


def _dot_tn(a, b):
    return lax.dot_general(a, b, (((0,), (0,)), ((), ())), preferred_element_type=F32)


def _split3(x):
    hi = x.astype(BF16)
    r = x - hi.astype(F32)
    mid = r.astype(BF16)
    lo = (r - mid.astype(F32)).astype(BF16)
    return hi, mid, lo


def _dot_hp(a, b):
    a_hi, a_mid, _ = _split3(a)
    b_hi, b_mid, _ = _split3(b)
    return _dot(a_hi, b_hi) + (_dot(a_hi, b_mid) + _dot(a_mid, b_hi))


def _rms_body(x_ref, g_ref, o_ref):
    x = x_ref[...]
    y = x * lax.rsqrt(jnp.mean(x * x, axis=-1, keepdims=True) + EPS)
    o_ref[...] = (y * g_ref[...]).astype(o_ref.dtype)


def _rms(x, g, out_dtype, row0=0, rows=None):
    d = x.shape[1]
    m = x.shape[0] if rows is None else rows
    tm = _pick(math.gcd(m, row0) if row0 else m, 256, 16)
    blk0 = row0 // tm
    return pl.pallas_call(
        _rms_body,
        grid=(m // tm,),
        in_specs=[pl.BlockSpec((tm, d), lambda i: (blk0 + i, 0)),
                  pl.BlockSpec((1, d), lambda i: (0, 0))],
        out_specs=pl.BlockSpec((tm, d), lambda i: (i, 0)),
        out_shape=jax.ShapeDtypeStruct((m, d), out_dtype),
        compiler_params=_params(("parallel",), 6 * tm * d * 4),
        name="rmsnorm",
    )(x, g.reshape(1, d))


def _mm_body(*refs, n_w, n_lhs, n_x, epi):
    lhs = [refs[l][...] for l in range(n_lhs)]
    accs = []
    for k in range(n_w):
        acc = _dot(lhs[0], refs[n_lhs + k * n_lhs][...])
        for l in range(1, n_lhs):
            acc = acc + _dot(lhs[l], refs[n_lhs + k * n_lhs + l][...])
        accs.append(acc)
    base = n_lhs + n_w * n_lhs
    xs = [refs[base + k][...] for k in range(n_x)]
    o_ref = refs[base + n_x]
    o_ref[...] = epi(accs, xs).astype(o_ref.dtype)


def _epi_plain(accs, xs):
    return accs[0]


def _epi_swiglu(accs, xs):
    return _silu(accs[0]) * accs[1]


def _epi_resid(accs, xs, scale):
    return xs[0] + scale * accs[0]


def _epi_rope(accs, xs):
    return accs[0] * xs[0] + accs[1] * xs[1]


def _mm(lhs, ws, extras, epi, out_dtype, tm_target, tn, name):
    if not isinstance(lhs, (list, tuple)):
        lhs, ws = [lhs], [[w] for w in ws]
    m = lhs[0].shape[0]
    n = ws[0][0].shape[1]
    tm = _pick(m, tm_target, 16)
    assert n % tn == 0, (n, tn)
    ktot = sum(a.shape[1] for a in lhs)
    in_specs = [pl.BlockSpec((tm, a.shape[1]), lambda i, j: (i, 0)) for a in lhs]
    for wl in ws:
        in_specs += [pl.BlockSpec((w.shape[0], tn), lambda i, j: (0, j)) for w in wl]
    for _, kind in extras:
        if kind == "ij":
            in_specs.append(pl.BlockSpec((tm, tn), lambda i, j: (i, j)))
        else:
            in_specs.append(pl.BlockSpec((tm, tn), lambda i, j: (i, 0)))
    vmem = 2 * (tm * ktot * 2 + len(ws) * ktot * tn * 2 + (len(extras) + 1) * tm * tn * 4)
    vmem += (len(ws) + 1) * tm * tn * 4 + (4 << 20)
    return pl.pallas_call(
        functools.partial(_mm_body, n_w=len(ws), n_lhs=len(lhs), n_x=len(extras), epi=epi),
        grid=(m // tm, n // tn),
        in_specs=in_specs,
        out_specs=pl.BlockSpec((tm, tn), lambda i, j: (i, j)),
        out_shape=jax.ShapeDtypeStruct((m, n), out_dtype),
        compiler_params=_params(("parallel", "arbitrary"), vmem),
        name=name,
    )(*lhs, *[w for wl in ws for w in wl], *[a for a, _ in extras])


def _headmm_body(l_ref, w_ref, o_ref):
    o_ref[...] = _dot(l_ref[...], w_ref[...]).astype(o_ref.dtype)


def _headmm(lhs, w, out_dtype, name):
    m = lhs.shape[0]
    nh, kd, nd = w.shape
    return pl.pallas_call(
        _headmm_body,
        grid=(nh,),
        in_specs=[pl.BlockSpec((m, kd), lambda h: (0, h)),
                  pl.BlockSpec((None, kd, nd), lambda h: (h, 0, 0))],
        out_specs=pl.BlockSpec((m, nd), lambda h: (0, h)),
        out_shape=jax.ShapeDtypeStruct((m, nh * nd), out_dtype),
        compiler_params=_params(("parallel",), 0),
        name=name,
    )(lhs, w)


def _mmw_body(*refs, n_acc, n_lhs, n_x, epi, w_t):
    n_w = n_acc * n_lhs
    lhs_refs = refs[:n_lhs]
    w_refs = refs[n_lhs:n_lhs + n_w]
    x_refs = refs[n_lhs + n_w:n_lhs + n_w + n_x]
    o_ref = refs[n_lhs + n_w + n_x]
    wb_refs = refs[n_lhs + n_w + n_x + 1:]

    @pl.when(pl.program_id(1) == 0)
    def _():
        for w_ref, wb_ref in zip(w_refs, wb_refs):
            wb_ref[...] = w_ref[...].astype(BF16)

    dot = _dot_nt if w_t else _dot
    lhs = [r[...] for r in lhs_refs]
    accs = []
    for a in range(n_acc):
        acc = dot(lhs[0], wb_refs[a * n_lhs][...])
        for l in range(1, n_lhs):
            acc = acc + dot(lhs[l], wb_refs[a * n_lhs + l][...])
        accs.append(acc)
    o_ref[...] = epi(accs, [r[...] for r in x_refs]).astype(o_ref.dtype)


def _mmw(lhs_list, w_list, extras, epi, out_dtype, tm_target, tn, name, w_t=False):
    m = lhs_list[0][0].shape[0]
    n = w_list[0][0][0].shape[1 if w_t else 2]
    tm = _pick(m, tm_target, 16)
    n_lhs = len(lhs_list)
    in_specs, args = [], []
    vmem = 0
    for arr, cb, k in lhs_list:
        in_specs.append(pl.BlockSpec((tm, k), lambda j, i, cb=cb: (i, cb)))
        args.append(arr)
        vmem += 2 * tm * k * arr.dtype.itemsize
    scratch = []
    for ws in w_list:
        assert len(ws) == n_lhs
        for (arr, layer, rb), (_, _, k) in zip(ws, lhs_list):
            if w_t:
                in_specs.append(pl.BlockSpec((None, tn, k), lambda j, i, layer=layer, rb=rb: (layer, j, rb)))
                scratch.append(pltpu.VMEM((tn, k), BF16))
            else:
                in_specs.append(pl.BlockSpec((None, k, tn), lambda j, i, layer=layer, rb=rb: (layer, rb, j)))
                scratch.append(pltpu.VMEM((k, tn), BF16))
            args.append(arr)
            vmem += 2 * k * tn * 4 + k * tn * 2
    for arr, kind in extras:
        if kind == "ij":
            in_specs.append(pl.BlockSpec((tm, tn), lambda j, i: (i, j)))
        else:
            in_specs.append(pl.BlockSpec((tm, tn), lambda j, i: (i, 0)))
        args.append(arr)
    vmem += (2 * (len(extras) + 1) + len(w_list) + 1) * tm * tn * 4 + (4 << 20)
    return pl.pallas_call(
        functools.partial(_mmw_body, n_acc=len(w_list), n_lhs=n_lhs, n_x=len(extras), epi=epi, w_t=w_t),
        grid=(pl.cdiv(n, tn), m // tm),
        in_specs=in_specs,
        out_specs=pl.BlockSpec((tm, tn), lambda j, i: (i, j)),
        out_shape=jax.ShapeDtypeStruct((m, n), out_dtype),
        scratch_shapes=scratch,
        compiler_params=_params(("parallel", "arbitrary"), vmem),
        name=name,
    )(*args)


def _ffn_half(x, g, wg, wu, wd, layer):
    d, f = wg.shape[1:]
    assert f % 256 == 0
    kh = f // 2
    assert kh % 128 == 0
    h = _rms(x, g, BF16)
    a = _mmw([(h, 0, d)], [[(wg, layer, 0)], [(wu, layer, 0)]], [], _epi_swiglu, BF16, 1088, 256,
             "ffn_gate_up")
    half = functools.partial(_epi_resid, scale=0.5)
    x = _mmw([(a, 0, kh)], [[(wd, layer, 0)]], [(x, "ij")], half, F32, 544, 512, "ffn_down_lo")
    return _mmw([(a, 1, kh)], [[(wd, layer, 1)]], [(x, "ij")], half, F32, 544, 512, "ffn_down_hi")


def _gmlp_body(z_ref, w_ref, b_ref, vn_ref, a_ref, v_ref, *, n_prompt_tiles, t_sample, d_a):
    i = pl.program_id(0)
    z = z_ref[...]
    c0 = math.sqrt(2.0 / math.pi)
    a = 0.5 * z * (1.0 + jnp.tanh(c0 * (z + 0.044715 * (z * z * z))))
    u = a[:, :d_a]
    v = a[:, d_a:]
    v = v * lax.rsqrt(jnp.mean(v * v, axis=-1, keepdims=True) + EPS) * vn_ref[...]
    v_ref[...] = v
    row = lax.broadcasted_iota(jnp.int32, (GROUP, GROUP), 0)
    col = lax.broadcasted_iota(jnp.int32, (GROUP, GROUP), 1)
    same_seq = (row // t_sample) == (col // t_sample)
    mask = (col <= row) & (same_seq | (i < n_prompt_tiles))
    for g in range(d_a // GROUP):
        sl = slice(g * GROUP, (g + 1) * GROUP)
        w = jnp.where(mask, w_ref[g], 0.0).astype(BF16)
        s = _dot(w, v[:, sl].astype(BF16)) + b_ref[:, sl]
        a_ref[:, sl] = (u[:, sl] * s).astype(a_ref.dtype)


def _gmlp(z_a, ws, bs, v_norm, n_prompt, t_sample, d_a):
    m = z_a.shape[0]
    ng = d_a // GROUP
    assert n_prompt % GROUP == 0 and (m - n_prompt) % GROUP == 0 and GROUP % t_sample == 0
    npt = n_prompt // GROUP
    rep = GROUP // t_sample
    w_all = jnp.stack([ws, jnp.tile(ws[:, :t_sample, :t_sample], (1, rep, rep))])
    b_p = jnp.repeat(bs.T, GROUP, axis=1)
    b_s = jnp.repeat(jnp.tile(bs[:, :t_sample].T, (rep, 1)), GROUP, axis=1)
    b_all = jnp.stack([b_p, b_s])

    def sel(i):
        return jnp.where(i < npt, 0, 1)

    return pl.pallas_call(
        functools.partial(_gmlp_body, n_prompt_tiles=npt, t_sample=t_sample, d_a=d_a),
        grid=(m // GROUP,),
        in_specs=[pl.BlockSpec((GROUP, 2 * d_a), lambda i: (i, 0)),
                  pl.BlockSpec((None, ng, GROUP, GROUP), lambda i: (sel(i), 0, 0, 0)),
                  pl.BlockSpec((None, GROUP, d_a), lambda i: (sel(i), 0, 0)),
                  pl.BlockSpec((1, d_a), lambda i: (0, 0))],
        out_specs=[pl.BlockSpec((GROUP, d_a), lambda i: (i, 0)),
                   pl.BlockSpec((GROUP, d_a), lambda i: (i, 0))],
        out_shape=[jax.ShapeDtypeStruct((m, d_a), BF16),
                   jax.ShapeDtypeStruct((m, d_a), F32)],
        compiler_params=_params(("parallel",), 0),
        name="gmlp",
    )(z_a, w_all, b_all, v_norm.reshape(1, d_a))


def _mla_prep_body(*refs, q_lora, n_blk):
    z_refs = refs[:n_blk]
    qg_ref, kg_ref, cos_ref, sin_ref, qn_ref, c_ref, kpe_ref, ck_ref = refs[n_blk:]
    nq = q_lora // PREP_BLK
    nkv = KV_LORA // PREP_BLK
    zq = jnp.concatenate([z_refs[b][...] for b in range(nq)], axis=1)
    qn_ref[...] = (zq * lax.rsqrt(jnp.mean(zq * zq, axis=-1, keepdims=True) + EPS)
                   * qg_ref[...]).astype(qn_ref.dtype)
    zkv = jnp.concatenate([z_refs[nq + b][...] for b in range(nkv)], axis=1)
    c = zkv * lax.rsqrt(jnp.mean(zkv * zkv, axis=-1, keepdims=True) + EPS) * kg_ref[...]
    c_ref[...] = c
    zr = z_refs[nq + nkv][:, :128]
    lane = lax.broadcasted_iota(jnp.int32, zr.shape, 1)
    zr = jnp.where(lane < ROPE, zr, 0.0)
    src = lax.broadcasted_iota(jnp.int32, (128, 128), 0)
    dst = lax.broadcasted_iota(jnp.int32, (128, 128), 1)
    half = ROPE // 2
    partner = jnp.where(dst < half, dst + half, dst - half)
    perm = ((src == partner) & (dst < ROPE)).astype(BF16)
    r_hi, r_mid, r_lo = _split3(zr)
    zrp = _dot(r_hi, perm) + (_dot(r_mid, perm) + _dot(r_lo, perm))
    kpe = zr * cos_ref[...] + zrp * sin_ref[...]
    kpe_ref[...] = kpe[:, :ROPE]
    ck_ref[:, :KV_LORA] = c.astype(ck_ref.dtype)
    one = (lax.broadcasted_iota(jnp.int32, kpe.shape, 1) == ROPE).astype(F32)
    ck_ref[:, KV_LORA:] = (kpe + one).astype(ck_ref.dtype)


def _mla_prep(z, col0, q_norm, kv_norm, cos128, sin128, q_lora):
    m = z.shape[0]
    tm = _pick(m, 544, 16)
    assert col0 % PREP_BLK == 0 and q_lora % PREP_BLK == 0 and KV_LORA % PREP_BLK == 0
    blk0 = col0 // PREP_BLK
    n_blk = (q_lora + KV_LORA) // PREP_BLK + 1
    row = lambda i: (i, 0)
    fix = lambda i: (0, 0)
    return pl.pallas_call(
        functools.partial(_mla_prep_body, q_lora=q_lora, n_blk=n_blk),
        grid=(m // tm,),
        in_specs=[pl.BlockSpec((tm, PREP_BLK), lambda i, b=b: (i, blk0 + b)) for b in range(n_blk)] + [
                  pl.BlockSpec((1, q_lora), fix),
                  pl.BlockSpec((1, KV_LORA), fix),
                  pl.BlockSpec((tm, 128), row),
                  pl.BlockSpec((tm, 128), row)],
        out_specs=[pl.BlockSpec((tm, q_lora), row),
                   pl.BlockSpec((tm, KV_LORA), row),
                   pl.BlockSpec((tm, ROPE), row),
                   pl.BlockSpec((tm, KV_LORA + 128), row)],
        out_shape=[jax.ShapeDtypeStruct((m, q_lora), BF16),
                   jax.ShapeDtypeStruct((m, KV_LORA), F32),
                   jax.ShapeDtypeStruct((m, ROPE), F32),
                   jax.ShapeDtypeStruct((m, KV_LORA + 128), BF16)],
        compiler_params=_params(("parallel",), 0),
        name="mla_prep",
    )(*([z] * n_blk), q_norm.reshape(1, q_lora), kv_norm.reshape(1, KV_LORA), cos128, sin128)


def _attn_prompt_body(qi_ref, ki_ref, q_ref, k_ref, v_ref, o_ref, m_ref, l_ref, acc_ref, *, tq, tk):
    qi = qi_ref[pl.program_id(1)]
    ki = ki_ref[pl.program_id(1)]

    @pl.when(ki == 0)
    def _():
        m_ref[...] = jnp.full(m_ref.shape, -jnp.inf, F32)
        l_ref[...] = jnp.zeros(l_ref.shape, F32)
        acc_ref[...] = jnp.zeros(acc_ref.shape, F32)

    def scores(h):
        qs = slice(h * HEAD_PAD, (h + 1) * HEAD_PAD)
        return _dot_nt(q_ref[:, qs], k_ref[:, qs]) * ATT_SCALE

    def process(diagonal):
        if diagonal:
            mask = (lax.broadcasted_iota(jnp.int32, (tq, tk), 1)
                    <= lax.broadcasted_iota(jnp.int32, (tq, tk), 0))
        s_next = scores(0)
        for h in range(N_HEADS):
            vs = slice(h * V_DIM, (h + 1) * V_DIM)
            s = s_next
            if h + 1 < N_HEADS:
                s_next = scores(h + 1)
            if diagonal:
                s = jnp.where(mask, s, -jnp.inf)
            m_old = m_ref[h]
            m_new = jnp.maximum(m_old, jnp.max(s, axis=-1, keepdims=True))
            corr = jnp.exp(m_old - m_new)
            p = jnp.concatenate([jnp.exp(s[:, t * 128:(t + 1) * 128] - m_new) for t in range(tk // 128)],
                                axis=1).astype(BF16)
            pv = _dot(p, v_ref[:, 2 * h * V_DIM:2 * (h + 1) * V_DIM])
            l_ref[h] = l_ref[h] * corr + pv[:, V_DIM:]
            acc_ref[:, vs] = acc_ref[:, vs] * corr + pv[:, :V_DIM]
            m_ref[h] = m_new

    @pl.when(ki < qi)
    def _():
        process(False)

    @pl.when(ki == qi)
    def _():
        process(True)
        for h in range(N_HEADS):
            vs = slice(h * V_DIM, (h + 1) * V_DIM)
            o_ref[:, vs] = (acc_ref[:, vs] / l_ref[h]).astype(o_ref.dtype)


def _attn_prompt(q_cat, kv, n_seq, seq_len):
    tq = _pick(seq_len, 256, 128)
    tk = tq
    nq = seq_len // tq
    qw = N_HEADS * HEAD_PAD
    vw = N_HEADS * V_DIM
    assert V_DIM == 128 and tk % 128 == 0 and qw == 2 * vw
    pairs = [(a, b) for a in range(nq) for b in range(a + 1)]
    qi_tab = jnp.asarray([a for a, _ in pairs], jnp.int32)
    ki_tab = jnp.asarray([b for _, b in pairs], jnp.int32)

    def q_map(b, p, qi_t, ki_t):
        return (b * nq + qi_t[p], 0)

    def k_map(b, p, qi_t, ki_t):
        return (b * nq + ki_t[p], 0)

    def v_map(b, p, qi_t, ki_t):
        return (b * nq + ki_t[p], 1)

    grid_spec = pltpu.PrefetchScalarGridSpec(
        num_scalar_prefetch=2,
        grid=(n_seq, len(pairs)),
        in_specs=[pl.BlockSpec((tq, qw), q_map),
                  pl.BlockSpec((tk, qw), k_map),
                  pl.BlockSpec((tk, 2 * vw), v_map)],
        out_specs=pl.BlockSpec((tq, vw), q_map),
        scratch_shapes=[pltpu.VMEM((N_HEADS, tq, 128), F32),
                        pltpu.VMEM((N_HEADS, tq, 128), F32),
                        pltpu.VMEM((tq, vw), F32)])
    return pl.pallas_call(
        functools.partial(_attn_prompt_body, tq=tq, tk=tk),
        grid_spec=grid_spec,
        out_shape=jax.ShapeDtypeStruct((n_seq * seq_len, vw), BF16),
        compiler_params=_params(("parallel", "arbitrary"), 0),
        name="attn_prompt",
    )(qi_tab, ki_tab, q_cat, kv, kv)


def _decode_body(pt_ref, qa_ref, qp_ref, sc_ref, sk_ref, lat_hbm, kr_hbm, o_ref,
                 m_ref, l_ref, acc_ref, lat_buf, kr_buf, sem, *, pps, gsz, layer):
    b = pl.program_id(0)
    c = pl.program_id(1)
    nb = pl.num_programs(0)
    nc = pl.num_programs(1)
    step = b * nc + c
    slot = step % 2
    qa = qa_ref[...]
    qp = qp_ref[...]
    nrow = qa.shape[0]

    def page_copies(page, slot_, k):
        return (pltpu.make_async_copy(lat_hbm.at[layer, page], lat_buf.at[slot_, k], sem.at[0, slot_]),
                pltpu.make_async_copy(kr_hbm.at[layer, page], kr_buf.at[slot_, k], sem.at[1, slot_]))

    def start_step(bb, cc, slot_):
        for k in range(pps):
            for cp in page_copies(pt_ref[bb, cc * pps + k], slot_, k):
                cp.start()

    @pl.when(step == 0)
    def _():
        start_step(0, 0, 0)

    @pl.when(step + 1 < nb * nc)
    def _():
        wrap = c + 1 == nc
        start_step(jnp.where(wrap, b + 1, b), jnp.where(wrap, 0, c + 1), 1 - slot)

    for k in range(pps):
        for cp in page_copies(0, slot, k):
            cp.wait()
    lat_refs = [lat_buf.at[slot, k] for k in range(pps)]
    kr_refs = [kr_buf.at[slot, k] for k in range(pps)]

    def scores(kc, kpt):
        return (_dot_nt(qa, kc) + _dot(qp, kpt)) * ATT_SCALE

    def update(state, s, kc):
        m_old, l_old, acc = state
        m_new = jnp.maximum(m_old, jnp.max(s, axis=-1, keepdims=True))
        corr = jnp.exp(m_old - m_new)
        p = jnp.exp(s - m_new)
        l_new = l_old * corr + jnp.sum(p, axis=-1, keepdims=True)
        return m_new, l_new, acc * corr + _dot(p.astype(BF16), kc)

    def store(state):
        m_ref[...] = jnp.broadcast_to(state[0], m_ref.shape)
        l_ref[...] = jnp.broadcast_to(state[1], l_ref.shape)
        acc_ref[...] = state[2]

    @pl.when(c == 0)
    def _():
        key = lax.broadcasted_iota(jnp.int32, (nrow, PAGE), 1)
        tok = lax.broadcasted_iota(jnp.int32, (nrow, PAGE), 0) // N_HEADS
        kc = sc_ref[...].astype(BF16)
        s = jnp.where(key <= tok, scores(kc, sk_ref[...].astype(BF16)), -jnp.inf)
        init = (jnp.full((nrow, 1), -jnp.inf, F32), jnp.zeros((nrow, 1), F32),
                jnp.zeros((nrow, KV_LORA), F32))
        store(update(init, s, kc))

    def load_group(g):
        ks = range(g * gsz, (g + 1) * gsz)
        kc = jnp.concatenate([lat_refs[k][...].astype(BF16) for k in ks], axis=0)
        kpt = jnp.concatenate([kr_refs[k][...].astype(BF16) for k in ks], axis=1)
        return kc, kpt

    state = (m_ref[:, :1], l_ref[:, :1], acc_ref[...])
    kc, kpt = load_group(0)
    s = scores(kc, kpt)
    for g in range(pps // gsz):
        if g + 1 < pps // gsz:
            kc_next, kpt_next = load_group(g + 1)
            s_next = scores(kc_next, kpt_next)
        state = update(state, s, kc)
        if g + 1 < pps // gsz:
            kc, s = kc_next, s_next
    store(state)

    @pl.when(c == nc - 1)
    def _():
        o_ref[...] = (acc_ref[...] / l_ref[:, :1]).astype(o_ref.dtype)


def _decode(q_abs, q_pe, self_c, self_kt, cache_lat, cache_krt, page_table, layer):
    bs, nrow, _ = q_abs.shape
    n_pages = page_table.shape[1]
    pps = _pick(n_pages, PAGES_PER_STEP, 1)
    gsz = _pick(pps, PAGES_PER_GROUP, 1)
    nc = n_pages // pps

    def fix(b, c, pt):
        return (b, 0, 0)

    in_specs = [pl.BlockSpec((None, nrow, KV_LORA), fix),
                pl.BlockSpec((None, nrow, ROPE), fix),
                pl.BlockSpec((None, PAGE, KV_LORA), fix),
                pl.BlockSpec((None, ROPE, PAGE), fix),
                pl.BlockSpec(memory_space=pl.ANY),
                pl.BlockSpec(memory_space=pl.ANY)]
    grid_spec = pltpu.PrefetchScalarGridSpec(
        num_scalar_prefetch=1,
        grid=(bs, nc),
        in_specs=in_specs,
        out_specs=pl.BlockSpec((None, nrow, KV_LORA), fix),
        scratch_shapes=[pltpu.VMEM((nrow, 128), F32),
                        pltpu.VMEM((nrow, 128), F32),
                        pltpu.VMEM((nrow, KV_LORA), F32),
                        pltpu.VMEM((2, pps, PAGE, KV_LORA), F32),
                        pltpu.VMEM((2, pps, ROPE, PAGE), F32),
                        pltpu.SemaphoreType.DMA((2, 2))])
    vmem = 2 * pps * (PAGE * KV_LORA + ROPE * PAGE) * 4 + (16 << 20)
    return pl.pallas_call(
        functools.partial(_decode_body, pps=pps, gsz=gsz, layer=layer),
        grid_spec=grid_spec,
        out_shape=jax.ShapeDtypeStruct((bs, nrow, KV_LORA), BF16),
        compiler_params=_params(("arbitrary", "arbitrary"), vmem),
        name="attn_decode",
    )(page_table, q_abs, q_pe, self_c, self_kt, cache_lat, cache_krt)


def _even_mixer(x, dims, tabs, cache_lat, cache_kr, page_table, e, w_in, v_norm, ws, bs,
                q_norm, w_uq, kv_norm, w_uk, w_uv, w_out, mix_norm):
    n_p, b_p, l_p, b_s, t_s = dims
    cos_k, sin_k, cos_q, sin_q = tabs
    d = x.shape[1]
    d_a = v_norm.shape[0]
    q_lora = q_norm.shape[0]
    n_s = x.shape[0] - n_p
    half = ROPE // 2
    perm = jnp.concatenate([jnp.arange(half, ROPE), jnp.arange(0, half)])

    h = _rms(x, mix_norm, BF16)
    o_q = 2 * d_a
    assert w_in.shape[2] == o_q + q_lora + KV_LORA + ROPE
    z = _mmw([(h, 0, d)], [[(jnp.swapaxes(w_in, 1, 2), e, 0)]], [], _epi_plain, F32, 1088, 512,
             "even_in", w_t=True)

    a_out, v_rows = _gmlp(z, ws, bs, v_norm, n_p, t_s, d_a)
    qn, c, kpe, ck = _mla_prep(z, o_q, q_norm, kv_norm, cos_k, sin_k, q_lora)

    zpad = jnp.zeros((q_lora, N_HEADS, HEAD_PAD - NOPE - ROPE), F32)
    w1 = jnp.concatenate([w_uq, zpad], axis=2).reshape(q_lora, N_HEADS * HEAD_PAD).astype(BF16)
    w2 = jnp.concatenate([jnp.zeros((q_lora, N_HEADS, NOPE), F32), w_uq[:, :, NOPE:][:, :, perm], zpad],
                         axis=2).reshape(q_lora, N_HEADS * HEAD_PAD).astype(BF16)
    q_cat = _mm(qn, [w1, w2], [(cos_q, "i0"), (sin_q, "i0")], _epi_rope, BF16, 2176, HEAD_PAD, "mla_q")

    ckw = ck.shape[1]
    eye = jnp.eye(ROPE, dtype=F32)
    wk = jnp.zeros((ckw, N_HEADS, HEAD_PAD), F32)
    wk = wk.at[:KV_LORA, :, :NOPE].set(w_uk)
    wk = wk.at[KV_LORA:KV_LORA + ROPE, :, NOPE:NOPE + ROPE].set(jnp.broadcast_to(eye[:, None, :], (ROPE, N_HEADS, ROPE)))
    wv = jnp.zeros((ckw, N_HEADS, 2 * V_DIM), F32)
    wv = wv.at[:KV_LORA, :, :V_DIM].set(w_uv)
    wv = wv.at[KV_LORA + ROPE, :, V_DIM:].set(1.0)
    w_kv = jnp.concatenate([wk.reshape(ckw, N_HEADS * HEAD_PAD), wv.reshape(ckw, N_HEADS * 2 * V_DIM)],
                           axis=1).astype(BF16)
    kv_p = _mm(ck[:n_p], [w_kv], [], _epi_plain, BF16, 2048, 1024, "mla_kv_up")
    o_p = _attn_prompt(q_cat, kv_p, b_p, l_p)

    qs = q_cat[n_p:].reshape(n_s, N_HEADS, HEAD_PAD)
    q_nope_s = qs[:, :, :NOPE].reshape(n_s, N_HEADS * NOPE)
    q_pe_s = qs[:, :, NOPE:NOPE + ROPE].reshape(b_s, t_s * N_HEADS, ROPE)
    w_ukt = jnp.transpose(w_uk, (1, 2, 0)).astype(BF16)
    q_abs = _headmm(q_nope_s, w_ukt, BF16, "mla_q_absorb").reshape(b_s, t_s * N_HEADS, KV_LORA)
    self_c = jnp.zeros((b_s, PAGE, KV_LORA), F32).at[:, :t_s].set(c[n_p:].reshape(b_s, t_s, KV_LORA))
    self_kt = jnp.zeros((b_s, ROPE, PAGE), F32).at[:, :, :t_s].set(
        jnp.swapaxes(kpe[n_p:].reshape(b_s, t_s, ROPE), 1, 2))
    o_lat = _decode(q_abs, q_pe_s, self_c, self_kt, cache_lat, jnp.swapaxes(cache_kr, 2, 3), page_table, e)
    w_uvh = jnp.transpose(w_uv, (1, 0, 2)).astype(BF16)
    o_s = _headmm(o_lat.reshape(n_s, N_HEADS * KV_LORA), w_uvh, BF16, "mla_o_up")

    o_all = jnp.concatenate([o_p, o_s], axis=0)
    kw = a_out.shape[1]
    x = _mm([a_out, o_all], [[w_out[e, :kw].astype(BF16), w_out[e, kw:].astype(BF16)]], [(x, "ij")],
            functools.partial(_epi_resid, scale=1.0), F32, 1088, 512, "even_out")
    return x, v_rows, c, kpe


def _shift_rows(x, j):
    if j == 0:
        return x
    row = lax.broadcasted_iota(jnp.int32, x.shape, 0)
    return jnp.where(row >= j, pltpu.roll(x, j, 0), 0.0)


def _qkv_factor(y, j, tiles_per_part):
    nrm = lax.rsqrt(jnp.sum(y * y, axis=-1, keepdims=True) + EPS)
    part = j // tiles_per_part
    return jnp.where(part == 0, nrm * DK ** -0.5, jnp.where(part == 1, nrm, 1.0))


def _conv_c_prompt_body(b_ref, c_ref, x_ref, w_ref, y_ref, tail_ref):
    xg = c_ref[...] * x_ref[...]
    w = w_ref[...]
    nw = w.shape[0]
    conv = w[nw - 1:nw] * xg
    for j in range(1, nw):
        conv = conv + w[nw - 1 - j:nw - j] * _shift_rows(xg, j)
    y_ref[...] = (b_ref[...] * conv).astype(y_ref.dtype)
    n = xg.shape[0]
    tail_ref[...] = xg[n - 8:, :]


def _conv_c_prompt(z, w, n_seq, seq_len, d_c):
    tc = 256
    nb = d_c // tc
    return pl.pallas_call(
        _conv_c_prompt_body,
        grid=(n_seq, nb),
        in_specs=[pl.BlockSpec((seq_len, tc), lambda b, j: (b, j)),
                  pl.BlockSpec((seq_len, tc), lambda b, j: (b, nb + j)),
                  pl.BlockSpec((seq_len, tc), lambda b, j: (b, 2 * nb + j)),
                  pl.BlockSpec((w.shape[0], tc), lambda b, j: (0, j))],
        out_specs=[pl.BlockSpec((seq_len, tc), lambda b, j: (b, j)),
                   pl.BlockSpec((None, 8, tc), lambda b, j: (b, 0, j))],
        out_shape=[jax.ShapeDtypeStruct((n_seq * seq_len, d_c), BF16),
                   jax.ShapeDtypeStruct((n_seq, 8, d_c), F32)],
        compiler_params=_params(("parallel", "parallel"), 0),
        name="conv_c_prompt",
    )(z, z, z, w)


def _conv_d_prompt_body(x_ref, w_ref, y_ref, tail_ref, *, tiles_per_part):
    j = pl.program_id(1)
    x = x_ref[...]
    w = w_ref[...]
    nw = w.shape[0]
    conv = w[nw - 1:nw] * x
    for s in range(1, nw):
        conv = conv + w[nw - 1 - s:nw - s] * _shift_rows(x, s)
    y = _silu(conv)
    for hh in range(x.shape[1] // DK):
        seg = y[:, hh * DK:(hh + 1) * DK]
        y_ref[:, hh * DK:(hh + 1) * DK] = seg * _qkv_factor(seg, j, tiles_per_part)
    n = x.shape[0]
    tail_ref[...] = x[n - 8:, :]


def _conv_d_prompt(z, w, n_seq, seq_len, col0, d_qkv):
    tc = 2 * DK
    assert (d_qkv // 3) % tc == 0 and col0 % tc == 0
    nb = d_qkv // tc
    off = col0 // tc
    return pl.pallas_call(
        functools.partial(_conv_d_prompt_body, tiles_per_part=nb // 3),
        grid=(n_seq, nb),
        in_specs=[pl.BlockSpec((seq_len, tc), lambda b, j: (b, off + j)),
                  pl.BlockSpec((w.shape[0], tc), lambda b, j: (0, j))],
        out_specs=[pl.BlockSpec((seq_len, tc), lambda b, j: (b, j)),
                   pl.BlockSpec((None, 8, tc), lambda b, j: (b, 0, j))],
        out_shape=[jax.ShapeDtypeStruct((n_seq * seq_len, d_qkv), F32),
                   jax.ShapeDtypeStruct((n_seq, 8, d_qkv), F32)],
        compiler_params=_params(("parallel", "parallel"), 0),
        name="conv_d_prompt",
    )(z, w)


def _conv_c_sample_body(b_ref, c_ref, x_ref, buf_ref, w_ref, y_ref, nbuf_ref):
    t_new = x_ref.shape[0]
    w = w_ref[...]
    nw = w.shape[0]
    xp = [buf_ref[s] for s in range(nw - 1)] + [c_ref[t] * x_ref[t] for t in range(t_new)]
    for t in range(t_new):
        conv = w[0:1] * xp[t]
        for s in range(1, nw):
            conv = conv + w[s:s + 1] * xp[t + s]
        y_ref[t] = (b_ref[t] * conv).astype(y_ref.dtype)
    for s in range(nw - 1):
        nbuf_ref[s] = xp[t_new + s]


def _conv_c_sample(zt, buf_t, w, d_c):
    t_new, n_seq, _ = zt.shape
    tc = 512
    nb = d_c // tc
    nw = w.shape[0]
    return pl.pallas_call(
        _conv_c_sample_body,
        grid=(nb,),
        in_specs=[pl.BlockSpec((t_new, n_seq, tc), lambda j: (0, 0, j)),
                  pl.BlockSpec((t_new, n_seq, tc), lambda j: (0, 0, nb + j)),
                  pl.BlockSpec((t_new, n_seq, tc), lambda j: (0, 0, 2 * nb + j)),
                  pl.BlockSpec((nw - 1, n_seq, tc), lambda j: (0, 0, j)),
                  pl.BlockSpec((nw, tc), lambda j: (0, j))],
        out_specs=[pl.BlockSpec((t_new, n_seq, tc), lambda j: (0, 0, j)),
                   pl.BlockSpec((nw - 1, n_seq, tc), lambda j: (0, 0, j))],
        out_shape=[jax.ShapeDtypeStruct((t_new, n_seq, d_c), BF16),
                   jax.ShapeDtypeStruct((nw - 1, n_seq, d_c), F32)],
        compiler_params=_params(("parallel",), 0),
        name="conv_c_sample",
    )(zt, zt, zt, buf_t, w)


def _conv_d_sample_body(x_ref, buf_ref, w_ref, y_ref, nbuf_ref, *, tiles_per_part):
    j = pl.program_id(0)
    t_new = x_ref.shape[0]
    w = w_ref[...]
    nw = w.shape[0]
    xp = [buf_ref[s] for s in range(nw - 1)] + [x_ref[t] for t in range(t_new)]
    for t in range(t_new):
        conv = w[0:1] * xp[t]
        for s in range(1, nw):
            conv = conv + w[s:s + 1] * xp[t + s]
        y = _silu(conv)
        y_ref[t] = y * _qkv_factor(y, j, tiles_per_part)
    for s in range(nw - 1):
        nbuf_ref[s] = xp[t_new + s]


def _conv_d_sample(zt, buf_t, w, col0, d_qkv):
    t_new, n_seq, _ = zt.shape
    tc = DK
    nb = d_qkv // tc
    off = col0 // tc
    nw = w.shape[0]
    return pl.pallas_call(
        functools.partial(_conv_d_sample_body, tiles_per_part=nb // 3),
        grid=(nb,),
        in_specs=[pl.BlockSpec((t_new, n_seq, tc), lambda j: (0, 0, off + j)),
                  pl.BlockSpec((nw - 1, n_seq, tc), lambda j: (0, 0, j)),
                  pl.BlockSpec((nw, tc), lambda j: (0, j))],
        out_specs=[pl.BlockSpec((t_new, n_seq, tc), lambda j: (0, 0, j)),
                   pl.BlockSpec((nw - 1, n_seq, tc), lambda j: (0, 0, j))],
        out_shape=[jax.ShapeDtypeStruct((t_new, n_seq, d_qkv), F32),
                   jax.ShapeDtypeStruct((nw - 1, n_seq, d_qkv), F32)],
        compiler_params=_params(("parallel",), 0),
        name="conv_d_sample",
    )(zt, buf_t, w)


def _cumsum_rows(x):
    n = x.shape[0]
    row = lax.broadcasted_iota(jnp.int32, x.shape, 0)
    s = 1
    while s < n:
        x = x + jnp.where(row >= s, pltpu.roll(x, s, 0), 0.0)
        s *= 2
    return x


def _split2(x):
    hi = x.astype(BF16)
    return hi, (x - hi.astype(F32)).astype(BF16)


def _hp_dup(a_parts, b_parts):
    a_hi, a_lo = a_parts
    b_hi, b_lo = b_parts
    lhs = jnp.concatenate([a_hi, a_lo], axis=1)
    rhs = jnp.concatenate([b_hi, b_lo, b_hi, jnp.zeros_like(b_hi)], axis=0)
    return _dot(lhs, rhs)


def _delta_body(q_ref, k_ref, v_ref, zg_ref, gate_ref, alog_ref, dt_ref, on_ref, s0_ref,
                o_ref, sout_ref, s_ref, *, chunk, n_valid, group):
    c = pl.program_id(1)
    nc = pl.num_programs(1)
    dup = 2 * chunk == 128
    width = 2 * chunk if dup else chunk

    @pl.when(c == 0)
    def _():
        s_ref[...] = s0_ref[...]

    gate = gate_ref[...]
    beta_all = _sigmoid(gate)
    x = gate + dt_ref[...]
    softplus = jnp.maximum(x, 0.0) + jnp.log(1.0 + jnp.exp(-jnp.abs(x)))
    g_all = -jnp.exp(alog_ref[...]) * softplus
    row128 = lax.broadcasted_iota(jnp.int32, (chunk, 128), 0)
    g_all = jnp.where(row128 < n_valid, g_all, 0.0)
    gcum = _cumsum_rows(g_all)
    eye = (lax.broadcasted_iota(jnp.int32, (128, 128), 0)
           == lax.broadcasted_iota(jnp.int32, (128, 128), 1)).astype(BF16)
    g_rows = jnp.concatenate([gcum, gcum], axis=0) if dup else gcum
    g_hi, g_mid, g_lo = _split3(g_rows)
    gcum_t = _dot_nt(eye, g_hi) + (_dot_nt(eye, g_mid) + _dot_nt(eye, g_lo))

    ri = lax.broadcasted_iota(jnp.int32, (chunk, width), 0)
    ci = lax.broadcasted_iota(jnp.int32, (chunk, width), 1)
    ci = jnp.where(ci >= chunk, ci - chunk, ci)
    incl = ri >= ci
    strict = ri > ci
    ident = (ri == ci).astype(F32)
    on = on_ref[...]
    nil = 1
    while nil < n_valid:
        nil *= 2
    nil = min(nil, chunk)

    for g0 in range(0, N_HEADS, group):
        heads = list(range(g0, g0 + group))
        pw, tm, qk = {}, {}, {}
        for h in heads:
            hs = slice(h * DK, (h + 1) * DK)
            q = q_ref[:, hs]
            k = k_ref[:, hs]
            beta = beta_all[:, h:h + 1]
            gc = gcum[:, N_HEADS + h:N_HEADS + h + 1]
            gr = gcum_t[N_HEADS + h:N_HEADS + h + 1, :]
            decay = jnp.exp(jnp.where(incl, gc - gr, -jnp.inf))
            k_b = k.astype(BF16)
            rhs = jnp.concatenate([k_b, k_b], axis=0) if dup else k_b
            lhs = jnp.concatenate([k * beta, q], axis=0).astype(BF16)
            r = _dot_nt(lhs, rhs)
            a = jnp.where(strict, r[:chunk] * decay, 0.0)
            qk[h] = jnp.where(incl[:, :chunk], r[chunk:, :chunk] * decay[:, :chunk], 0.0)
            pw[h] = -a
            tm[h] = ident + pw[h]
        p = 1
        while 2 * p < nil:
            for h in heads:
                if dup:
                    p2 = _split2(pw[h])
                    pw[h] = _hp_dup(p2, p2)
                else:
                    pw[h] = _dot_hp(pw[h], pw[h])
            for h in heads:
                if dup:
                    tm[h] = tm[h] + _hp_dup(_split2(tm[h]), _split2(pw[h]))
                else:
                    tm[h] = tm[h] + _dot_hp(tm[h], pw[h])
            p *= 2
        uw = {}
        for h in heads:
            hs = slice(h * DK, (h + 1) * DK)
            beta = beta_all[:, h:h + 1]
            gc = gcum[:, N_HEADS + h:N_HEADS + h + 1]
            kb = k_ref[:, hs] * beta
            rhs = jnp.concatenate([v_ref[:, hs] * beta, kb * jnp.exp(gc)], axis=1).astype(BF16)
            uw[h] = _dot(tm[h][:, :chunk].astype(BF16), rhs)
        ws = {}
        for h in heads:
            hs = slice(h * DK, (h + 1) * DK)
            gc = gcum[:, N_HEADS + h:N_HEADS + h + 1]
            lhs = jnp.concatenate([uw[h][:, DV:], q_ref[:, hs] * jnp.exp(gc)], axis=0).astype(BF16)
            ws[h] = _dot(lhs, s_ref[h].astype(BF16))
        for h in heads:
            hs = slice(h * DK, (h + 1) * DK)
            gc = gcum[:, N_HEADS + h:N_HEADS + h + 1]
            g_last = gcum[chunk - 1:chunk, N_HEADS + h:N_HEADS + h + 1]
            v_new = (uw[h][:, :DV] - ws[h][:chunk]).astype(BF16)
            o = ws[h][chunk:] + _dot(qk[h].astype(BF16), v_new)
            k_dec = k_ref[:, hs] * jnp.exp(g_last - gc)
            s_ref[h] = s_ref[h] * jnp.exp(g_last) + _dot_tn(k_dec.astype(BF16), v_new)
            o = o * lax.rsqrt(jnp.mean(o * o, axis=-1, keepdims=True) + EPS) * on
            o_ref[:, hs] = (o * _silu(zg_ref[:, hs])).astype(o_ref.dtype)

    @pl.when(c == nc - 1)
    def _():
        sout_ref[...] = s_ref[...]


def _delta(qkv, zg, zg_blk, gate, a_log, dt_bias, o_norm, s0, n_seq, chunk, n_valid):
    rows = qkv.shape[0]
    nc = rows // (n_seq * chunk)
    hw = N_HEADS * DK

    def rmap(blk):
        return lambda s, c: (s * nc + c, blk)

    alog = jnp.zeros((1, 128), F32).at[0, N_HEADS:2 * N_HEADS].set(a_log)
    dtb = jnp.zeros((1, 128), F32).at[0, N_HEADS:2 * N_HEADS].set(dt_bias)
    fix = lambda s, c: (0, 0)
    return pl.pallas_call(
        functools.partial(_delta_body, chunk=chunk, n_valid=n_valid, group=DELTA_GROUP),
        grid=(n_seq, nc),
        in_specs=[pl.BlockSpec((chunk, hw), rmap(0)),
                  pl.BlockSpec((chunk, hw), rmap(1)),
                  pl.BlockSpec((chunk, hw), rmap(2)),
                  pl.BlockSpec((chunk, hw), rmap(zg_blk)),
                  pl.BlockSpec((chunk, 128), rmap(0)),
                  pl.BlockSpec((1, 128), fix),
                  pl.BlockSpec((1, 128), fix),
                  pl.BlockSpec((1, DV), fix),
                  pl.BlockSpec((None, N_HEADS, DK, DV), lambda s, c: (s, 0, 0, 0))],
        out_specs=[pl.BlockSpec((chunk, hw), rmap(0)),
                   pl.BlockSpec((None, N_HEADS, DK, DV), lambda s, c: (s, 0, 0, 0))],
        out_shape=[jax.ShapeDtypeStruct((rows, hw), BF16),
                   jax.ShapeDtypeStruct((n_seq, N_HEADS, DK, DV), F32)],
        scratch_shapes=[pltpu.VMEM((N_HEADS, DK, DV), F32)],
        compiler_params=_params(("parallel", "arbitrary"), 0),
        name="gated_delta",
    )(qkv, qkv, qkv, zg, gate, alog, dtb, o_norm.reshape(1, DV), s0)


def _odd_mixer(x, dims, buf_c, buf_d, s0_s, w_in, conv_c_w, conv_d_w, a_log, dt_bias, o_norm,
               w_out, layer_idx, mix_norm):
    n_p, b_p, l_p, b_s, t_s = dims
    d = x.shape[1]
    d_c = conv_c_w.shape[1]
    d_qkv = conv_d_w.shape[1]
    n_in = w_in.shape[2]
    o_qkv = 3 * d_c
    o_gate = o_qkv + d_qkv
    o_tail = o_gate + N_HEADS * DV
    n_tail = n_in - o_tail
    assert o_tail % 128 == 0 and n_tail == 2 * N_HEADS and o_gate % (N_HEADS * DV) == 0

    h = _rms(x, mix_norm, BF16)
    z = _mmw([(h, 0, d)], [[(jnp.swapaxes(w_in, 1, 2), layer_idx, 0)]], [], _epi_plain, F32, 1088, 512,
             "odd_in", w_t=True)

    y_c_p, tail_c_p = _conv_c_prompt(z, conv_c_w, b_p, l_p, d_c)
    qkv_p, tail_d_p = _conv_d_prompt(z, conv_d_w, b_p, l_p, o_qkv, d_qkv)
    chunk_p = math.gcd(l_p, DN_CHUNK)
    gate_p = jnp.pad(z[:n_p, o_tail:], ((0, 0), (0, 128 - n_tail)))
    s0_p = jnp.zeros((b_p, N_HEADS, DK, DV), F32)
    o_p, st_p = _delta(qkv_p, z, o_gate // (N_HEADS * DV), gate_p, a_log, dt_bias, o_norm, s0_p,
                       b_p, chunk_p, chunk_p)

    zt = jnp.transpose(z[n_p:].reshape(b_s, t_s, n_in), (1, 0, 2))
    y_c_t, nbuf_c_t = _conv_c_sample(zt, jnp.transpose(buf_c, (1, 0, 2)), conv_c_w, d_c)
    qkv_t, nbuf_d_t = _conv_d_sample(zt, jnp.transpose(buf_d, (1, 0, 2)), conv_d_w, o_qkv, d_qkv)
    pad_t = SAMPLE_CHUNK - t_s
    assert pad_t >= 0

    def to_seq(a_t):
        a = jnp.transpose(a_t, (1, 0, 2))
        a = jnp.pad(a, ((0, 0), (0, pad_t), (0, 0)))
        return a.reshape(b_s * SAMPLE_CHUNK, a.shape[2])

    qkv_s = to_seq(qkv_t)
    zs = z[n_p:].reshape(b_s, t_s, n_in)
    zg_s = jnp.pad(zs[:, :, o_gate:o_tail], ((0, 0), (0, pad_t), (0, 0))).reshape(b_s * SAMPLE_CHUNK, -1)
    gate_s = jnp.pad(zs[:, :, o_tail:], ((0, 0), (0, pad_t), (0, 128 - n_tail))).reshape(b_s * SAMPLE_CHUNK, 128)
    o_s8, st_s = _delta(qkv_s, zg_s, 0, gate_s, a_log, dt_bias, o_norm, s0_s, b_s, SAMPLE_CHUNK, t_s)
    o_s = o_s8.reshape(b_s, SAMPLE_CHUNK, -1)[:, :t_s].reshape(b_s * t_s, -1)
    y_c_s = jnp.transpose(y_c_t, (1, 0, 2)).reshape(b_s * t_s, d_c)

    y_c = jnp.concatenate([y_c_p, y_c_s], axis=0)
    o_all = jnp.concatenate([o_p, o_s], axis=0)
    kw = y_c.shape[1]
    x = _mm([y_c, o_all], [[w_out[layer_idx, :kw].astype(BF16), w_out[layer_idx, kw:].astype(BF16)]],
            [(x, "ij")], functools.partial(_epi_resid, scale=1.0), F32, 1088, 512, "odd_out")
    nw_c = conv_c_w.shape[0] - 1
    nw_d = conv_d_w.shape[0] - 1
    outs = (tail_c_p[:, 8 - nw_c:], jnp.transpose(nbuf_c_t, (1, 0, 2)),
            tail_d_p[:, 8 - nw_d:], jnp.transpose(nbuf_d_t, (1, 0, 2)), st_p, st_s)
    return x, outs


def _rope_tables(pos):
    half = ROPE // 2
    inv = ROPE_THETA ** (-jnp.arange(half, dtype=F32) / half)
    ang = pos[:, None] * inv[None, :]
    cos = jnp.cos(ang)
    sin = jnp.sin(ang)
    n = pos.shape[0]
    cos_f = jnp.concatenate([cos, cos], axis=1)
    sin_f = jnp.concatenate([-sin, sin], axis=1)
    z = jnp.zeros((n, 128 - ROPE), F32)
    cos_k = jnp.concatenate([cos_f, z], axis=1)
    sin_k = jnp.concatenate([sin_f, z], axis=1)
    cos_q = jnp.concatenate([jnp.ones((n, NOPE), F32), cos_f, z], axis=1)
    sin_q = jnp.concatenate([jnp.zeros((n, NOPE), F32), sin_f, z], axis=1)
    return cos_k, sin_k, cos_q, sin_q


def kernel(x_prompt, x_sample, cache_mla_latent, cache_mla_krope, state_conv_c, state_conv_d, state_delta, page_table, ffn1_norm, ffn1_w_gate, ffn1_w_up, ffn1_w_down, mix_norm, ffn2_norm, ffn2_w_gate, ffn2_w_up, ffn2_w_down, even_w_in, gmlp_v_norm, gmlp_ws, gmlp_bs, mla_q_norm, mla_w_uq, mla_kv_norm, mla_w_uk, mla_w_uv, even_w_out, odd_w_in, conv_c_w, conv_d_w, delta_a_log, delta_dt_bias, delta_o_norm, odd_w_out, final_norm):
    b_p, l_p, d = x_prompt.shape
    b_s, t_s, _ = x_sample.shape
    n_p = b_p * l_p
    n_s = b_s * t_s
    depth = ffn1_norm.shape[0]
    dims = (n_p, b_p, l_p, b_s, t_s)
    past_len = page_table.shape[1] * PAGE
    pos = jnp.concatenate([jnp.tile(jnp.arange(l_p, dtype=F32), b_p),
                           jnp.tile(jnp.arange(t_s, dtype=F32) + past_len, b_s)])
    tabs = _rope_tables(pos)

    x = jnp.concatenate([x_prompt.reshape(n_p, d), x_sample.reshape(n_s, d)], axis=0)
    lat, kr, vrow = [], [], []
    cc_p, cc_s, cd_p, cd_s, sd_p, sd_s = [], [], [], [], [], []
    for layer in range(depth):
        x = _ffn_half(x, ffn1_norm[layer], ffn1_w_gate, ffn1_w_up, ffn1_w_down, layer)
        if layer % 2 == 0:
            e = layer // 2
            x, v_rows, c, kpe = _even_mixer(
                x, dims, tabs, cache_mla_latent, cache_mla_krope, page_table, e, even_w_in,
                gmlp_v_norm[e], gmlp_ws[e], gmlp_bs[e], mla_q_norm[e], mla_w_uq[e], mla_kv_norm[e],
                mla_w_uk[e], mla_w_uv[e], even_w_out, mix_norm[layer])
            lat.append(c)
            kr.append(kpe)
            vrow.append(v_rows[n_p:])
        else:
            o = layer // 2
            x, outs = _odd_mixer(
                x, dims, state_conv_c[o], state_conv_d[o], state_delta[o], odd_w_in, conv_c_w[o],
                conv_d_w[o], delta_a_log[o], delta_dt_bias[o], delta_o_norm[o], odd_w_out, o,
                mix_norm[layer])
            cc_p.append(outs[0]); cc_s.append(outs[1]); cd_p.append(outs[2]); cd_s.append(outs[3])
            sd_p.append(outs[4]); sd_s.append(outs[5])
        x = _ffn_half(x, ffn2_norm[layer], ffn2_w_gate, ffn2_w_up, ffn2_w_down, layer)

    y_p = _rms(x, final_norm, F32, 0, n_p)
    y_s = _rms(x, final_norm, F32, n_p, n_s)
    lat = jnp.stack(lat)
    kr = jnp.stack(kr)
    n_e = lat.shape[0]
    return (y_p.reshape(b_p, l_p, d), y_s.reshape(b_s, t_s, d),
            lat[:, :n_p].reshape(n_e, b_p, l_p, -1), kr[:, :n_p].reshape(n_e, b_p, l_p, -1),
            lat[:, n_p:].reshape(n_e, b_s, t_s, -1), kr[:, n_p:].reshape(n_e, b_s, t_s, -1),
            jnp.stack(vrow).reshape(n_e, b_s, t_s, -1),
            jnp.stack(cc_p), jnp.stack(cc_s), jnp.stack(cd_p), jnp.stack(cd_s),
            jnp.stack(sd_p), jnp.stack(sd_s))
```

```python
import functools
import math

import jax
import jax.numpy as jnp
from jax import lax
from jax.experimental import pallas as pl
from jax.experimental.pallas import tpu as pltpu

F32 = jnp.float32
BF16 = jnp.bfloat16
EPS = 1e-6

PAGE = 128
GROUP = 128
N_HEADS = 16
NOPE = 128
ROPE = 64
HEAD_PAD = 256
KV_LORA = 512
V_DIM = 128
DK = 128
DV = 128
ROPE_THETA = 10000.0
ATT_SCALE = (NOPE + ROPE) ** -0.5
DN_CHUNK = 64
SAMPLE_CHUNK = 8
PAGES_PER_STEP = 64
PAGES_PER_GROUP = 16
PREP_BLK = 256
DELTA_GROUP = 16

VMEM_CAP_BYTES = 60 * 1024 * 1024


def _pick(n, target, mult=8):
    for d in range(min(n, target), 0, -1):
        if n % d == 0 and d % mult == 0:
            return d
    return n


def _params(sem, vmem_bytes):
    return pltpu.CompilerParams(
        dimension_semantics=sem,
        vmem_limit_bytes=int(min(VMEM_CAP_BYTES, max(vmem_bytes, 32 * 1024 * 1024))))


def _sigmoid(x):
    return 1.0 / (1.0 + jnp.exp(-x))


def _silu(x):
    return x * _sigmoid(x)


def _dot(a, b):
    return jnp.dot(a, b, preferred_element_type=F32)


def _dot_nt(a, b):
    return lax.dot_general(a, b, (((1,), (1,)), ((), ())), preferred_element_type=F32)


def _dot_tn(a, b):
    return lax.dot_general(a, b, (((0,), (0,)), ((), ())), preferred_element_type=F32)


def _split3(x):
    hi = x.astype(BF16)
    r = x - hi.astype(F32)
    mid = r.astype(BF16)
    lo = (r - mid.astype(F32)).astype(BF16)
    return hi, mid, lo


def _dot_hp(a, b):
    a_hi, a_mid, _ = _split3(a)
    b_hi, b_mid, _ = _split3(b)
    return _dot(a_hi, b_hi) + (_dot(a_hi, b_mid) + _dot(a_mid, b_hi))


def _rms_body(x_ref, g_ref, o_ref):
    x = x_ref[...]
    y = x * lax.rsqrt(jnp.mean(x * x, axis=-1, keepdims=True) + EPS)
    o_ref[...] = (y * g_ref[...]).astype(o_ref.dtype)


def _rms(x, g, out_dtype, row0=0, rows=None):
    d = x.shape[1]
    m = x.shape[0] if rows is None else rows
    tm = _pick(math.gcd(m, row0) if row0 else m, 256, 16)
    blk0 = row0 // tm
    return pl.pallas_call(
        _rms_body,
        grid=(m // tm,),
        in_specs=[pl.BlockSpec((tm, d), lambda i: (blk0 + i, 0)),
                  pl.BlockSpec((1, d), lambda i: (0, 0))],
        out_specs=pl.BlockSpec((tm, d), lambda i: (i, 0)),
        out_shape=jax.ShapeDtypeStruct((m, d), out_dtype),
        compiler_params=_params(("parallel",), 6 * tm * d * 4),
        name="rmsnorm",
    )(x, g.reshape(1, d))


def _mm_body(*refs, n_w, n_lhs, n_x, epi):
    lhs = [refs[l][...] for l in range(n_lhs)]
    accs = []
    for k in range(n_w):
        acc = _dot(lhs[0], refs[n_lhs + k * n_lhs][...])
        for l in range(1, n_lhs):
            acc = acc + _dot(lhs[l], refs[n_lhs + k * n_lhs + l][...])
        accs.append(acc)
    base = n_lhs + n_w * n_lhs
    xs = [refs[base + k][...] for k in range(n_x)]
    o_ref = refs[base + n_x]
    o_ref[...] = epi(accs, xs).astype(o_ref.dtype)


def _epi_plain(accs, xs):
    return accs[0]


def _epi_swiglu(accs, xs):
    return _silu(accs[0]) * accs[1]


def _epi_resid(accs, xs, scale):
    return xs[0] + scale * accs[0]


def _epi_rope(accs, xs):
    return accs[0] * xs[0] + accs[1] * xs[1]


def _mm(lhs, ws, extras, epi, out_dtype, tm_target, tn, name):
    if not isinstance(lhs, (list, tuple)):
        lhs, ws = [lhs], [[w] for w in ws]
    m = lhs[0].shape[0]
    n = ws[0][0].shape[1]
    tm = _pick(m, tm_target, 16)
    assert n % tn == 0, (n, tn)
    ktot = sum(a.shape[1] for a in lhs)
    in_specs = [pl.BlockSpec((tm, a.shape[1]), lambda i, j: (i, 0)) for a in lhs]
    for wl in ws:
        in_specs += [pl.BlockSpec((w.shape[0], tn), lambda i, j: (0, j)) for w in wl]
    for _, kind in extras:
        if kind == "ij":
            in_specs.append(pl.BlockSpec((tm, tn), lambda i, j: (i, j)))
        else:
            in_specs.append(pl.BlockSpec((tm, tn), lambda i, j: (i, 0)))
    vmem = 2 * (tm * ktot * 2 + len(ws) * ktot * tn * 2 + (len(extras) + 1) * tm * tn * 4)
    vmem += (len(ws) + 1) * tm * tn * 4 + (4 << 20)
    return pl.pallas_call(
        functools.partial(_mm_body, n_w=len(ws), n_lhs=len(lhs), n_x=len(extras), epi=epi),
        grid=(m // tm, n // tn),
        in_specs=in_specs,
        out_specs=pl.BlockSpec((tm, tn), lambda i, j: (i, j)),
        out_shape=jax.ShapeDtypeStruct((m, n), out_dtype),
        compiler_params=_params(("parallel", "arbitrary"), vmem),
        name=name,
    )(*lhs, *[w for wl in ws for w in wl], *[a for a, _ in extras])


def _headmm_body(l_ref, w_ref, o_ref):
    o_ref[...] = _dot(l_ref[...], w_ref[...]).astype(o_ref.dtype)


def _headmm(lhs, w, out_dtype, name):
    m = lhs.shape[0]
    nh, kd, nd = w.shape
    return pl.pallas_call(
        _headmm_body,
        grid=(nh,),
        in_specs=[pl.BlockSpec((m, kd), lambda h: (0, h)),
                  pl.BlockSpec((None, kd, nd), lambda h: (h, 0, 0))],
        out_specs=pl.BlockSpec((m, nd), lambda h: (0, h)),
        out_shape=jax.ShapeDtypeStruct((m, nh * nd), out_dtype),
        compiler_params=_params(("parallel",), 0),
        name=name,
    )(lhs, w)


def _mmw_body(*refs, n_acc, n_lhs, n_x, epi, w_t):
    n_w = n_acc * n_lhs
    lhs_refs = refs[:n_lhs]
    w_refs = refs[n_lhs:n_lhs + n_w]
    x_refs = refs[n_lhs + n_w:n_lhs + n_w + n_x]
    o_ref = refs[n_lhs + n_w + n_x]
    wb_refs = refs[n_lhs + n_w + n_x + 1:]

    @pl.when(pl.program_id(1) == 0)
    def _():
        for w_ref, wb_ref in zip(w_refs, wb_refs):
            wb_ref[...] = w_ref[...].astype(BF16)

    dot = _dot_nt if w_t else _dot
    lhs = [r[...] for r in lhs_refs]
    accs = []
    for a in range(n_acc):
        acc = dot(lhs[0], wb_refs[a * n_lhs][...])
        for l in range(1, n_lhs):
            acc = acc + dot(lhs[l], wb_refs[a * n_lhs + l][...])
        accs.append(acc)
    o_ref[...] = epi(accs, [r[...] for r in x_refs]).astype(o_ref.dtype)


def _mmw(lhs_list, w_list, extras, epi, out_dtype, tm_target, tn, name, w_t=False):
    m = lhs_list[0][0].shape[0]
    n = w_list[0][0][0].shape[1 if w_t else 2]
    tm = _pick(m, tm_target, 16)
    n_lhs = len(lhs_list)
    in_specs, args = [], []
    vmem = 0
    for arr, cb, k in lhs_list:
        in_specs.append(pl.BlockSpec((tm, k), lambda j, i, cb=cb: (i, cb)))
        args.append(arr)
        vmem += 2 * tm * k * arr.dtype.itemsize
    scratch = []
    for ws in w_list:
        assert len(ws) == n_lhs
        for (arr, layer, rb), (_, _, k) in zip(ws, lhs_list):
            if w_t:
                in_specs.append(pl.BlockSpec((None, tn, k), lambda j, i, layer=layer, rb=rb: (layer, j, rb)))
                scratch.append(pltpu.VMEM((tn, k), BF16))
            else:
                in_specs.append(pl.BlockSpec((None, k, tn), lambda j, i, layer=layer, rb=rb: (layer, rb, j)))
                scratch.append(pltpu.VMEM((k, tn), BF16))
            args.append(arr)
            vmem += 2 * k * tn * 4 + k * tn * 2
    for arr, kind in extras:
        if kind == "ij":
            in_specs.append(pl.BlockSpec((tm, tn), lambda j, i: (i, j)))
        else:
            in_specs.append(pl.BlockSpec((tm, tn), lambda j, i: (i, 0)))
        args.append(arr)
    vmem += (2 * (len(extras) + 1) + len(w_list) + 1) * tm * tn * 4 + (4 << 20)
    return pl.pallas_call(
        functools.partial(_mmw_body, n_acc=len(w_list), n_lhs=n_lhs, n_x=len(extras), epi=epi, w_t=w_t),
        grid=(pl.cdiv(n, tn), m // tm),
        in_specs=in_specs,
        out_specs=pl.BlockSpec((tm, tn), lambda j, i: (i, j)),
        out_shape=jax.ShapeDtypeStruct((m, n), out_dtype),
        scratch_shapes=scratch,
        compiler_params=_params(("parallel", "arbitrary"), vmem),
        name=name,
    )(*args)


def _ffn_half(x, g, wg, wu, wd, layer):
    d, f = wg.shape[1:]
    assert f % 256 == 0
    kh = f // 2
    assert kh % 128 == 0
    h = _rms(x, g, BF16)
    a = _mmw([(h, 0, d)], [[(wg, layer, 0)], [(wu, layer, 0)]], [], _epi_swiglu, BF16, 1088, 256,
             "ffn_gate_up")
    half = functools.partial(_epi_resid, scale=0.5)
    x = _mmw([(a, 0, kh)], [[(wd, layer, 0)]], [(x, "ij")], half, F32, 544, 512, "ffn_down_lo")
    return _mmw([(a, 1, kh)], [[(wd, layer, 1)]], [(x, "ij")], half, F32, 544, 512, "ffn_down_hi")


def _gmlp_body(z_ref, w_ref, b_ref, vn_ref, a_ref, v_ref, *, n_prompt_tiles, t_sample, d_a):
    i = pl.program_id(0)
    z = z_ref[...]
    c0 = math.sqrt(2.0 / math.pi)
    a = 0.5 * z * (1.0 + jnp.tanh(c0 * (z + 0.044715 * (z * z * z))))
    u = a[:, :d_a]
    v = a[:, d_a:]
    v = v * lax.rsqrt(jnp.mean(v * v, axis=-1, keepdims=True) + EPS) * vn_ref[...]
    v_ref[...] = v
    row = lax.broadcasted_iota(jnp.int32, (GROUP, GROUP), 0)
    col = lax.broadcasted_iota(jnp.int32, (GROUP, GROUP), 1)
    same_seq = (row // t_sample) == (col // t_sample)
    mask = (col <= row) & (same_seq | (i < n_prompt_tiles))
    for g in range(d_a // GROUP):
        sl = slice(g * GROUP, (g + 1) * GROUP)
        w = jnp.where(mask, w_ref[g], 0.0).astype(BF16)
        s = _dot(w, v[:, sl].astype(BF16)) + b_ref[:, sl]
        a_ref[:, sl] = (u[:, sl] * s).astype(a_ref.dtype)


def _gmlp(z_a, ws, bs, v_norm, n_prompt, t_sample, d_a):
    m = z_a.shape[0]
    ng = d_a // GROUP
    assert n_prompt % GROUP == 0 and (m - n_prompt) % GROUP == 0 and GROUP % t_sample == 0
    npt = n_prompt // GROUP
    rep = GROUP // t_sample
    w_all = jnp.stack([ws, jnp.tile(ws[:, :t_sample, :t_sample], (1, rep, rep))])
    b_p = jnp.repeat(bs.T, GROUP, axis=1)
    b_s = jnp.repeat(jnp.tile(bs[:, :t_sample].T, (rep, 1)), GROUP, axis=1)
    b_all = jnp.stack([b_p, b_s])

    def sel(i):
        return jnp.where(i < npt, 0, 1)

    return pl.pallas_call(
        functools.partial(_gmlp_body, n_prompt_tiles=npt, t_sample=t_sample, d_a=d_a),
        grid=(m // GROUP,),
        in_specs=[pl.BlockSpec((GROUP, 2 * d_a), lambda i: (i, 0)),
                  pl.BlockSpec((None, ng, GROUP, GROUP), lambda i: (sel(i), 0, 0, 0)),
                  pl.BlockSpec((None, GROUP, d_a), lambda i: (sel(i), 0, 0)),
                  pl.BlockSpec((1, d_a), lambda i: (0, 0))],
        out_specs=[pl.BlockSpec((GROUP, d_a), lambda i: (i, 0)),
                   pl.BlockSpec((GROUP, d_a), lambda i: (i, 0))],
        out_shape=[jax.ShapeDtypeStruct((m, d_a), BF16),
                   jax.ShapeDtypeStruct((m, d_a), F32)],
        compiler_params=_params(("parallel",), 0),
        name="gmlp",
    )(z_a, w_all, b_all, v_norm.reshape(1, d_a))


def _mla_prep_body(*refs, q_lora, n_blk):
    z_refs = refs[:n_blk]
    qg_ref, kg_ref, cos_ref, sin_ref, qn_ref, c_ref, kpe_ref, ck_ref = refs[n_blk:]
    nq = q_lora // PREP_BLK
    nkv = KV_LORA // PREP_BLK
    zq = jnp.concatenate([z_refs[b][...] for b in range(nq)], axis=1)
    qn_ref[...] = (zq * lax.rsqrt(jnp.mean(zq * zq, axis=-1, keepdims=True) + EPS)
                   * qg_ref[...]).astype(qn_ref.dtype)
    zkv = jnp.concatenate([z_refs[nq + b][...] for b in range(nkv)], axis=1)
    c = zkv * lax.rsqrt(jnp.mean(zkv * zkv, axis=-1, keepdims=True) + EPS) * kg_ref[...]
    c_ref[...] = c
    zr = z_refs[nq + nkv][:, :128]
    lane = lax.broadcasted_iota(jnp.int32, zr.shape, 1)
    zr = jnp.where(lane < ROPE, zr, 0.0)
    src = lax.broadcasted_iota(jnp.int32, (128, 128), 0)
    dst = lax.broadcasted_iota(jnp.int32, (128, 128), 1)
    half = ROPE // 2
    partner = jnp.where(dst < half, dst + half, dst - half)
    perm = ((src == partner) & (dst < ROPE)).astype(BF16)
    r_hi, r_mid, r_lo = _split3(zr)
    zrp = _dot(r_hi, perm) + (_dot(r_mid, perm) + _dot(r_lo, perm))
    kpe = zr * cos_ref[...] + zrp * sin_ref[...]
    kpe_ref[...] = kpe[:, :ROPE]
    ck_ref[:, :KV_LORA] = c.astype(ck_ref.dtype)
    one = (lax.broadcasted_iota(jnp.int32, kpe.shape, 1) == ROPE).astype(F32)
    ck_ref[:, KV_LORA:] = (kpe + one).astype(ck_ref.dtype)


def _mla_prep(z, col0, q_norm, kv_norm, cos128, sin128, q_lora):
    m = z.shape[0]
    tm = _pick(m, 544, 16)
    assert col0 % PREP_BLK == 0 and q_lora % PREP_BLK == 0 and KV_LORA % PREP_BLK == 0
    blk0 = col0 // PREP_BLK
    n_blk = (q_lora + KV_LORA) // PREP_BLK + 1
    row = lambda i: (i, 0)
    fix = lambda i: (0, 0)
    return pl.pallas_call(
        functools.partial(_mla_prep_body, q_lora=q_lora, n_blk=n_blk),
        grid=(m // tm,),
        in_specs=[pl.BlockSpec((tm, PREP_BLK), lambda i, b=b: (i, blk0 + b)) for b in range(n_blk)] + [
                  pl.BlockSpec((1, q_lora), fix),
                  pl.BlockSpec((1, KV_LORA), fix),
                  pl.BlockSpec((tm, 128), row),
                  pl.BlockSpec((tm, 128), row)],
        out_specs=[pl.BlockSpec((tm, q_lora), row),
                   pl.BlockSpec((tm, KV_LORA), row),
                   pl.BlockSpec((tm, ROPE), row),
                   pl.BlockSpec((tm, KV_LORA + 128), row)],
        out_shape=[jax.ShapeDtypeStruct((m, q_lora), BF16),
                   jax.ShapeDtypeStruct((m, KV_LORA), F32),
                   jax.ShapeDtypeStruct((m, ROPE), F32),
                   jax.ShapeDtypeStruct((m, KV_LORA + 128), BF16)],
        compiler_params=_params(("parallel",), 0),
        name="mla_prep",
    )(*([z] * n_blk), q_norm.reshape(1, q_lora), kv_norm.reshape(1, KV_LORA), cos128, sin128)


def _attn_prompt_body(qi_ref, ki_ref, q_ref, k_ref, v_ref, o_ref, m_ref, l_ref, acc_ref, *, tq, tk):
    qi = qi_ref[pl.program_id(1)]
    ki = ki_ref[pl.program_id(1)]

    @pl.when(ki == 0)
    def _():
        m_ref[...] = jnp.full(m_ref.shape, -jnp.inf, F32)
        l_ref[...] = jnp.zeros(l_ref.shape, F32)
        acc_ref[...] = jnp.zeros(acc_ref.shape, F32)

    def scores(h):
        qs = slice(h * HEAD_PAD, (h + 1) * HEAD_PAD)
        return _dot_nt(q_ref[:, qs], k_ref[:, qs]) * ATT_SCALE

    def process(diagonal):
        if diagonal:
            mask = (lax.broadcasted_iota(jnp.int32, (tq, tk), 1)
                    <= lax.broadcasted_iota(jnp.int32, (tq, tk), 0))
        s_next = scores(0)
        for h in range(N_HEADS):
            vs = slice(h * V_DIM, (h + 1) * V_DIM)
            s = s_next
            if h + 1 < N_HEADS:
                s_next = scores(h + 1)
            if diagonal:
                s = jnp.where(mask, s, -jnp.inf)
            m_old = m_ref[h]
            m_new = jnp.maximum(m_old, jnp.max(s, axis=-1, keepdims=True))
            corr = jnp.exp(m_old - m_new)
            p = jnp.concatenate([jnp.exp(s[:, t * 128:(t + 1) * 128] - m_new) for t in range(tk // 128)],
                                axis=1).astype(BF16)
            pv = _dot(p, v_ref[:, 2 * h * V_DIM:2 * (h + 1) * V_DIM])
            l_ref[h] = l_ref[h] * corr + pv[:, V_DIM:]
            acc_ref[:, vs] = acc_ref[:, vs] * corr + pv[:, :V_DIM]
            m_ref[h] = m_new

    @pl.when(ki < qi)
    def _():
        process(False)

    @pl.when(ki == qi)
    def _():
        process(True)
        for h in range(N_HEADS):
            vs = slice(h * V_DIM, (h + 1) * V_DIM)
            o_ref[:, vs] = (acc_ref[:, vs] / l_ref[h]).astype(o_ref.dtype)


def _attn_prompt(q_cat, kv, n_seq, seq_len):
    tq = _pick(seq_len, 256, 128)
    tk = tq
    nq = seq_len // tq
    qw = N_HEADS * HEAD_PAD
    vw = N_HEADS * V_DIM
    assert V_DIM == 128 and tk % 128 == 0 and qw == 2 * vw
    pairs = [(a, b) for a in range(nq) for b in range(a + 1)]
    qi_tab = jnp.asarray([a for a, _ in pairs], jnp.int32)
    ki_tab = jnp.asarray([b for _, b in pairs], jnp.int32)

    def q_map(b, p, qi_t, ki_t):
        return (b * nq + qi_t[p], 0)

    def k_map(b, p, qi_t, ki_t):
        return (b * nq + ki_t[p], 0)

    def v_map(b, p, qi_t, ki_t):
        return (b * nq + ki_t[p], 1)

    grid_spec = pltpu.PrefetchScalarGridSpec(
        num_scalar_prefetch=2,
        grid=(n_seq, len(pairs)),
        in_specs=[pl.BlockSpec((tq, qw), q_map),
                  pl.BlockSpec((tk, qw), k_map),
                  pl.BlockSpec((tk, 2 * vw), v_map)],
        out_specs=pl.BlockSpec((tq, vw), q_map),
        scratch_shapes=[pltpu.VMEM((N_HEADS, tq, 128), F32),
                        pltpu.VMEM((N_HEADS, tq, 128), F32),
                        pltpu.VMEM((tq, vw), F32)])
    return pl.pallas_call(
        functools.partial(_attn_prompt_body, tq=tq, tk=tk),
        grid_spec=grid_spec,
        out_shape=jax.ShapeDtypeStruct((n_seq * seq_len, vw), BF16),
        compiler_params=_params(("parallel", "arbitrary"), 0),
        name="attn_prompt",
    )(qi_tab, ki_tab, q_cat, kv, kv)


def _decode_body(pt_ref, qa_ref, qp_ref, sc_ref, sk_ref, lat_hbm, kr_hbm, o_ref,
                 m_ref, l_ref, acc_ref, lat_buf, kr_buf, sem, *, pps, gsz, layer):
    b = pl.program_id(0)
    c = pl.program_id(1)
    nb = pl.num_programs(0)
    nc = pl.num_programs(1)
    step = b * nc + c
    slot = step % 2
    qa = qa_ref[...]
    qp = qp_ref[...]
    nrow = qa.shape[0]

    def page_copies(page, slot_, k):
        return (pltpu.make_async_copy(lat_hbm.at[layer, page], lat_buf.at[slot_, k], sem.at[0, slot_]),
                pltpu.make_async_copy(kr_hbm.at[layer, page], kr_buf.at[slot_, k], sem.at[1, slot_]))

    def start_step(bb, cc, slot_):
        for k in range(pps):
            lat_cp, kr_cp = page_copies(pt_ref[bb, cc * pps + k], slot_, k)
            lat_cp.start(priority=1)
            kr_cp.start(priority=0)

    @pl.when(step == 0)
    def _():
        start_step(0, 0, 0)

    @pl.when(step + 1 < nb * nc)
    def _():
        wrap = c + 1 == nc
        start_step(jnp.where(wrap, b + 1, b), jnp.where(wrap, 0, c + 1), 1 - slot)

    for k in range(pps):
        for cp in page_copies(0, slot, k):
            cp.wait()
    lat_refs = [lat_buf.at[slot, k] for k in range(pps)]
    kr_refs = [kr_buf.at[slot, k] for k in range(pps)]

    def scores(kc, kpt):
        return (_dot_nt(qa, kc) + _dot(qp, kpt)) * ATT_SCALE

    def update(state, s, kc):
        m_old, l_old, acc = state
        m_new = jnp.maximum(m_old, jnp.max(s, axis=-1, keepdims=True))
        corr = jnp.exp(m_old - m_new)
        p = jnp.exp(s - m_new)
        l_new = l_old * corr + jnp.sum(p, axis=-1, keepdims=True)
        return m_new, l_new, acc * corr + _dot(p.astype(BF16), kc)

    def store(state):
        m_ref[...] = jnp.broadcast_to(state[0], m_ref.shape)
        l_ref[...] = jnp.broadcast_to(state[1], l_ref.shape)
        acc_ref[...] = state[2]

    @pl.when(c == 0)
    def _():
        key = lax.broadcasted_iota(jnp.int32, (nrow, PAGE), 1)
        tok = lax.broadcasted_iota(jnp.int32, (nrow, PAGE), 0) // N_HEADS
        kc = sc_ref[...].astype(BF16)
        s = jnp.where(key <= tok, scores(kc, sk_ref[...].astype(BF16)), -jnp.inf)
        init = (jnp.full((nrow, 1), -jnp.inf, F32), jnp.zeros((nrow, 1), F32),
                jnp.zeros((nrow, KV_LORA), F32))
        store(update(init, s, kc))

    def load_group(g):
        ks = range(g * gsz, (g + 1) * gsz)
        kc = jnp.concatenate([lat_refs[k][...].astype(BF16) for k in ks], axis=0)
        kpt = jnp.concatenate([kr_refs[k][...].astype(BF16) for k in ks], axis=1)
        return kc, kpt

    state = (m_ref[:, :1], l_ref[:, :1], acc_ref[...])
    kc, kpt = load_group(0)
    s = scores(kc, kpt)
    for g in range(pps // gsz):
        if g + 1 < pps // gsz:
            kc_next, kpt_next = load_group(g + 1)
            s_next = scores(kc_next, kpt_next)
        state = update(state, s, kc)
        if g + 1 < pps // gsz:
            kc, s = kc_next, s_next
    store(state)

    @pl.when(c == nc - 1)
    def _():
        o_ref[...] = (acc_ref[...] / l_ref[:, :1]).astype(o_ref.dtype)


def _decode(q_abs, q_pe, self_c, self_kt, cache_lat, cache_krt, page_table, layer):
    bs, nrow, _ = q_abs.shape
    n_pages = page_table.shape[1]
    pps = _pick(n_pages, PAGES_PER_STEP, 1)
    gsz = _pick(pps, PAGES_PER_GROUP, 1)
    nc = n_pages // pps

    def fix(b, c, pt):
        return (b, 0, 0)

    in_specs = [pl.BlockSpec((None, nrow, KV_LORA), fix),
                pl.BlockSpec((None, nrow, ROPE), fix),
                pl.BlockSpec((None, PAGE, KV_LORA), fix),
                pl.BlockSpec((None, ROPE, PAGE), fix),
                pl.BlockSpec(memory_space=pl.ANY),
                pl.BlockSpec(memory_space=pl.ANY)]
    grid_spec = pltpu.PrefetchScalarGridSpec(
        num_scalar_prefetch=1,
        grid=(bs, nc),
        in_specs=in_specs,
        out_specs=pl.BlockSpec((None, nrow, KV_LORA), fix),
        scratch_shapes=[pltpu.VMEM((nrow, 128), F32),
                        pltpu.VMEM((nrow, 128), F32),
                        pltpu.VMEM((nrow, KV_LORA), F32),
                        pltpu.VMEM((2, pps, PAGE, KV_LORA), F32),
                        pltpu.VMEM((2, pps, ROPE, PAGE), F32),
                        pltpu.SemaphoreType.DMA((2, 2))])
    vmem = 2 * pps * (PAGE * KV_LORA + ROPE * PAGE) * 4 + (16 << 20)
    return pl.pallas_call(
        functools.partial(_decode_body, pps=pps, gsz=gsz, layer=layer),
        grid_spec=grid_spec,
        out_shape=jax.ShapeDtypeStruct((bs, nrow, KV_LORA), BF16),
        compiler_params=_params(("arbitrary", "arbitrary"), vmem),
        name="attn_decode",
    )(page_table, q_abs, q_pe, self_c, self_kt, cache_lat, cache_krt)


def _even_mixer(x, dims, tabs, cache_lat, cache_kr, page_table, e, w_in, v_norm, ws, bs,
                q_norm, w_uq, kv_norm, w_uk, w_uv, w_out, mix_norm):
    n_p, b_p, l_p, b_s, t_s = dims
    cos_k, sin_k, cos_q, sin_q = tabs
    d = x.shape[1]
    d_a = v_norm.shape[0]
    q_lora = q_norm.shape[0]
    n_s = x.shape[0] - n_p
    half = ROPE // 2
    perm = jnp.concatenate([jnp.arange(half, ROPE), jnp.arange(0, half)])

    h = _rms(x, mix_norm, BF16)
    o_q = 2 * d_a
    assert w_in.shape[2] == o_q + q_lora + KV_LORA + ROPE
    z = _mmw([(h, 0, d)], [[(jnp.swapaxes(w_in, 1, 2), e, 0)]], [], _epi_plain, F32, 1088, 512,
             "even_in", w_t=True)

    a_out, v_rows = _gmlp(z, ws, bs, v_norm, n_p, t_s, d_a)
    qn, c, kpe, ck = _mla_prep(z, o_q, q_norm, kv_norm, cos_k, sin_k, q_lora)

    zpad = jnp.zeros((q_lora, N_HEADS, HEAD_PAD - NOPE - ROPE), F32)
    w1 = jnp.concatenate([w_uq, zpad], axis=2).reshape(q_lora, N_HEADS * HEAD_PAD).astype(BF16)
    w2 = jnp.concatenate([jnp.zeros((q_lora, N_HEADS, NOPE), F32), w_uq[:, :, NOPE:][:, :, perm], zpad],
                         axis=2).reshape(q_lora, N_HEADS * HEAD_PAD).astype(BF16)
    q_cat = _mm(qn, [w1, w2], [(cos_q, "i0"), (sin_q, "i0")], _epi_rope, BF16, 2176, HEAD_PAD, "mla_q")

    ckw = ck.shape[1]
    eye = jnp.eye(ROPE, dtype=F32)
    wk = jnp.zeros((ckw, N_HEADS, HEAD_PAD), F32)
    wk = wk.at[:KV_LORA, :, :NOPE].set(w_uk)
    wk = wk.at[KV_LORA:KV_LORA + ROPE, :, NOPE:NOPE + ROPE].set(jnp.broadcast_to(eye[:, None, :], (ROPE, N_HEADS, ROPE)))
    wv = jnp.zeros((ckw, N_HEADS, 2 * V_DIM), F32)
    wv = wv.at[:KV_LORA, :, :V_DIM].set(w_uv)
    wv = wv.at[KV_LORA + ROPE, :, V_DIM:].set(1.0)
    w_kv = jnp.concatenate([wk.reshape(ckw, N_HEADS * HEAD_PAD), wv.reshape(ckw, N_HEADS * 2 * V_DIM)],
                           axis=1).astype(BF16)
    kv_p = _mm(ck[:n_p], [w_kv], [], _epi_plain, BF16, 2048, 1024, "mla_kv_up")
    o_p = _attn_prompt(q_cat, kv_p, b_p, l_p)

    qs = q_cat[n_p:].reshape(n_s, N_HEADS, HEAD_PAD)
    q_nope_s = qs[:, :, :NOPE].reshape(n_s, N_HEADS * NOPE)
    q_pe_s = qs[:, :, NOPE:NOPE + ROPE].reshape(b_s, t_s * N_HEADS, ROPE)
    w_ukt = jnp.transpose(w_uk, (1, 2, 0)).astype(BF16)
    q_abs = _headmm(q_nope_s, w_ukt, BF16, "mla_q_absorb").reshape(b_s, t_s * N_HEADS, KV_LORA)
    self_c = jnp.zeros((b_s, PAGE, KV_LORA), F32).at[:, :t_s].set(c[n_p:].reshape(b_s, t_s, KV_LORA))
    self_kt = jnp.zeros((b_s, ROPE, PAGE), F32).at[:, :, :t_s].set(
        jnp.swapaxes(kpe[n_p:].reshape(b_s, t_s, ROPE), 1, 2))
    o_lat = _decode(q_abs, q_pe_s, self_c, self_kt, cache_lat, jnp.swapaxes(cache_kr, 2, 3), page_table, e)
    w_uvh = jnp.transpose(w_uv, (1, 0, 2)).astype(BF16)
    o_s = _headmm(o_lat.reshape(n_s, N_HEADS * KV_LORA), w_uvh, BF16, "mla_o_up")

    o_all = jnp.concatenate([o_p, o_s], axis=0)
    kw = a_out.shape[1]
    x = _mm([a_out, o_all], [[w_out[e, :kw].astype(BF16), w_out[e, kw:].astype(BF16)]], [(x, "ij")],
            functools.partial(_epi_resid, scale=1.0), F32, 1088, 512, "even_out")
    return x, v_rows, c, kpe


def _shift_rows(x, j):
    if j == 0:
        return x
    row = lax.broadcasted_iota(jnp.int32, x.shape, 0)
    return jnp.where(row >= j, pltpu.roll(x, j, 0), 0.0)


def _qkv_factor(y, j, tiles_per_part):
    nrm = lax.rsqrt(jnp.sum(y * y, axis=-1, keepdims=True) + EPS)
    part = j // tiles_per_part
    return jnp.where(part == 0, nrm * DK ** -0.5, jnp.where(part == 1, nrm, 1.0))


def _conv_c_prompt_body(b_ref, c_ref, x_ref, w_ref, y_ref, tail_ref):
    xg = c_ref[...] * x_ref[...]
    w = w_ref[...]
    nw = w.shape[0]
    conv = w[nw - 1:nw] * xg
    for j in range(1, nw):
        conv = conv + w[nw - 1 - j:nw - j] * _shift_rows(xg, j)
    y_ref[...] = (b_ref[...] * conv).astype(y_ref.dtype)
    n = xg.shape[0]
    tail_ref[...] = xg[n - 8:, :]


def _conv_c_prompt(z, w, n_seq, seq_len, d_c):
    tc = 256
    nb = d_c // tc
    return pl.pallas_call(
        _conv_c_prompt_body,
        grid=(n_seq, nb),
        in_specs=[pl.BlockSpec((seq_len, tc), lambda b, j: (b, j)),
                  pl.BlockSpec((seq_len, tc), lambda b, j: (b, nb + j)),
                  pl.BlockSpec((seq_len, tc), lambda b, j: (b, 2 * nb + j)),
                  pl.BlockSpec((w.shape[0], tc), lambda b, j: (0, j))],
        out_specs=[pl.BlockSpec((seq_len, tc), lambda b, j: (b, j)),
                   pl.BlockSpec((None, 8, tc), lambda b, j: (b, 0, j))],
        out_shape=[jax.ShapeDtypeStruct((n_seq * seq_len, d_c), BF16),
                   jax.ShapeDtypeStruct((n_seq, 8, d_c), F32)],
        compiler_params=_params(("parallel", "parallel"), 0),
        name="conv_c_prompt",
    )(z, z, z, w)


def _conv_d_prompt_body(x_ref, w_ref, y_ref, tail_ref, *, tiles_per_part):
    j = pl.program_id(1)
    x = x_ref[...]
    w = w_ref[...]
    nw = w.shape[0]
    conv = w[nw - 1:nw] * x
    for s in range(1, nw):
        conv = conv + w[nw - 1 - s:nw - s] * _shift_rows(x, s)
    y = _silu(conv)
    for hh in range(x.shape[1] // DK):
        seg = y[:, hh * DK:(hh + 1) * DK]
        y_ref[:, hh * DK:(hh + 1) * DK] = seg * _qkv_factor(seg, j, tiles_per_part)
    n = x.shape[0]
    tail_ref[...] = x[n - 8:, :]


def _conv_d_prompt(z, w, n_seq, seq_len, col0, d_qkv):
    tc = 2 * DK
    assert (d_qkv // 3) % tc == 0 and col0 % tc == 0
    nb = d_qkv // tc
    off = col0 // tc
    return pl.pallas_call(
        functools.partial(_conv_d_prompt_body, tiles_per_part=nb // 3),
        grid=(n_seq, nb),
        in_specs=[pl.BlockSpec((seq_len, tc), lambda b, j: (b, off + j)),
                  pl.BlockSpec((w.shape[0], tc), lambda b, j: (0, j))],
        out_specs=[pl.BlockSpec((seq_len, tc), lambda b, j: (b, j)),
                   pl.BlockSpec((None, 8, tc), lambda b, j: (b, 0, j))],
        out_shape=[jax.ShapeDtypeStruct((n_seq * seq_len, d_qkv), F32),
                   jax.ShapeDtypeStruct((n_seq, 8, d_qkv), F32)],
        compiler_params=_params(("parallel", "parallel"), 0),
        name="conv_d_prompt",
    )(z, w)


def _conv_c_sample_body(b_ref, c_ref, x_ref, buf_ref, w_ref, y_ref, nbuf_ref):
    t_new = x_ref.shape[0]
    w = w_ref[...]
    nw = w.shape[0]
    xp = [buf_ref[s] for s in range(nw - 1)] + [c_ref[t] * x_ref[t] for t in range(t_new)]
    for t in range(t_new):
        conv = w[0:1] * xp[t]
        for s in range(1, nw):
            conv = conv + w[s:s + 1] * xp[t + s]
        y_ref[t] = (b_ref[t] * conv).astype(y_ref.dtype)
    for s in range(nw - 1):
        nbuf_ref[s] = xp[t_new + s]


def _conv_c_sample(zt, buf_t, w, d_c):
    t_new, n_seq, _ = zt.shape
    tc = 512
    nb = d_c // tc
    nw = w.shape[0]
    return pl.pallas_call(
        _conv_c_sample_body,
        grid=(nb,),
        in_specs=[pl.BlockSpec((t_new, n_seq, tc), lambda j: (0, 0, j)),
                  pl.BlockSpec((t_new, n_seq, tc), lambda j: (0, 0, nb + j)),
                  pl.BlockSpec((t_new, n_seq, tc), lambda j: (0, 0, 2 * nb + j)),
                  pl.BlockSpec((nw - 1, n_seq, tc), lambda j: (0, 0, j)),
                  pl.BlockSpec((nw, tc), lambda j: (0, j))],
        out_specs=[pl.BlockSpec((t_new, n_seq, tc), lambda j: (0, 0, j)),
                   pl.BlockSpec((nw - 1, n_seq, tc), lambda j: (0, 0, j))],
        out_shape=[jax.ShapeDtypeStruct((t_new, n_seq, d_c), BF16),
                   jax.ShapeDtypeStruct((nw - 1, n_seq, d_c), F32)],
        compiler_params=_params(("parallel",), 0),
        name="conv_c_sample",
    )(zt, zt, zt, buf_t, w)


def _conv_d_sample_body(x_ref, buf_ref, w_ref, y_ref, nbuf_ref, *, tiles_per_part):
    j = pl.program_id(0)
    t_new = x_ref.shape[0]
    w = w_ref[...]
    nw = w.shape[0]
    xp = [buf_ref[s] for s in range(nw - 1)] + [x_ref[t] for t in range(t_new)]
    for t in range(t_new):
        conv = w[0:1] * xp[t]
        for s in range(1, nw):
            conv = conv + w[s:s + 1] * xp[t + s]
        y = _silu(conv)
        y_ref[t] = y * _qkv_factor(y, j, tiles_per_part)
    for s in range(nw - 1):
        nbuf_ref[s] = xp[t_new + s]


def _conv_d_sample(zt, buf_t, w, col0, d_qkv):
    t_new, n_seq, _ = zt.shape
    tc = DK
    nb = d_qkv // tc
    off = col0 // tc
    nw = w.shape[0]
    return pl.pallas_call(
        functools.partial(_conv_d_sample_body, tiles_per_part=nb // 3),
        grid=(nb,),
        in_specs=[pl.BlockSpec((t_new, n_seq, tc), lambda j: (0, 0, off + j)),
                  pl.BlockSpec((nw - 1, n_seq, tc), lambda j: (0, 0, j)),
                  pl.BlockSpec((nw, tc), lambda j: (0, j))],
        out_specs=[pl.BlockSpec((t_new, n_seq, tc), lambda j: (0, 0, j)),
                   pl.BlockSpec((nw - 1, n_seq, tc), lambda j: (0, 0, j))],
        out_shape=[jax.ShapeDtypeStruct((t_new, n_seq, d_qkv), F32),
                   jax.ShapeDtypeStruct((nw - 1, n_seq, d_qkv), F32)],
        compiler_params=_params(("parallel",), 0),
        name="conv_d_sample",
    )(zt, buf_t, w)


def _cumsum_rows(x):
    n = x.shape[0]
    row = lax.broadcasted_iota(jnp.int32, x.shape, 0)
    s = 1
    while s < n:
        x = x + jnp.where(row >= s, pltpu.roll(x, s, 0), 0.0)
        s *= 2
    return x


def _split2(x):
    hi = x.astype(BF16)
    return hi, (x - hi.astype(F32)).astype(BF16)


def _hp_dup(a_parts, b_parts):
    a_hi, a_lo = a_parts
    b_hi, b_lo = b_parts
    lhs = jnp.concatenate([a_hi, a_lo], axis=1)
    rhs = jnp.concatenate([b_hi, b_lo, b_hi, jnp.zeros_like(b_hi)], axis=0)
    return _dot(lhs, rhs)


def _delta_body(q_ref, k_ref, v_ref, zg_ref, gate_ref, alog_ref, dt_ref, on_ref, s0_ref,
                o_ref, sout_ref, s_ref, *, chunk, n_valid, group):
    c = pl.program_id(1)
    nc = pl.num_programs(1)
    dup = 2 * chunk == 128
    width = 2 * chunk if dup else chunk

    @pl.when(c == 0)
    def _():
        s_ref[...] = s0_ref[...]

    gate = gate_ref[...]
    beta_all = _sigmoid(gate)
    x = gate + dt_ref[...]
    softplus = jnp.maximum(x, 0.0) + jnp.log(1.0 + jnp.exp(-jnp.abs(x)))
    g_all = -jnp.exp(alog_ref[...]) * softplus
    row128 = lax.broadcasted_iota(jnp.int32, (chunk, 128), 0)
    g_all = jnp.where(row128 < n_valid, g_all, 0.0)
    gcum = _cumsum_rows(g_all)
    eye = (lax.broadcasted_iota(jnp.int32, (128, 128), 0)
           == lax.broadcasted_iota(jnp.int32, (128, 128), 1)).astype(BF16)
    g_rows = jnp.concatenate([gcum, gcum], axis=0) if dup else gcum
    g_hi, g_mid, g_lo = _split3(g_rows)
    gcum_t = _dot_nt(eye, g_hi) + (_dot_nt(eye, g_mid) + _dot_nt(eye, g_lo))

    ri = lax.broadcasted_iota(jnp.int32, (chunk, width), 0)
    ci = lax.broadcasted_iota(jnp.int32, (chunk, width), 1)
    ci = jnp.where(ci >= chunk, ci - chunk, ci)
    incl = ri >= ci
    strict = ri > ci
    ident = (ri == ci).astype(F32)
    on = on_ref[...]
    nil = 1
    while nil < n_valid:
        nil *= 2
    nil = min(nil, chunk)

    for g0 in range(0, N_HEADS, group):
        heads = list(range(g0, g0 + group))
        pw, tm, qk = {}, {}, {}
        for h in heads:
            hs = slice(h * DK, (h + 1) * DK)
            q = q_ref[:, hs]
            k = k_ref[:, hs]
            beta = beta_all[:, h:h + 1]
            gc = gcum[:, N_HEADS + h:N_HEADS + h + 1]
            gr = gcum_t[N_HEADS + h:N_HEADS + h + 1, :]
            decay = jnp.exp(jnp.where(incl, gc - gr, -jnp.inf))
            k_b = k.astype(BF16)
            rhs = jnp.concatenate([k_b, k_b], axis=0) if dup else k_b
            lhs = jnp.concatenate([k * beta, q], axis=0).astype(BF16)
            r = _dot_nt(lhs, rhs)
            a = jnp.where(strict, r[:chunk] * decay, 0.0)
            qk[h] = jnp.where(incl[:, :chunk], r[chunk:, :chunk] * decay[:, :chunk], 0.0)
            pw[h] = -a
            tm[h] = ident + pw[h]
        p = 1
        while 2 * p < nil:
            for h in heads:
                if dup:
                    p2 = _split2(pw[h])
                    pw[h] = _hp_dup(p2, p2)
                else:
                    pw[h] = _dot_hp(pw[h], pw[h])
            for h in heads:
                if dup:
                    tm[h] = tm[h] + _hp_dup(_split2(tm[h]), _split2(pw[h]))
                else:
                    tm[h] = tm[h] + _dot_hp(tm[h], pw[h])
            p *= 2
        uw = {}
        for h in heads:
            hs = slice(h * DK, (h + 1) * DK)
            beta = beta_all[:, h:h + 1]
            gc = gcum[:, N_HEADS + h:N_HEADS + h + 1]
            kb = k_ref[:, hs] * beta
            rhs = jnp.concatenate([v_ref[:, hs] * beta, kb * jnp.exp(gc)], axis=1).astype(BF16)
            uw[h] = _dot(tm[h][:, :chunk].astype(BF16), rhs)
        ws = {}
        for h in heads:
            hs = slice(h * DK, (h + 1) * DK)
            gc = gcum[:, N_HEADS + h:N_HEADS + h + 1]
            lhs = jnp.concatenate([uw[h][:, DV:], q_ref[:, hs] * jnp.exp(gc)], axis=0).astype(BF16)
            ws[h] = _dot(lhs, s_ref[h].astype(BF16))
        for h in heads:
            hs = slice(h * DK, (h + 1) * DK)
            gc = gcum[:, N_HEADS + h:N_HEADS + h + 1]
            g_last = gcum[chunk - 1:chunk, N_HEADS + h:N_HEADS + h + 1]
            v_new = (uw[h][:, :DV] - ws[h][:chunk]).astype(BF16)
            o = ws[h][chunk:] + _dot(qk[h].astype(BF16), v_new)
            k_dec = k_ref[:, hs] * jnp.exp(g_last - gc)
            s_ref[h] = s_ref[h] * jnp.exp(g_last) + _dot_tn(k_dec.astype(BF16), v_new)
            o = o * lax.rsqrt(jnp.mean(o * o, axis=-1, keepdims=True) + EPS) * on
            o_ref[:, hs] = (o * _silu(zg_ref[:, hs])).astype(o_ref.dtype)

    @pl.when(c == nc - 1)
    def _():
        sout_ref[...] = s_ref[...]


def _delta(qkv, zg, zg_blk, gate, a_log, dt_bias, o_norm, s0, n_seq, chunk, n_valid):
    rows = qkv.shape[0]
    nc = rows // (n_seq * chunk)
    hw = N_HEADS * DK

    def rmap(blk):
        return lambda s, c: (s * nc + c, blk)

    alog = jnp.zeros((1, 128), F32).at[0, N_HEADS:2 * N_HEADS].set(a_log)
    dtb = jnp.zeros((1, 128), F32).at[0, N_HEADS:2 * N_HEADS].set(dt_bias)
    fix = lambda s, c: (0, 0)
    return pl.pallas_call(
        functools.partial(_delta_body, chunk=chunk, n_valid=n_valid, group=DELTA_GROUP),
        grid=(n_seq, nc),
        in_specs=[pl.BlockSpec((chunk, hw), rmap(0)),
                  pl.BlockSpec((chunk, hw), rmap(1)),
                  pl.BlockSpec((chunk, hw), rmap(2)),
                  pl.BlockSpec((chunk, hw), rmap(zg_blk)),
                  pl.BlockSpec((chunk, 128), rmap(0)),
                  pl.BlockSpec((1, 128), fix),
                  pl.BlockSpec((1, 128), fix),
                  pl.BlockSpec((1, DV), fix),
                  pl.BlockSpec((None, N_HEADS, DK, DV), lambda s, c: (s, 0, 0, 0))],
        out_specs=[pl.BlockSpec((chunk, hw), rmap(0)),
                   pl.BlockSpec((None, N_HEADS, DK, DV), lambda s, c: (s, 0, 0, 0))],
        out_shape=[jax.ShapeDtypeStruct((rows, hw), BF16),
                   jax.ShapeDtypeStruct((n_seq, N_HEADS, DK, DV), F32)],
        scratch_shapes=[pltpu.VMEM((N_HEADS, DK, DV), F32)],
        compiler_params=_params(("parallel", "arbitrary"), 0),
        name="gated_delta",
    )(qkv, qkv, qkv, zg, gate, alog, dtb, o_norm.reshape(1, DV), s0)


def _odd_mixer(x, dims, buf_c, buf_d, s0_s, w_in, conv_c_w, conv_d_w, a_log, dt_bias, o_norm,
               w_out, layer_idx, mix_norm):
    n_p, b_p, l_p, b_s, t_s = dims
    d = x.shape[1]
    d_c = conv_c_w.shape[1]
    d_qkv = conv_d_w.shape[1]
    n_in = w_in.shape[2]
    o_qkv = 3 * d_c
    o_gate = o_qkv + d_qkv
    o_tail = o_gate + N_HEADS * DV
    n_tail = n_in - o_tail
    assert o_tail % 128 == 0 and n_tail == 2 * N_HEADS and o_gate % (N_HEADS * DV) == 0

    h = _rms(x, mix_norm, BF16)
    z = _mmw([(h, 0, d)], [[(jnp.swapaxes(w_in, 1, 2), layer_idx, 0)]], [], _epi_plain, F32, 1088, 512,
             "odd_in", w_t=True)

    y_c_p, tail_c_p = _conv_c_prompt(z, conv_c_w, b_p, l_p, d_c)
    qkv_p, tail_d_p = _conv_d_prompt(z, conv_d_w, b_p, l_p, o_qkv, d_qkv)
    chunk_p = math.gcd(l_p, DN_CHUNK)
    gate_p = jnp.pad(z[:n_p, o_tail:], ((0, 0), (0, 128 - n_tail)))
    s0_p = jnp.zeros((b_p, N_HEADS, DK, DV), F32)
    o_p, st_p = _delta(qkv_p, z, o_gate // (N_HEADS * DV), gate_p, a_log, dt_bias, o_norm, s0_p,
                       b_p, chunk_p, chunk_p)

    zt = jnp.transpose(z[n_p:].reshape(b_s, t_s, n_in), (1, 0, 2))
    y_c_t, nbuf_c_t = _conv_c_sample(zt, jnp.transpose(buf_c, (1, 0, 2)), conv_c_w, d_c)
    qkv_t, nbuf_d_t = _conv_d_sample(zt, jnp.transpose(buf_d, (1, 0, 2)), conv_d_w, o_qkv, d_qkv)
    pad_t = SAMPLE_CHUNK - t_s
    assert pad_t >= 0

    def to_seq(a_t):
        a = jnp.transpose(a_t, (1, 0, 2))
        a = jnp.pad(a, ((0, 0), (0, pad_t), (0, 0)))
        return a.reshape(b_s * SAMPLE_CHUNK, a.shape[2])

    qkv_s = to_seq(qkv_t)
    zs = z[n_p:].reshape(b_s, t_s, n_in)
    zg_s = jnp.pad(zs[:, :, o_gate:o_tail], ((0, 0), (0, pad_t), (0, 0))).reshape(b_s * SAMPLE_CHUNK, -1)
    gate_s = jnp.pad(zs[:, :, o_tail:], ((0, 0), (0, pad_t), (0, 128 - n_tail))).reshape(b_s * SAMPLE_CHUNK, 128)
    o_s8, st_s = _delta(qkv_s, zg_s, 0, gate_s, a_log, dt_bias, o_norm, s0_s, b_s, SAMPLE_CHUNK, t_s)
    o_s = o_s8.reshape(b_s, SAMPLE_CHUNK, -1)[:, :t_s].reshape(b_s * t_s, -1)
    y_c_s = jnp.transpose(y_c_t, (1, 0, 2)).reshape(b_s * t_s, d_c)

    y_c = jnp.concatenate([y_c_p, y_c_s], axis=0)
    o_all = jnp.concatenate([o_p, o_s], axis=0)
    kw = y_c.shape[1]
    x = _mm([y_c, o_all], [[w_out[layer_idx, :kw].astype(BF16), w_out[layer_idx, kw:].astype(BF16)]],
            [(x, "ij")], functools.partial(_epi_resid, scale=1.0), F32, 1088, 512, "odd_out")
    nw_c = conv_c_w.shape[0] - 1
    nw_d = conv_d_w.shape[0] - 1
    outs = (tail_c_p[:, 8 - nw_c:], jnp.transpose(nbuf_c_t, (1, 0, 2)),
            tail_d_p[:, 8 - nw_d:], jnp.transpose(nbuf_d_t, (1, 0, 2)), st_p, st_s)
    return x, outs


def _rope_tables(pos):
    half = ROPE // 2
    inv = ROPE_THETA ** (-jnp.arange(half, dtype=F32) / half)
    ang = pos[:, None] * inv[None, :]
    cos = jnp.cos(ang)
    sin = jnp.sin(ang)
    n = pos.shape[0]
    cos_f = jnp.concatenate([cos, cos], axis=1)
    sin_f = jnp.concatenate([-sin, sin], axis=1)
    z = jnp.zeros((n, 128 - ROPE), F32)
    cos_k = jnp.concatenate([cos_f, z], axis=1)
    sin_k = jnp.concatenate([sin_f, z], axis=1)
    cos_q = jnp.concatenate([jnp.ones((n, NOPE), F32), cos_f, z], axis=1)
    sin_q = jnp.concatenate([jnp.zeros((n, NOPE), F32), sin_f, z], axis=1)
    return cos_k, sin_k, cos_q, sin_q


def kernel(x_prompt, x_sample, cache_mla_latent, cache_mla_krope, state_conv_c, state_conv_d, state_delta, page_table, ffn1_norm, ffn1_w_gate, ffn1_w_up, ffn1_w_down, mix_norm, ffn2_norm, ffn2_w_gate, ffn2_w_up, ffn2_w_down, even_w_in, gmlp_v_norm, gmlp_ws, gmlp_bs, mla_q_norm, mla_w_uq, mla_kv_norm, mla_w_uk, mla_w_uv, even_w_out, odd_w_in, conv_c_w, conv_d_w, delta_a_log, delta_dt_bias, delta_o_norm, odd_w_out, final_norm):
    b_p, l_p, d = x_prompt.shape
    b_s, t_s, _ = x_sample.shape
    n_p = b_p * l_p
    n_s = b_s * t_s
    depth = ffn1_norm.shape[0]
    dims = (n_p, b_p, l_p, b_s, t_s)
    past_len = page_table.shape[1] * PAGE
    pos = jnp.concatenate([jnp.tile(jnp.arange(l_p, dtype=F32), b_p),
                           jnp.tile(jnp.arange(t_s, dtype=F32) + past_len, b_s)])
    tabs = _rope_tables(pos)

    x = jnp.concatenate([x_prompt.reshape(n_p, d), x_sample.reshape(n_s, d)], axis=0)
    lat, kr, vrow = [], [], []
    cc_p, cc_s, cd_p, cd_s, sd_p, sd_s = [], [], [], [], [], []
    for layer in range(depth):
        x = _ffn_half(x, ffn1_norm[layer], ffn1_w_gate, ffn1_w_up, ffn1_w_down, layer)
        if layer % 2 == 0:
            e = layer // 2
            x, v_rows, c, kpe = _even_mixer(
                x, dims, tabs, cache_mla_latent, cache_mla_krope, page_table, e, even_w_in,
                gmlp_v_norm[e], gmlp_ws[e], gmlp_bs[e], mla_q_norm[e], mla_w_uq[e], mla_kv_norm[e],
                mla_w_uk[e], mla_w_uv[e], even_w_out, mix_norm[layer])
            lat.append(c)
            kr.append(kpe)
            vrow.append(v_rows[n_p:])
        else:
            o = layer // 2
            x, outs = _odd_mixer(
                x, dims, state_conv_c[o], state_conv_d[o], state_delta[o], odd_w_in, conv_c_w[o],
                conv_d_w[o], delta_a_log[o], delta_dt_bias[o], delta_o_norm[o], odd_w_out, o,
                mix_norm[layer])
            cc_p.append(outs[0]); cc_s.append(outs[1]); cd_p.append(outs[2]); cd_s.append(outs[3])
            sd_p.append(outs[4]); sd_s.append(outs[5])
        x = _ffn_half(x, ffn2_norm[layer], ffn2_w_gate, ffn2_w_up, ffn2_w_down, layer)

    y_p = _rms(x, final_norm, F32, 0, n_p)
    y_s = _rms(x, final_norm, F32, n_p, n_s)
    lat = jnp.stack(lat)
    kr = jnp.stack(kr)
    n_e = lat.shape[0]
    return (y_p.reshape(b_p, l_p, d), y_s.reshape(b_s, t_s, d),
            lat[:, :n_p].reshape(n_e, b_p, l_p, -1), kr[:, :n_p].reshape(n_e, b_p, l_p, -1),
            lat[:, n_p:].reshape(n_e, b_s, t_s, -1), kr[:, n_p:].reshape(n_e, b_s, t_s, -1),
            jnp.stack(vrow).reshape(n_e, b_s, t_s, -1),
            jnp.stack(cc_p), jnp.stack(cc_s), jnp.stack(cd_p), jnp.stack(cd_s),
            jnp.stack(sd_p), jnp.stack(sd_s))
```
